```python
import math
import jax, jax.numpy as jnp
from jax import lax
import numpy as np

D_MODEL = 1024
BATCH = 16
SEQ = 2048
DEPTH = 2

S5_WIDTH = D_MODEL // 4
S5_GROUP = 16
S5_GROUPS = S5_WIDTH // S5_GROUP
S5_STATE = 64
GLA_HEADS = 4
GLA_WIDTH = D_MODEL // 4
GLA_DV = GLA_WIDTH // GLA_HEADS
GLA_DK = GLA_DV // 2
GLA_QK = GLA_HEADS * GLA_DK
GLA_RANK = 16
GLA_TAU = 16.0
GLA_CHUNK = 64
SWA_WIDTH = D_MODEL - S5_WIDTH - GLA_WIDTH
SWA_HEAD_DIM = 64
SWA_HEADS = SWA_WIDTH // SWA_HEAD_DIM
SWA_KV_HEADS = 2
SWA_KV = SWA_KV_HEADS * SWA_HEAD_DIM
SWA_WINDOW = 128
SWA_BLOCK = 128
ROT_DIM = SWA_HEAD_DIM // 4
ROPE_THETA = 500000.0
D_FF = 4 * D_MODEL
LN_EPS = 1e-5
DEEPNORM_ALPHA = (2 * DEPTH) ** 0.25
DEEPNORM_BETA = (8 * DEPTH) ** -0.25
NEG_BIG = -1e30

IN_SIZES = (S5_WIDTH, GLA_QK, GLA_QK, GLA_WIDTH, GLA_WIDTH, GLA_RANK, GLA_RANK,
            SWA_WIDTH, SWA_KV, SWA_KV)
D_IN = sum(IN_SIZES)

kernel_name = "hybrid_s5_gla_swa_deepnorm_encoder"


def layer_norm(x, g, b):
    xf = x.astype(jnp.float32)
    mu = jnp.mean(xf, axis=-1, keepdims=True)
    var = jnp.mean(jnp.square(xf - mu), axis=-1, keepdims=True)
    y = (xf - mu) * lax.rsqrt(var + LN_EPS) * g.astype(jnp.float32) + b.astype(jnp.float32)
    return y.astype(x.dtype)


def _linear_rec(lhs, rhs):
    a1, b1 = lhs
    a2, b2 = rhs
    return a1 * a2, a2 * b1 + b2


def s5_mixer(u, a_re, a_im, log_step, b_re, b_im, c_re, c_im, d_skip, w_glu, b_glu):
    f32 = jnp.float32
    bsz, L, _ = u.shape
    u = u.astype(f32).reshape(bsz, L, S5_GROUPS, S5_GROUP)
    lam = lax.complex(a_re.astype(f32), a_im.astype(f32))
    step = jnp.exp(log_step.astype(f32))
    lam_bar = jnp.exp(lam * step)
    b_c = lax.complex(b_re.astype(f32), b_im.astype(f32))
    b_bar = ((lam_bar - 1.0) / lam)[..., None] * b_c
    bu = jnp.einsum('zgph,blgh->zblgp', b_bar, u.astype(jnp.complex64))
    a_f = jnp.broadcast_to(lam_bar[0], (1, L, S5_GROUPS, S5_STATE))
    a_b = jnp.broadcast_to(lam_bar[1], (1, L, S5_GROUPS, S5_STATE))
    _, h_f = lax.associative_scan(_linear_rec, (a_f, bu[0]), axis=1)
    _, h_b = lax.associative_scan(_linear_rec, (a_b, bu[1]), axis=1, reverse=True)
    c_c = lax.complex(c_re.astype(f32), c_im.astype(f32))
    y = (jnp.einsum('ghp,blgp->blgh', c_c[0], h_f).real
         + jnp.einsum('ghp,blgp->blgh', c_c[1], h_b).real
         + d_skip.astype(f32) * u)
    z = jax.nn.gelu(y.reshape(bsz, L, S5_WIDTH))
    val, gate = jnp.split(z @ w_glu.astype(f32) + b_glu.astype(f32), 2, axis=-1)
    return val * jax.nn.sigmoid(gate)


def gla_chunked(q, k, v, log_a, strict):
    bsz, nh, L, dk = q.shape
    dv = v.shape[-1]
    c = GLA_CHUNK
    n = L // c
    q, k, log_a = [t.reshape(bsz, nh, n, c, dk) for t in (q, k, log_a)]
    v = v.reshape(bsz, nh, n, c, dv)
    b = jnp.cumsum(log_a, axis=3)
    b_last = b[:, :, :, -1:]
    q_in = q * jnp.exp(b)
    k_in = k * jnp.exp(-b)
    k_st = k * jnp.exp(b_last - b)
    scores = jnp.einsum('bhnid,bhnjd->bhnij', q_in, k_in)
    mask = jnp.tril(jnp.ones((c, c), dtype=bool), k=-1 if strict else 0)
    o_intra = jnp.einsum('bhnij,bhnjv->bhniv', jnp.where(mask, scores, 0.0), v)
    kv = jnp.einsum('bhnjd,bhnjv->bhndv', k_st, v)
    decay = jnp.exp(b_last[:, :, :, 0])

    def step(state, inp):
        kv_n, dec_n = inp
        return dec_n[..., None] * state + kv_n, state

    init = jnp.zeros((bsz, nh, dk, dv), q.dtype)
    _, states = lax.scan(step, init, (jnp.moveaxis(kv, 2, 0), jnp.moveaxis(decay, 2, 0)))
    states = jnp.moveaxis(states, 0, 2)
    o_inter = jnp.einsum('bhnid,bhndv->bhniv', q_in, states)
    return (o_intra + o_inter).reshape(bsz, nh, L, dv)


def gla_mixer(q, k, v, r, lr_f, lr_b, w_a, b_a, ln_g):
    f32 = jnp.float32
    bsz, L, _ = q.shape

    def heads(t, d):
        return t.astype(f32).reshape(bsz, L, GLA_HEADS, d).transpose(0, 2, 1, 3)

    q = heads(q, GLA_DK) * (GLA_DK ** -0.5)
    k = heads(k, GLA_DK)
    v = heads(v, GLA_DV)
    w_a = w_a.astype(f32)
    b_a = b_a.astype(f32)
    la_f = heads(jax.nn.log_sigmoid(lr_f.astype(f32) @ w_a[0] + b_a[0]) / GLA_TAU, GLA_DK)
    la_b = heads(jax.nn.log_sigmoid(lr_b.astype(f32) @ w_a[1] + b_a[1]) / GLA_TAU, GLA_DK)
    flip = lambda t: jnp.flip(t, axis=2)
    o = (gla_chunked(q, k, v, la_f, strict=False)
         + flip(gla_chunked(flip(q), flip(k), flip(v), flip(la_b), strict=True)))
    mu = jnp.mean(o, axis=-1, keepdims=True)
    var = jnp.mean(jnp.square(o - mu), axis=-1, keepdims=True)
    o = ((o - mu) * lax.rsqrt(var + LN_EPS)).transpose(0, 2, 1, 3).reshape(bsz, L, GLA_WIDTH)
    return o * ln_g.astype(f32) * jax.nn.silu(r.astype(f32))


def rope_partial(t, cos, sin):
    rot, rest = t[..., :ROT_DIM], t[..., ROT_DIM:]
    x1, x2 = rot[..., :ROT_DIM // 2], rot[..., ROT_DIM // 2:]
    rotated = jnp.concatenate([x1 * cos - x2 * sin, x2 * cos + x1 * sin], axis=-1)
    return jnp.concatenate([rotated.astype(t.dtype), rest], axis=-1)


def swa_mixer(q, k, v, sink):
    f32 = jnp.float32
    bsz, L, _ = q.shape
    nb = L // SWA_BLOCK
    grp = SWA_HEADS // SWA_KV_HEADS
    q = q.reshape(bsz, L, SWA_HEADS, SWA_HEAD_DIM)
    k = k.reshape(bsz, L, SWA_KV_HEADS, SWA_HEAD_DIM)
    v = v.reshape(bsz, L, SWA_KV_HEADS, SWA_HEAD_DIM)
    pos = jnp.arange(L, dtype=f32)
    inv_freq = ROPE_THETA ** (-jnp.arange(0, ROT_DIM, 2, dtype=f32) / ROT_DIM)
    ang = pos[:, None] * inv_freq[None, :]
    cos, sin = jnp.cos(ang)[:, None, :], jnp.sin(ang)[:, None, :]
    q = rope_partial(q, cos, sin)
    k = rope_partial(k, cos, sin)

    def band(t):
        tp = jnp.pad(t, ((0, 0), (SWA_BLOCK, SWA_BLOCK), (0, 0), (0, 0)))
        tp = tp.reshape(bsz, nb + 2, SWA_BLOCK, SWA_KV_HEADS, SWA_HEAD_DIM)
        return jnp.concatenate([tp[:, :-2], tp[:, 1:-1], tp[:, 2:]], axis=2)

    kb, vb = band(k), band(v)
    qb = q.reshape(bsz, nb, SWA_BLOCK, SWA_KV_HEADS, grp, SWA_HEAD_DIM)
    s = jnp.einsum('bnqhgd,bnkhd->bnhgqk', qb, kb).astype(f32) * (SWA_HEAD_DIM ** -0.5)
    blk = jnp.arange(nb)[:, None] * SWA_BLOCK
    qpos = blk + jnp.arange(SWA_BLOCK)[None, :]
    kpos = blk - SWA_BLOCK + jnp.arange(3 * SWA_BLOCK)[None, :]
    valid = ((jnp.abs(qpos[:, :, None] - kpos[:, None, :]) <= SWA_WINDOW)
             & (kpos >= 0)[:, None, :] & (kpos < L)[:, None, :])
    s = jnp.where(valid[None, :, None, None], s, NEG_BIG)
    sink_col = jnp.broadcast_to(sink.astype(f32).reshape(1, 1, SWA_KV_HEADS, grp, 1, 1),
                                s.shape[:-1] + (1,))
    p = jax.nn.softmax(jnp.concatenate([s, sink_col], axis=-1), axis=-1)[..., :-1]
    o = jnp.einsum('bnhgqk,bnkhd->bnqhgd', p.astype(vb.dtype), vb)
    return o.reshape(bsz, L, SWA_WIDTH)


def hybrid_mixer(x, w_in, a_re, a_im, log_step, b_re, b_im, c_re, c_im, d_skip,
                 w_glu, b_glu, gla_w_a, gla_b_a, gla_ln_g, swa_sink, w_out):
    h = x @ w_in
    points = np.cumsum(IN_SIZES)[:-1].tolist()
    (s5_u, g_q, g_k, g_v, g_r, g_lf, g_lb, a_q, a_k, a_v) = jnp.split(h, points, axis=-1)
    y_a = s5_mixer(s5_u, a_re, a_im, log_step, b_re, b_im, c_re, c_im, d_skip, w_glu, b_glu)
    y_b = gla_mixer(g_q, g_k, g_v, g_r, g_lf, g_lb, gla_w_a, gla_b_a, gla_ln_g)
    y_c = swa_mixer(a_q, a_k, a_v, swa_sink)
    y = jnp.concatenate([y_a.astype(x.dtype), y_b.astype(x.dtype), y_c.astype(x.dtype)], axis=-1)
    return y @ w_out


def setup_inputs(seed: int = 0) -> dict:
    key = jax.random.key(seed)
    ks = jax.random.split(key, 24)
    f32 = jnp.float32
    nrm = lambda k, shape, scale: jax.random.normal(k, shape, f32) * scale
    L2 = (DEPTH, 2, S5_GROUPS, S5_STATE)
    x = nrm(ks[0], (BATCH, SEQ, D_MODEL), 1.0)
    w_in = nrm(ks[1], (DEPTH, D_MODEL, D_IN), D_MODEL ** -0.5)
    s5_a_re = -0.5 + nrm(ks[2], L2, 0.01)
    s5_a_im = math.pi * jnp.arange(S5_STATE, dtype=f32) + nrm(ks[3], L2, 0.01)
    s5_log_step = jax.random.uniform(ks[4], L2, f32, math.log(1e-3), math.log(1e-1))
    s5_b_re = nrm(ks[5], L2 + (S5_GROUP,), (2 * S5_GROUP) ** -0.5)
    s5_b_im = nrm(ks[6], L2 + (S5_GROUP,), (2 * S5_GROUP) ** -0.5)
    s5_c_re = nrm(ks[7], (DEPTH, 2, S5_GROUPS, S5_GROUP, S5_STATE), S5_STATE ** -0.5)
    s5_c_im = nrm(ks[8], (DEPTH, 2, S5_GROUPS, S5_GROUP, S5_STATE), S5_STATE ** -0.5)
    s5_d = nrm(ks[9], (DEPTH, S5_GROUPS, S5_GROUP), 1.0)
    s5_w_glu = nrm(ks[10], (DEPTH, S5_WIDTH, 2 * S5_WIDTH), S5_WIDTH ** -0.5)
    s5_b_glu = nrm(ks[11], (DEPTH, 2 * S5_WIDTH), 0.01)
    gla_w_a = nrm(ks[12], (DEPTH, 2, GLA_RANK, GLA_QK), GLA_RANK ** -0.5)
    gla_b_a = nrm(ks[13], (DEPTH, 2, GLA_QK), 0.01)
    gla_ln_g = 1.0 + nrm(ks[14], (DEPTH, GLA_WIDTH), 0.02)
    swa_sink = nrm(ks[15], (DEPTH, SWA_HEADS), 0.5)
    w_out = nrm(ks[16], (DEPTH, D_MODEL, D_MODEL), D_MODEL ** -0.5 * DEEPNORM_BETA)
    ln1_g = 1.0 + nrm(ks[17], (DEPTH, D_MODEL), 0.02)
    ln1_b = nrm(ks[18], (DEPTH, D_MODEL), 0.01)
    w_ff1 = nrm(ks[19], (DEPTH, D_MODEL, D_FF), D_MODEL ** -0.5)
    w_ff2 = nrm(ks[20], (DEPTH, D_FF, D_MODEL), D_FF ** -0.5 * DEEPNORM_BETA)
    ln2_g = 1.0 + nrm(ks[21], (DEPTH, D_MODEL), 0.02)
    ln2_b = nrm(ks[22], (DEPTH, D_MODEL), 0.01)
    return {"x": x, "w_in": w_in, "s5_a_re": s5_a_re, "s5_a_im": s5_a_im,
            "s5_log_step": s5_log_step, "s5_b_re": s5_b_re, "s5_b_im": s5_b_im,
            "s5_c_re": s5_c_re, "s5_c_im": s5_c_im, "s5_d": s5_d,
            "s5_w_glu": s5_w_glu, "s5_b_glu": s5_b_glu, "gla_w_a": gla_w_a,
            "gla_b_a": gla_b_a, "gla_ln_g": gla_ln_g, "swa_sink": swa_sink,
            "w_out": w_out, "ln1_g": ln1_g, "ln1_b": ln1_b, "w_ff1": w_ff1,
            "w_ff2": w_ff2, "ln2_g": ln2_g, "ln2_b": ln2_b}


def reference(x, w_in, s5_a_re, s5_a_im, s5_log_step, s5_b_re, s5_b_im, s5_c_re, s5_c_im,
              s5_d, s5_w_glu, s5_b_glu, gla_w_a, gla_b_a, gla_ln_g, swa_sink, w_out,
              ln1_g, ln1_b, w_ff1, w_ff2, ln2_g, ln2_b):
    for l in range(DEPTH):
        mix = hybrid_mixer(x, w_in[l], s5_a_re[l], s5_a_im[l], s5_log_step[l],
                           s5_b_re[l], s5_b_im[l], s5_c_re[l], s5_c_im[l], s5_d[l],
                           s5_w_glu[l], s5_b_glu[l], gla_w_a[l], gla_b_a[l], gla_ln_g[l],
                           swa_sink[l], w_out[l])
        x = layer_norm(DEEPNORM_ALPHA * x + mix, ln1_g[l], ln1_b[l])
        hid = jnp.square(jax.nn.relu(x @ w_ff1[l]))
        x = layer_norm(DEEPNORM_ALPHA * x + hid @ w_ff2[l], ln2_g[l], ln2_b[l])
    return x
```

```python
import functools
import math

import jax
import jax.numpy as jnp
import numpy as np
from jax import lax
from jax.experimental import pallas as pl
from jax.experimental.pallas import tpu as pltpu

F32 = jnp.float32
BF16 = jnp.bfloat16

D_MODEL = 1024
S5_WIDTH = 256
S5_GROUP = 16
S5_GROUPS = 16
S5_STATE = 64
S5_CHUNK = 16
GLA_HEADS = 4
GLA_DK = 32
GLA_DV = 64
GLA_QK = 128
GLA_WIDTH = 256
GLA_RANK = 16
GLA_TAU = 16.0
GLA_CHUNK = 64
SWA_WIDTH = 512
SWA_HEAD_DIM = 64
SWA_HEADS = 8
SWA_KV_HEADS = 2
SWA_GROUP = SWA_HEADS // SWA_KV_HEADS
SWA_KV = 128
SWA_WINDOW = 128
SWA_BLOCK = 128
ROT_DIM = 16
ROPE_THETA = 500000.0
D_FF = 4096
LN_EPS = 1e-5
NEG_BIG = -1e30

U_COLS = S5_WIDTH
G_COLS = GLA_QK + GLA_QK + GLA_WIDTH + GLA_WIDTH + 128
A_COLS = SWA_WIDTH + SWA_KV + SWA_KV
IN_COLS = U_COLS + G_COLS + A_COLS

VMEM_LIMIT = 56 * 1024 * 1024
ROW_TILE = 512
FF_CHUNK = 512

_NT = (((1,), (1,)), ((), ()))
_TN = (((0,), (0,)), ((), ()))


def _mm(a, b):
    return jnp.dot(a, b, preferred_element_type=F32)


def _params(*sem):
    return pltpu.CompilerParams(dimension_semantics=sem, vmem_limit_bytes=VMEM_LIMIT)


def _resident(shape):
    nd = len(shape)
    return pl.BlockSpec(shape, lambda *_: (0,) * nd)


def _inproj_kernel(x_ref, w_ref, u_ref, g_ref, a_ref):
    x = x_ref[...].astype(BF16)
    u_ref[...] = _mm(x, w_ref[:, 0:U_COLS])
    g_ref[...] = _mm(x, w_ref[:, U_COLS:U_COLS + G_COLS])
    a_ref[...] = _mm(x, w_ref[:, U_COLS + G_COLS:IN_COLS])


def _inproj(x2d, w_packed):
    t = x2d.shape[0]
    tm = min(ROW_TILE, t)
    row = lambda n: pl.BlockSpec((tm, n), lambda i: (i, 0))
    return pl.pallas_call(
        _inproj_kernel,
        grid=(t // tm,),
        in_specs=[row(D_MODEL), _resident((D_MODEL, IN_COLS))],
        out_specs=[row(U_COLS), row(G_COLS), row(A_COLS)],
        out_shape=[jax.ShapeDtypeStruct((t, U_COLS), F32),
                   jax.ShapeDtypeStruct((t, G_COLS), F32),
                   jax.ShapeDtypeStruct((t, A_COLS), F32)],
        compiler_params=_params("parallel"),
        name="inproj",
    )(x2d, w_packed)


def _s5_matrices(a_re, a_im, log_step, b_re, b_im, c_re, c_im, d_skip):
    hp = lax.Precision.HIGHEST
    c = S5_CHUNK
    lam = lax.complex(a_re.astype(F32), a_im.astype(F32))
    step = jnp.exp(log_step.astype(F32))
    ls = lam * step
    lam_bar = jnp.exp(ls)
    b_bar = ((lam_bar - 1.0) / lam)[..., None] * lax.complex(b_re.astype(F32), b_im.astype(F32))
    c_c = lax.complex(c_re.astype(F32), c_im.astype(F32))
    kk = jnp.arange(c + 1, dtype=F32)
    pw = jnp.exp(kk[:, None, None, None] * ls[None])

    kern = jnp.einsum('zgop,kzgp,zgpi->zkgoi', c_c, pw[:c], b_bar, precision=hp).real
    kf, kb = kern[0], kern[1]
    lag = jnp.concatenate([kb[1:][::-1], (kf[0] + kb[0])[None], kf[1:]], axis=0)
    jj = np.arange(c)[:, None]
    ii = np.arange(c)[None, :]
    t4 = lag[(ii - jj) + c - 1]
    t_mat = jnp.transpose(t4, (2, 0, 4, 1, 3)).reshape(S5_GROUPS, c * S5_GROUP, c * S5_GROUP)

    sf = pw[:c][::-1, 0][..., None] * b_bar[0][None]
    sb = pw[:c][:, 1][..., None] * b_bar[1][None]
    s4 = jnp.stack([sf.real, sb.real, sf.imag, sb.imag], axis=0)
    s_mat = jnp.transpose(s4, (2, 1, 4, 0, 3)).reshape(S5_GROUPS, c * S5_GROUP, 4 * S5_STATE)

    of = c_c[0][None] * pw[1:c + 1, 0][:, :, None, :]
    ob = c_c[1][None] * pw[1:c + 1][::-1, 1][:, :, None, :]
    o4 = jnp.stack([of.real, ob.real, -of.imag, -ob.imag], axis=0)
    o_mat = jnp.transpose(o4, (2, 0, 4, 1, 3)).reshape(S5_GROUPS, 4 * S5_STATE, c * S5_GROUP)

    lam_c = pw[c]
    a_row = jnp.stack([jnp.concatenate([lam_c[0].real, lam_c[1].real], axis=-1),
                       jnp.concatenate([lam_c[0].imag, lam_c[1].imag], axis=-1)], axis=1)
    d_row = jnp.tile(d_skip.astype(F32), (1, c))[:, None, :]
    return t_mat.astype(BF16), s_mat.astype(BF16), o_mat.astype(BF16), a_row, d_row


def _gelu_tanh(x):
    return 0.5 * x * (1.0 + jnp.tanh(math.sqrt(2.0 / math.pi) * (x + 0.044715 * (x * x * x))))


def _s5_kernel(u_ref, t_ref, s_ref, o_ref, a_ref, d_ref, z_ref, c_scr, ef_scr, eb_scr, *, nb):
    rows = u_ref.shape[1]
    nchunk = rows // nb
    u = u_ref[0]
    ub = u.astype(BF16)
    c_scr[...] = _mm(ub, s_ref[0])
    ar = jnp.broadcast_to(a_ref[0, 0:1, :], (nb, 128))
    ai = jnp.broadcast_to(a_ref[0, 1:2, :], (nb, 128))
    fwd = (lax.broadcasted_iota(jnp.int32, (nb, 256), 1) % 128) < S5_STATE

    def step(k, s):
        r0 = pl.multiple_of(k * nb, nb)
        r1 = pl.multiple_of((nchunk - 1 - k) * nb, nb)
        ef_scr[pl.ds(r0, nb), :] = s
        eb_scr[pl.ds(r1, nb), :] = s
        cc = jnp.where(fwd, c_scr[pl.ds(r0, nb), :], c_scr[pl.ds(r1, nb), :])
        sre, sim = s[:, :128], s[:, 128:]
        nxt = jnp.concatenate([ar * sre - ai * sim, ar * sim + ai * sre], axis=1)
        return nxt + cc

    lax.fori_loop(0, nchunk, step, jnp.zeros((nb, 256), F32))
    fwd_all = (lax.broadcasted_iota(jnp.int32, (rows, 256), 1) % 128) < S5_STATE
    enter = jnp.where(fwd_all, ef_scr[...], eb_scr[...]).astype(BF16)
    y = _mm(ub, t_ref[0]) + _mm(enter, o_ref[0]) + d_ref[0] * u
    z_ref[0] = _gelu_tanh(y)


def _s5(u_t, mats, nb):
    t_mat, s_mat, o_mat, a_row, d_row = mats
    g, rows, w = u_t.shape
    per_g = lambda shape: pl.BlockSpec((1,) + shape, lambda i: (i, 0, 0))
    return pl.pallas_call(
        functools.partial(_s5_kernel, nb=nb),
        grid=(g,),
        in_specs=[per_g((rows, w)), per_g((w, w)), per_g((w, w)), per_g((w, w)),
                  per_g((2, 128)), per_g((1, w))],
        out_specs=per_g((rows, w)),
        out_shape=jax.ShapeDtypeStruct((g, rows, w), F32),
        scratch_shapes=[pltpu.VMEM((rows, w), F32)] * 3,
        compiler_params=_params("parallel"),
        name="s5",
    )(u_t, t_mat, s_mat, o_mat, a_row, d_row)


def _gla_constants():
    r = np.arange(256)
    same = (r[:, None] // GLA_CHUNK) == (r[None, :] // GLA_CHUNK)
    prefix = same & (r[None, :] <= r[:, None])
    suffix = same & (r[None, :] >= r[:, None])
    avg = ((r[:, None] // GLA_DV) == (r[None, :] // GLA_DV)) / float(GLA_DV)
    as_bf = lambda m: jnp.asarray(m.astype(np.float32), BF16)
    return as_bf(prefix), as_bf(suffix), as_bf(same), as_bf(avg)


def _log_sigmoid(x):
    return jnp.minimum(x, 0.0) - jnp.log1p(jnp.exp(-jnp.abs(x)))


def _split_mm(a, b):
    hi = a.astype(BF16)
    lo = (a - hi.astype(F32)).astype(BF16)
    return _mm(hi, b) + _mm(lo, b)


def _chunk_sums(mat, la):
    hi = la.astype(BF16)
    lo = (la - hi.astype(F32)).astype(BF16)
    out = []
    for i in range(la.shape[0] // 256):
        sl = slice(i * 256, (i + 1) * 256)
        out.append(_mm(mat, hi[sl]) + _mm(mat, lo[sl]))
    return jnp.concatenate(out, axis=0)


def _gla_kernel(g_ref, waf_ref, wab_ref, ba_ref, pre_ref, suf_ref, tot_ref, avg_ref, lng_ref, o_ref,
                qf_s, kfi_s, kfs_s, qb_s, kbi_s, kbs_s, v_s, totf_s, totb_s, of_s, ob_s):
    seq = g_ref.shape[1]
    nchunk = seq // GLA_CHUNK
    c = GLA_CHUNK
    q = g_ref[0, :, 0:128] * (GLA_DK ** -0.5)
    k = g_ref[0, :, 128:256]
    lfb = g_ref[0, :, 768:896].astype(BF16)
    la_f = _log_sigmoid(_mm(lfb, waf_ref[...]) + ba_ref[0:1, :]) * (1.0 / GLA_TAU)
    la_b = _log_sigmoid(_mm(lfb, wab_ref[...]) + ba_ref[1:2, :]) * (1.0 / GLA_TAU)

    bf = _chunk_sums(pre_ref[...], la_f)
    tf = _chunk_sums(tot_ref[...], la_f)
    qf_s[...] = (q * jnp.exp(bf)).astype(BF16)
    kfi_s[...] = (k * jnp.exp(-bf)).astype(BF16)
    kfs_s[...] = (k * jnp.exp(tf - bf)).astype(BF16)
    totf_s[...] = tf
    bs = _chunk_sums(suf_ref[...], la_b)
    tb = _chunk_sums(tot_ref[...], la_b)
    qb_s[...] = (q * jnp.exp(bs)).astype(BF16)
    kbi_s[...] = (k * jnp.exp(-bs)).astype(BF16)
    kbs_s[...] = (k * jnp.exp(tb - bs)).astype(BF16)
    totb_s[...] = tb
    v_s[...] = g_ref[0, :, 256:512].astype(BF16)

    r256 = lax.broadcasted_iota(jnp.int32, (4 * c, 128), 0)
    l128 = lax.broadcasted_iota(jnp.int32, (4 * c, 128), 1)
    head_k = (r256 // c) == (l128 // GLA_DK)
    r256v = lax.broadcasted_iota(jnp.int32, (4 * c, 256), 0)
    l256v = lax.broadcasted_iota(jnp.int32, (4 * c, 256), 1)
    head_v = (r256v // c) == (l256v // GLA_DV)
    qi = lax.broadcasted_iota(jnp.int32, (c, 4 * c), 0)
    kj = lax.broadcasted_iota(jnp.int32, (c, 4 * c), 1) % c
    causal_f = kj <= qi
    causal_b = kj > qi

    def one_dir(r0, q_s, ki_s, ks_s, tot_s, out_s, causal, state):
        qq = q_s[pl.ds(r0, c), :]
        ki = ki_s[pl.ds(r0, c), :]
        ks = ks_s[pl.ds(r0, c), :]
        vv = v_s[pl.ds(r0, c), :]
        kexp = jnp.where(head_k, jnp.concatenate([ki] * 4, axis=0), jnp.zeros((), BF16))
        vexp = jnp.where(head_v, jnp.concatenate([vv] * 4, axis=0), jnp.zeros((), BF16))
        sc = lax.dot_general(qq, kexp, _NT, preferred_element_type=F32)
        pm = jnp.where(causal, sc, 0.0).astype(BF16)
        o = _mm(pm, vexp) + lax.dot_general(qq, state.astype(BF16), _NT, preferred_element_type=F32)
        out_s[pl.ds(r0, c), :] = o
        kv_t = lax.dot_general(vv, ks, _TN, preferred_element_type=F32)
        dec = jnp.exp(tot_s[pl.ds(r0, 1), :])
        return dec * state + jnp.where(head_k, kv_t, 0.0)

    def body(n, carry):
        st_f, st_b = carry
        r0 = pl.multiple_of(n * c, c)
        r1 = pl.multiple_of((nchunk - 1 - n) * c, c)
        st_f = one_dir(r0, qf_s, kfi_s, kfs_s, totf_s, of_s, causal_f, st_f)
        st_b = one_dir(r1, qb_s, kbi_s, kbs_s, totb_s, ob_s, causal_b, st_b)
        return st_f, st_b

    zero = jnp.zeros((GLA_WIDTH, GLA_QK), F32)
    lax.fori_loop(0, nchunk, body, (zero, zero))

    o = of_s[...] + ob_s[...]
    avg = avg_ref[...]
    mu = _split_mm(o, avg)
    dlt = o - mu
    var = _split_mm(dlt * dlt, avg)
    on = dlt * lax.rsqrt(var + LN_EPS)
    r = g_ref[0, :, 512:768]
    o_ref[0] = on * lng_ref[...] * (r * (1.0 / (1.0 + jnp.exp(-r))))


def _gla(g3, waf, wab, ba, consts, ln_g):
    b, seq, _ = g3.shape
    pre, suf, tot, avg = consts
    bf = lambda w: pltpu.VMEM((seq, w), BF16)
    ff = lambda w: pltpu.VMEM((seq, w), F32)
    return pl.pallas_call(
        _gla_kernel,
        grid=(b,),
        in_specs=[pl.BlockSpec((1, seq, G_COLS), lambda i: (i, 0, 0)),
                  _resident((128, 128)), _resident((128, 128)), _resident((2, 128)),
                  _resident((256, 256)), _resident((256, 256)), _resident((256, 256)),
                  _resident((256, 256)), _resident((1, 256))],
        out_specs=pl.BlockSpec((1, seq, GLA_WIDTH), lambda i: (i, 0, 0)),
        out_shape=jax.ShapeDtypeStruct((b, seq, GLA_WIDTH), F32),
        scratch_shapes=[bf(128)] * 6 + [bf(256), ff(128), ff(128), ff(256), ff(256)],
        compiler_params=_params("parallel"),
        name="gla",
    )(g3, waf, wab, ba, pre, suf, tot, avg, ln_g)


def _rope_tables(seq):
    pos = jnp.arange(seq, dtype=F32)
    inv_freq = ROPE_THETA ** (-jnp.arange(0, ROT_DIM, 2, dtype=F32) / ROT_DIM)
    ang = pos[:, None] * inv_freq[None, :]
    cos, sin = jnp.cos(ang), jnp.sin(ang)
    half = ROT_DIM // 2
    pad = jnp.zeros((seq, SWA_HEAD_DIM - ROT_DIM), F32)
    zeros = jnp.zeros((seq, half), F32)
    c_head = jnp.concatenate([cos, cos, pad + 1.0], axis=1)
    up_head = jnp.concatenate([-sin, zeros, pad], axis=1)
    dn_head = jnp.concatenate([zeros, sin, pad], axis=1)
    two = lambda t: jnp.concatenate([t, t], axis=1)
    return two(c_head), two(up_head), two(dn_head)


def _swa_kernel(sink_ref, a_ref, cos_ref, up_ref, dn_ref, o_ref, q_s, k_s, v_s):
    seq = a_ref.shape[1]
    blk = SWA_BLOCK
    hd = SWA_HEAD_DIM
    cosv, upv, dnv = cos_ref[...], up_ref[...], dn_ref[...]

    def rope(x):
        return x * cosv + pltpu.roll(x, 128 - ROT_DIM // 2, 1) * upv + pltpu.roll(x, ROT_DIM // 2, 1) * dnv

    for slab in range(SWA_WIDTH // 128):
        qr = (rope(a_ref[0, :, slab * 128:(slab + 1) * 128]) * (hd ** -0.5)).astype(BF16)
        q_s[2 * slab] = qr[:, :hd]
        q_s[2 * slab + 1] = qr[:, hd:]
    kr = rope(a_ref[0, :, SWA_WIDTH:SWA_WIDTH + SWA_KV]).astype(BF16)
    vr = a_ref[0, :, SWA_WIDTH + SWA_KV:A_COLS].astype(BF16)
    zpad = jnp.zeros((blk, hd), BF16)
    for kh in range(SWA_KV_HEADS):
        k_s[kh, 0:blk, :] = zpad
        k_s[kh, blk + seq:2 * blk + seq, :] = zpad
        v_s[kh, 0:blk, :] = zpad
        v_s[kh, blk + seq:2 * blk + seq, :] = zpad
        k_s[kh, blk:blk + seq, :] = kr[:, kh * hd:(kh + 1) * hd]
        v_s[kh, blk:blk + seq, :] = vr[:, kh * hd:(kh + 1) * hd]

    qi = lax.broadcasted_iota(jnp.int32, (blk, 3 * blk), 0)
    kk = lax.broadcasted_iota(jnp.int32, (blk, 3 * blk), 1)
    band = jnp.abs(qi + blk - kk) <= SWA_WINDOW

    def body(n, carry):
        r0 = pl.multiple_of(n * blk, blk)
        kpos = kk + (n - 1) * blk
        valid = band & (kpos >= 0) & (kpos < seq)
        for kh in range(SWA_KV_HEADS):
            k3 = k_s[kh, pl.ds(r0, 3 * blk), :]
            v3 = v_s[kh, pl.ds(r0, 3 * blk), :]
            outs = []
            for g in range(SWA_GROUP):
                h = kh * SWA_GROUP + g
                sink = sink_ref[h]
                qq = q_s[h, pl.ds(r0, blk), :]
                s = lax.dot_general(qq, k3, _NT, preferred_element_type=F32)
                s = jnp.where(valid, s, NEG_BIG)
                m = jnp.maximum(jnp.max(s, axis=1, keepdims=True), sink)
                p = jnp.exp(s - m)
                den = jnp.sum(p, axis=1, keepdims=True) + jnp.exp(sink - m)
                outs.append(_mm(p.astype(BF16), v3) / den)
            o_ref[0, pl.ds(r0, blk), kh * SWA_GROUP * hd:(kh + 1) * SWA_GROUP * hd] = jnp.concatenate(outs, axis=1)
        return carry

    lax.fori_loop(0, seq // blk, body, 0)


def _swa(a3, sink, tables):
    b, seq, _ = a3.shape
    cos_t, up_t, dn_t = tables
    tab = pl.BlockSpec((seq, 128), lambda i, s: (0, 0))
    grid_spec = pltpu.PrefetchScalarGridSpec(
        num_scalar_prefetch=1,
        grid=(b,),
        in_specs=[pl.BlockSpec((1, seq, A_COLS), lambda i, s: (i, 0, 0)), tab, tab, tab],
        out_specs=pl.BlockSpec((1, seq, SWA_WIDTH), lambda i, s: (i, 0, 0)),
        scratch_shapes=[pltpu.VMEM((SWA_HEADS, seq, SWA_HEAD_DIM), BF16),
                        pltpu.VMEM((SWA_KV_HEADS, seq + 2 * SWA_BLOCK, SWA_HEAD_DIM), BF16),
                        pltpu.VMEM((SWA_KV_HEADS, seq + 2 * SWA_BLOCK, SWA_HEAD_DIM), BF16)],
    )
    return pl.pallas_call(
        _swa_kernel,
        grid_spec=grid_spec,
        out_shape=jax.ShapeDtypeStruct((b, seq, SWA_WIDTH), F32),
        compiler_params=_params("parallel"),
        name="swa",
    )(sink, a3, cos_t, up_t, dn_t)


def _layer_norm(x, g, b):
    mu = jnp.mean(x, axis=-1, keepdims=True)
    d = x - mu
    var = jnp.mean(d * d, axis=-1, keepdims=True)
    return d * lax.rsqrt(var + LN_EPS) * g + b


def _out_ffn_kernel(x_ref, z_ref, yb_ref, yc_ref, wglu_ref, bglu_ref, woa_ref, wob_ref, woc_ref,
                    g1_ref, b1_ref, w1_ref, w2_ref, g2_ref, b2_ref, o_ref, acc_ref, *, alpha):
    glu = _mm(z_ref[...].astype(BF16), wglu_ref[...]) + bglu_ref[...]
    gate = glu[:, S5_WIDTH:]
    ya = glu[:, :S5_WIDTH] * (1.0 / (1.0 + jnp.exp(-gate)))
    mix = (_mm(ya.astype(BF16), woa_ref[...]) + _mm(yb_ref[...].astype(BF16), wob_ref[...])
           + _mm(yc_ref[...].astype(BF16), woc_ref[...]))
    x1 = _layer_norm(alpha * x_ref[...] + mix, g1_ref[...], b1_ref[...])
    x1b = x1.astype(BF16)
    for ci in range(D_FF // FF_CHUNK):
        sl = slice(ci * FF_CHUNK, (ci + 1) * FF_CHUNK)
        hid = jnp.maximum(_mm(x1b, w1_ref[:, sl]), 0.0)
        part = _mm((hid * hid).astype(BF16), w2_ref[sl, :])
        if ci == 0:
            acc_ref[...] = part
        else:
            acc_ref[...] += part
    o_ref[...] = _layer_norm(alpha * x1 + acc_ref[...], g2_ref[...], b2_ref[...])


def _out_ffn(x2d, z2d, yb2d, yc2d, wglu, bglu, woa, wob, woc, g1, b1, w1, w2, g2, b2, alpha):
    t = x2d.shape[0]
    tm = min(ROW_TILE, t)
    row = lambda n: pl.BlockSpec((tm, n), lambda i: (i, 0))
    once = lambda shape: pl.BlockSpec(shape, lambda i: (0, 0), pipeline_mode=pl.Buffered(1))
    return pl.pallas_call(
        functools.partial(_out_ffn_kernel, alpha=alpha),
        grid=(t // tm,),
        in_specs=[row(D_MODEL), row(S5_WIDTH), row(GLA_WIDTH), row(SWA_WIDTH),
                  once((S5_WIDTH, 2 * S5_WIDTH)), once((1, 2 * S5_WIDTH)),
                  once((S5_WIDTH, D_MODEL)), once((GLA_WIDTH, D_MODEL)), once((SWA_WIDTH, D_MODEL)),
                  once((1, D_MODEL)), once((1, D_MODEL)),
                  once((D_MODEL, D_FF)), once((D_FF, D_MODEL)),
                  once((1, D_MODEL)), once((1, D_MODEL))],
        out_specs=row(D_MODEL),
        out_shape=jax.ShapeDtypeStruct((t, D_MODEL), F32),
        scratch_shapes=[pltpu.VMEM((tm, D_MODEL), F32)],
        compiler_params=_params("parallel"),
        name="out_ffn",
    )(x2d, z2d, yb2d, yc2d, wglu, bglu, woa, wob, woc, g1, b1, w1, w2, g2, b2)


def _pack_w_in(w_in):
    points = np.cumsum([S5_WIDTH, GLA_QK, GLA_QK, GLA_WIDTH, GLA_WIDTH, GLA_RANK, GLA_RANK,
                        SWA_WIDTH, SWA_KV]).tolist()
    s5, gq, gk, gv, gr, glf, glb, aq, ak, av = jnp.split(w_in, points, axis=-1)
    pad = jnp.zeros((w_in.shape[0], 128 - 2 * GLA_RANK), w_in.dtype)
    return jnp.concatenate([s5, gq, gk, gv, gr, glf, glb, pad, aq, ak, av], axis=-1).astype(BF16)


def _pad_gate_w(w_a):
    z = jnp.zeros((128, GLA_QK), F32)
    waf = z.at[0:GLA_RANK].set(w_a[0].astype(F32))
    wab = z.at[GLA_RANK:2 * GLA_RANK].set(w_a[1].astype(F32))
    return waf.astype(BF16), wab.astype(BF16)


def _layer(x2d, bsz, seq, alpha, w_in, a_re, a_im, log_step, b_re, b_im, c_re, c_im, d_skip,
           w_glu, b_glu, gla_w_a, gla_b_a, gla_ln_g, swa_sink, w_out, ln1_g, ln1_b, w_ff1, w_ff2,
           ln2_g, ln2_b, gla_consts, rope_tabs):
    t = bsz * seq
    u, g, a = _inproj(x2d, _pack_w_in(w_in))

    nch = seq // S5_CHUNK
    u_t = u.reshape(bsz, nch, S5_CHUNK, S5_GROUPS, S5_GROUP).transpose(3, 1, 0, 2, 4)
    u_t = u_t.reshape(S5_GROUPS, nch * bsz, S5_CHUNK * S5_GROUP)
    z_t = _s5(u_t, _s5_matrices(a_re, a_im, log_step, b_re, b_im, c_re, c_im, d_skip), bsz)
    z = z_t.reshape(S5_GROUPS, nch, bsz, S5_CHUNK, S5_GROUP).transpose(2, 1, 3, 0, 4).reshape(t, S5_WIDTH)

    waf, wab = _pad_gate_w(gla_w_a)
    yb = _gla(g.reshape(bsz, seq, G_COLS), waf, wab, gla_b_a.astype(F32), gla_consts,
              gla_ln_g.astype(F32)[None, :])
    yc = _swa(a.reshape(bsz, seq, A_COLS), swa_sink.astype(F32), rope_tabs)

    wo = w_out.astype(BF16)
    row = lambda v: v.astype(F32)[None, :]
    return _out_ffn(x2d, z, yb.reshape(t, GLA_WIDTH), yc.reshape(t, SWA_WIDTH),
                    w_glu.astype(BF16), row(b_glu),
                    wo[0:S5_WIDTH], wo[S5_WIDTH:S5_WIDTH + GLA_WIDTH], wo[S5_WIDTH + GLA_WIDTH:],
                    row(ln1_g), row(ln1_b), w_ff1.astype(BF16), w_ff2.astype(BF16),
                    row(ln2_g), row(ln2_b), alpha)


def kernel(x, w_in, s5_a_re, s5_a_im, s5_log_step, s5_b_re, s5_b_im, s5_c_re, s5_c_im, s5_d, s5_w_glu, s5_b_glu, gla_w_a, gla_b_a, gla_ln_g, swa_sink, w_out, ln1_g, ln1_b, w_ff1, w_ff2, ln2_g, ln2_b):
    bsz, seq, _ = x.shape
    depth = w_in.shape[0]
    alpha = (2 * depth) ** 0.25
    gla_consts = _gla_constants()
    rope_tabs = _rope_tables(seq)
    h = x.reshape(bsz * seq, D_MODEL)
    for l in range(depth):
        h = _layer(h, bsz, seq, alpha, w_in[l], s5_a_re[l], s5_a_im[l], s5_log_step[l],
                   s5_b_re[l], s5_b_im[l], s5_c_re[l], s5_c_im[l], s5_d[l], s5_w_glu[l], s5_b_glu[l],
                   gla_w_a[l], gla_b_a[l], gla_ln_g[l], swa_sink[l], w_out[l], ln1_g[l], ln1_b[l],
                   w_ff1[l], w_ff2[l], ln2_g[l], ln2_b[l], gla_consts, rope_tabs)
    return h.reshape(bsz, seq, D_MODEL)
```

```python
import functools
import math

import jax
import jax.numpy as jnp
import numpy as np
from jax import lax
from jax.experimental import pallas as pl
from jax.experimental.pallas import tpu as pltpu

F32 = jnp.float32
BF16 = jnp.bfloat16

D_MODEL = 1024
S5_WIDTH = 256
S5_GROUP = 16
S5_GROUPS = 16
S5_STATE = 64
S5_CHUNK = 16
GLA_HEADS = 4
GLA_DK = 32
GLA_DV = 64
GLA_QK = 128
GLA_WIDTH = 256
GLA_RANK = 16
GLA_TAU = 16.0
GLA_CHUNK = 64
SWA_WIDTH = 512
SWA_HEAD_DIM = 64
SWA_HEADS = 8
SWA_KV_HEADS = 2
SWA_GROUP = SWA_HEADS // SWA_KV_HEADS
SWA_KV = 128
SWA_WINDOW = 128
SWA_BLOCK = 128
ROT_DIM = 16
ROPE_THETA = 500000.0
D_FF = 4096
LN_EPS = 1e-5
NEG_BIG = -1e30

G_COLS = GLA_QK + GLA_QK + GLA_WIDTH + GLA_WIDTH + 128
A_COLS = SWA_WIDTH + SWA_KV + SWA_KV
IN_COLS = G_COLS + A_COLS

VMEM_LIMIT = 56 * 1024 * 1024
ROW_TILE = 512
FF_CHUNK = 512

_NT = (((1,), (1,)), ((), ()))
_TN = (((0,), (0,)), ((), ()))


def _mm(a, b):
    return jnp.dot(a, b, preferred_element_type=F32)


def _params(*sem):
    return pltpu.CompilerParams(dimension_semantics=sem, vmem_limit_bytes=VMEM_LIMIT)


def _resident(shape):
    nd = len(shape)
    return pl.BlockSpec(shape, lambda *_: (0,) * nd)


def _inproj_kernel(x_ref, w_ref, g_ref, a_ref):
    x = x_ref[...].astype(BF16)
    g_ref[...] = _mm(x, w_ref[:, 0:G_COLS])
    a_ref[...] = _mm(x, w_ref[:, G_COLS:IN_COLS])


def _inproj(x2d, w_packed):
    t = x2d.shape[0]
    tm = min(ROW_TILE, t)
    row = lambda n: pl.BlockSpec((tm, n), lambda i: (i, 0))
    return pl.pallas_call(
        _inproj_kernel,
        grid=(t // tm,),
        in_specs=[row(D_MODEL), _resident((D_MODEL, IN_COLS))],
        out_specs=[row(G_COLS), row(A_COLS)],
        out_shape=[jax.ShapeDtypeStruct((t, G_COLS), F32),
                   jax.ShapeDtypeStruct((t, A_COLS), F32)],
        compiler_params=_params("parallel"),
        name="inproj",
    )(x2d, w_packed)


def _s5_matrices(a_re, a_im, log_step, b_re, b_im, c_re, c_im, d_skip, nch):
    hp = lax.Precision.HIGHEST
    c = S5_CHUNK
    ar, ai = a_re.astype(F32), a_im.astype(F32)
    step = jnp.exp(log_step.astype(F32))
    lr, li = ar * step, ai * step
    kk = jnp.arange(c + 1, dtype=F32)[:, None, None, None]
    mag = jnp.exp(kk * lr[None])
    pr, pi = mag * jnp.cos(kk * li[None]), mag * jnp.sin(kk * li[None])
    nr, ni = pr[1] - 1.0, pi[1]
    den = ar * ar + ai * ai
    qr, qi = (nr * ar + ni * ai) / den, (ni * ar - nr * ai) / den
    br, bi = b_re.astype(F32), b_im.astype(F32)
    bbr = qr[..., None] * br - qi[..., None] * bi
    bbi = qr[..., None] * bi + qi[..., None] * br
    cr, ci = c_re.astype(F32), c_im.astype(F32)

    cpr = cr[None] * pr[:c, :, :, None, :] - ci[None] * pi[:c, :, :, None, :]
    cpi = cr[None] * pi[:c, :, :, None, :] + ci[None] * pr[:c, :, :, None, :]
    kern = (jnp.einsum('kzgop,zgpi->zkgoi', cpr, bbr, precision=hp)
            - jnp.einsum('kzgop,zgpi->zkgoi', cpi, bbi, precision=hp))
    kf, kb = kern[0], kern[1]
    lag = jnp.concatenate([kb[1:][::-1], (kf[0] + kb[0])[None], kf[1:]], axis=0)
    jj = np.arange(c)[:, None]
    ii = np.arange(c)[None, :]
    t4 = lag[(ii - jj) + c - 1]
    tt_mat = jnp.transpose(t4, (2, 1, 3, 0, 4)).reshape(S5_GROUPS, c * S5_GROUP, c * S5_GROUP)

    def cmul_b(wr, wi, z):
        return (wr[..., None] * bbr[z][None] - wi[..., None] * bbi[z][None],
                wr[..., None] * bbi[z][None] + wi[..., None] * bbr[z][None])
    sfr, sfi = cmul_b(pr[:c][::-1, 0], pi[:c][::-1, 0], 0)
    sbr, sbi = cmul_b(pr[:c][:, 1], pi[:c][:, 1], 1)
    s4 = jnp.stack([sfr, sbr, sfi, sbi], axis=0)
    s_mat = jnp.transpose(s4, (2, 1, 4, 0, 3)).reshape(S5_GROUPS, c * S5_GROUP, 4 * S5_STATE)

    def cmul_c(wr, wi, z):
        return (cr[z][None] * wr[:, :, None, :] - ci[z][None] * wi[:, :, None, :],
                cr[z][None] * wi[:, :, None, :] + ci[z][None] * wr[:, :, None, :])
    ofr, ofi = cmul_c(pr[1:c + 1, 0], pi[1:c + 1, 0], 0)
    obr, obi = cmul_c(pr[1:c + 1][::-1, 1], pi[1:c + 1][::-1, 1], 1)
    o4 = jnp.stack([ofr, obr, -ofi, -obi], axis=0)
    ot_mat = jnp.transpose(o4, (2, 1, 3, 0, 4)).reshape(S5_GROUPS, c * S5_GROUP, 4 * S5_STATE)

    a_r = jnp.concatenate([pr[c, 0], pr[c, 1]], axis=-1)
    a_i = jnp.concatenate([pi[c, 0], pi[c, 1]], axis=-1)
    d_col = jnp.broadcast_to(jnp.tile(d_skip.astype(F32), (1, c))[:, :, None], (S5_GROUPS, c * S5_GROUP, nch))
    return tt_mat.astype(BF16), s_mat.astype(BF16), ot_mat.astype(BF16), a_r, a_i, d_col


def _gelu_tanh(x):
    return 0.5 * x * (1.0 + jnp.tanh(math.sqrt(2.0 / math.pi) * (x + 0.044715 * (x * x * x))))


def _sigmoid(x):
    return 1.0 / (1.0 + jnp.exp(-x))


def _s5_kernel(x_ref, wu_ref, tt_ref, s_ref, ot_ref, ar_ref, ai_ref, dcol_ref, wglu_ref, bglu_ref, ya_ref,
               d_scr, cr_scr, ci_scr, efr_scr, efi_scr, ebr_scr, ebi_scr, zt_scr):
    c, ng = S5_CHUNK, S5_GROUPS
    xs = D_MODEL // 128
    ys = S5_WIDTH // 128
    nch = x_ref.shape[1] // (xs * c)
    wu = wu_ref[...]
    for j in range(c):
        xj = jnp.concatenate([x_ref[0, pl.ds(j * xs + s, nch, stride=c * xs), :] for s in range(xs)],
                             axis=1).astype(BF16)
        ujt = lax.dot_general(wu, xj, _NT, preferred_element_type=F32)
        d_scr[:, j * S5_GROUP:(j + 1) * S5_GROUP, :] = ujt.reshape(ng, S5_GROUP, nch)
    for g in range(ng):
        dg = d_scr[g].astype(BF16)
        contrib = lax.dot_general(dg, s_ref[g], _TN, preferred_element_type=F32)
        cr_scr[g * nch:(g + 1) * nch, :] = contrib[:, :128]
        ci_scr[g * nch:(g + 1) * nch, :] = contrib[:, 128:]

    ar, ai = ar_ref[...], ai_ref[...]
    fwd = lax.broadcasted_iota(jnp.int32, (ng, 128), 1) < S5_STATE

    def step(k, carry):
        sre, sim = carry
        up = pl.ds(k, ng, stride=nch)
        dn = pl.ds(nch - 1 - k, ng, stride=nch)
        efr_scr[up, :] = sre
        efi_scr[up, :] = sim
        ebr_scr[dn, :] = sre
        ebi_scr[dn, :] = sim
        cre = jnp.where(fwd, cr_scr[up, :], cr_scr[dn, :])
        cim = jnp.where(fwd, ci_scr[up, :], ci_scr[dn, :])
        return ar * sre - ai * sim + cre, ar * sim + ai * sre + cim

    zero = jnp.zeros((ng, 128), F32)
    lax.fori_loop(0, nch, step, (zero, zero))

    fwd_all = lax.broadcasted_iota(jnp.int32, (nch, 128), 1) < S5_STATE
    for g in range(ng):
        dg32 = d_scr[g]
        rows = slice(g * nch, (g + 1) * nch)
        enter = jnp.concatenate([jnp.where(fwd_all, efr_scr[rows, :], ebr_scr[rows, :]),
                                 jnp.where(fwd_all, efi_scr[rows, :], ebi_scr[rows, :])],
                                axis=1).astype(BF16)
        yt = (_mm(tt_ref[g], dg32.astype(BF16))
              + lax.dot_general(ot_ref[g], enter, _NT, preferred_element_type=F32)
              + dcol_ref[g] * dg32)
        zt_scr[:, g * S5_GROUP:(g + 1) * S5_GROUP, :] = _gelu_tanh(yt).reshape(c, S5_GROUP, nch)
    for i in range(c):
        zi = zt_scr[i].astype(BF16)
        glu = lax.dot_general(zi, wglu_ref[...], _TN, preferred_element_type=F32) + bglu_ref[...]
        ya = glu[:, :S5_WIDTH] * _sigmoid(glu[:, S5_WIDTH:])
        for s in range(ys):
            ya_ref[0, pl.ds(i * ys + s, nch, stride=c * ys), :] = ya[:, s * 128:(s + 1) * 128]


def _s5(x3, wu_t, mats, wglu, bglu):
    b, seq, _ = x3.shape
    nch = seq // S5_CHUNK
    tt_mat, s_mat, ot_mat, a_r, a_i, d_col = mats
    w = S5_CHUNK * S5_GROUP
    xrows = seq * (D_MODEL // 128)
    yrows = seq * (S5_WIDTH // 128)
    state_scr = pltpu.VMEM((S5_GROUPS * nch, 2 * S5_STATE), F32)
    out = pl.pallas_call(
        _s5_kernel,
        grid=(b,),
        in_specs=[pl.BlockSpec((1, xrows, 128), lambda i: (i, 0, 0)),
                  _resident((S5_WIDTH, D_MODEL)),
                  _resident((S5_GROUPS, w, w)), _resident((S5_GROUPS, w, 4 * S5_STATE)),
                  _resident((S5_GROUPS, w, 4 * S5_STATE)),
                  _resident((S5_GROUPS, 128)), _resident((S5_GROUPS, 128)),
                  _resident((S5_GROUPS, w, nch)),
                  _resident((S5_WIDTH, 2 * S5_WIDTH)), _resident((1, 2 * S5_WIDTH))],
        out_specs=pl.BlockSpec((1, yrows, 128), lambda i: (i, 0, 0)),
        out_shape=jax.ShapeDtypeStruct((b, yrows, 128), F32),
        scratch_shapes=[pltpu.VMEM((S5_GROUPS, w, nch), F32)] + [state_scr] * 6
                       + [pltpu.VMEM((S5_CHUNK, w, nch), F32)],
        compiler_params=_params("parallel"),
        name="s5",
    )(x3.reshape(b, xrows, 128), wu_t, tt_mat, s_mat, ot_mat, a_r, a_i, d_col, wglu, bglu)
    return out.reshape(b, seq, S5_WIDTH)


def _gla_constants():
    r = np.arange(256)
    same = (r[:, None] // GLA_CHUNK) == (r[None, :] // GLA_CHUNK)
    prefix = same & (r[None, :] <= r[:, None])
    suffix = same & (r[None, :] >= r[:, None])
    avg = ((r[:, None] // GLA_DV) == (r[None, :] // GLA_DV)) / float(GLA_DV)
    as_bf = lambda m: jnp.asarray(m.astype(np.float32), BF16)
    return as_bf(prefix), as_bf(suffix), as_bf(same), as_bf(avg)


def _log_sigmoid(x):
    return jnp.minimum(x, 0.0) - jnp.log1p(jnp.exp(-jnp.abs(x)))


def _split_mm(a, b):
    hi = a.astype(BF16)
    lo = (a - hi.astype(F32)).astype(BF16)
    return _mm(hi, b) + _mm(lo, b)


def _chunk_sums(mat, la):
    hi = la.astype(BF16)
    lo = (la - hi.astype(F32)).astype(BF16)
    out = []
    for i in range(la.shape[0] // 256):
        sl = slice(i * 256, (i + 1) * 256)
        out.append(_mm(mat, hi[sl]) + _mm(mat, lo[sl]))
    return jnp.concatenate(out, axis=0)


def _gla_kernel(g_ref, waf_ref, wab_ref, ba_ref, pre_ref, suf_ref, tot_ref, avg_ref, lng_ref, o_ref,
                qf_s, kfi_s, kfs_s, qb_s, kbi_s, kbs_s, v_s, totf_s, totb_s, of_s, ob_s):
    seq = g_ref.shape[1]
    nchunk = seq // GLA_CHUNK
    c = GLA_CHUNK
    q = g_ref[0, :, 0:128] * (GLA_DK ** -0.5)
    k = g_ref[0, :, 128:256]
    lfb = g_ref[0, :, 768:896].astype(BF16)
    la_f = _log_sigmoid(_mm(lfb, waf_ref[...]) + ba_ref[0:1, :]) * (1.0 / GLA_TAU)
    la_b = _log_sigmoid(_mm(lfb, wab_ref[...]) + ba_ref[1:2, :]) * (1.0 / GLA_TAU)

    bf = _chunk_sums(pre_ref[...], la_f)
    tf = _chunk_sums(tot_ref[...], la_f)
    qf_s[...] = (q * jnp.exp(bf)).astype(BF16)
    kfi_s[...] = (k * jnp.exp(-bf)).astype(BF16)
    kfs_s[...] = (k * jnp.exp(tf - bf)).astype(BF16)
    totf_s[...] = tf
    bs = _chunk_sums(suf_ref[...], la_b)
    tb = _chunk_sums(tot_ref[...], la_b)
    qb_s[...] = (q * jnp.exp(bs)).astype(BF16)
    kbi_s[...] = (k * jnp.exp(-bs)).astype(BF16)
    kbs_s[...] = (k * jnp.exp(tb - bs)).astype(BF16)
    totb_s[...] = tb
    v_s[...] = g_ref[0, :, 256:512].astype(BF16)

    r256 = lax.broadcasted_iota(jnp.int32, (4 * c, 128), 0)
    l128 = lax.broadcasted_iota(jnp.int32, (4 * c, 128), 1)
    head_k = (r256 // c) == (l128 // GLA_DK)
    r256v = lax.broadcasted_iota(jnp.int32, (4 * c, 256), 0)
    l256v = lax.broadcasted_iota(jnp.int32, (4 * c, 256), 1)
    head_v = (r256v // c) == (l256v // GLA_DV)
    qi = lax.broadcasted_iota(jnp.int32, (c, 4 * c), 0)
    kj = lax.broadcasted_iota(jnp.int32, (c, 4 * c), 1) % c
    causal_f = kj <= qi
    causal_b = kj > qi

    def one_dir(r0, q_s, ki_s, ks_s, tot_s, out_s, causal, state):
        qq = q_s[pl.ds(r0, c), :]
        ki = ki_s[pl.ds(r0, c), :]
        ks = ks_s[pl.ds(r0, c), :]
        vv = v_s[pl.ds(r0, c), :]
        kexp = jnp.where(head_k, jnp.concatenate([ki] * 4, axis=0), jnp.zeros((), BF16))
        vexp = jnp.where(head_v, jnp.concatenate([vv] * 4, axis=0), jnp.zeros((), BF16))
        sc = lax.dot_general(qq, kexp, _NT, preferred_element_type=F32)
        pm = jnp.where(causal, sc, 0.0).astype(BF16)
        o = _mm(pm, vexp) + lax.dot_general(qq, state.astype(BF16), _NT, preferred_element_type=F32)
        out_s[pl.ds(r0, c), :] = o
        kv_t = lax.dot_general(vv, ks, _TN, preferred_element_type=F32)
        dec = jnp.exp(tot_s[pl.ds(r0, 1), :])
        return dec * state + jnp.where(head_k, kv_t, 0.0)

    def body(n, carry):
        st_f, st_b = carry
        r0 = pl.multiple_of(n * c, c)
        r1 = pl.multiple_of((nchunk - 1 - n) * c, c)
        st_f = one_dir(r0, qf_s, kfi_s, kfs_s, totf_s, of_s, causal_f, st_f)
        st_b = one_dir(r1, qb_s, kbi_s, kbs_s, totb_s, ob_s, causal_b, st_b)
        return st_f, st_b

    zero = jnp.zeros((GLA_WIDTH, GLA_QK), F32)
    lax.fori_loop(0, nchunk, body, (zero, zero))

    o = of_s[...] + ob_s[...]
    avg = avg_ref[...]
    mu = _split_mm(o, avg)
    dlt = o - mu
    var = _split_mm(dlt * dlt, avg)
    on = dlt * lax.rsqrt(var + LN_EPS)
    r = g_ref[0, :, 512:768]
    o_ref[0] = on * lng_ref[...] * (r * _sigmoid(r))


def _gla(g3, waf, wab, ba, consts, ln_g):
    b, seq, _ = g3.shape
    pre, suf, tot, avg = consts
    bf = lambda w: pltpu.VMEM((seq, w), BF16)
    ff = lambda w: pltpu.VMEM((seq, w), F32)
    return pl.pallas_call(
        _gla_kernel,
        grid=(b,),
        in_specs=[pl.BlockSpec((1, seq, G_COLS), lambda i: (i, 0, 0)),
                  _resident((128, 128)), _resident((128, 128)), _resident((2, 128)),
                  _resident((256, 256)), _resident((256, 256)), _resident((256, 256)),
                  _resident((256, 256)), _resident((1, 256))],
        out_specs=pl.BlockSpec((1, seq, GLA_WIDTH), lambda i: (i, 0, 0)),
        out_shape=jax.ShapeDtypeStruct((b, seq, GLA_WIDTH), F32),
        scratch_shapes=[bf(128)] * 6 + [bf(256), ff(128), ff(128), ff(256), ff(256)],
        compiler_params=_params("parallel"),
        name="gla",
    )(g3, waf, wab, ba, pre, suf, tot, avg, ln_g)


def _rope_tables(seq):
    pos = jnp.arange(seq, dtype=F32)
    inv_freq = ROPE_THETA ** (-jnp.arange(0, ROT_DIM, 2, dtype=F32) / ROT_DIM)
    ang = pos[:, None] * inv_freq[None, :]
    cos, sin = jnp.cos(ang), jnp.sin(ang)
    half = ROT_DIM // 2
    pad = jnp.zeros((seq, SWA_HEAD_DIM - ROT_DIM), F32)
    zeros = jnp.zeros((seq, half), F32)
    c_head = jnp.concatenate([cos, cos, pad + 1.0], axis=1)
    up_head = jnp.concatenate([-sin, zeros, pad], axis=1)
    dn_head = jnp.concatenate([zeros, sin, pad], axis=1)
    two = lambda t: jnp.concatenate([t, t], axis=1)
    return two(c_head), two(up_head), two(dn_head)


def _swa_kernel(sink_ref, a_ref, cos_ref, up_ref, dn_ref, o_ref, q_s, k_s, v_s):
    seq = a_ref.shape[1]
    blk = SWA_BLOCK
    hd = SWA_HEAD_DIM
    cosv, upv, dnv = cos_ref[...], up_ref[...], dn_ref[...]

    def rope(x):
        return x * cosv + pltpu.roll(x, 128 - ROT_DIM // 2, 1) * upv + pltpu.roll(x, ROT_DIM // 2, 1) * dnv

    for slab in range(SWA_WIDTH // 128):
        qr = (rope(a_ref[0, :, slab * 128:(slab + 1) * 128]) * (hd ** -0.5)).astype(BF16)
        q_s[2 * slab] = qr[:, :hd]
        q_s[2 * slab + 1] = qr[:, hd:]
    kr = rope(a_ref[0, :, SWA_WIDTH:SWA_WIDTH + SWA_KV]).astype(BF16)
    vr = a_ref[0, :, SWA_WIDTH + SWA_KV:A_COLS].astype(BF16)
    zpad = jnp.zeros((blk, hd), BF16)
    for kh in range(SWA_KV_HEADS):
        k_s[kh, 0:blk, :] = zpad
        k_s[kh, blk + seq:2 * blk + seq, :] = zpad
        v_s[kh, 0:blk, :] = zpad
        v_s[kh, blk + seq:2 * blk + seq, :] = zpad
        k_s[kh, blk:blk + seq, :] = kr[:, kh * hd:(kh + 1) * hd]
        v_s[kh, blk:blk + seq, :] = vr[:, kh * hd:(kh + 1) * hd]

    qi = lax.broadcasted_iota(jnp.int32, (blk, 3 * blk), 0)
    kk = lax.broadcasted_iota(jnp.int32, (blk, 3 * blk), 1)
    band = jnp.abs(qi + blk - kk) <= SWA_WINDOW

    def body(n, carry):
        r0 = pl.multiple_of(n * blk, blk)
        kpos = kk + (n - 1) * blk
        valid = band & (kpos >= 0) & (kpos < seq)
        for kh in range(SWA_KV_HEADS):
            k3 = k_s[kh, pl.ds(r0, 3 * blk), :]
            v3 = v_s[kh, pl.ds(r0, 3 * blk), :]
            outs = []
            for g in range(SWA_GROUP):
                h = kh * SWA_GROUP + g
                sink = sink_ref[h]
                qq = q_s[h, pl.ds(r0, blk), :]
                s = lax.dot_general(qq, k3, _NT, preferred_element_type=F32)
                s = jnp.where(valid, s, NEG_BIG)
                m = jnp.maximum(jnp.max(s, axis=1, keepdims=True), sink)
                p = jnp.exp(s - m)
                den = jnp.sum(p, axis=1, keepdims=True) + jnp.exp(sink - m)
                outs.append(_mm(p.astype(BF16), v3) / den)
            o_ref[0, pl.ds(r0, blk), kh * SWA_GROUP * hd:(kh + 1) * SWA_GROUP * hd] = jnp.concatenate(outs, axis=1)
        return carry

    lax.fori_loop(0, seq // blk, body, 0)


def _swa(a3, sink, tables):
    b, seq, _ = a3.shape
    cos_t, up_t, dn_t = tables
    tab = pl.BlockSpec((seq, 128), lambda i, s: (0, 0))
    grid_spec = pltpu.PrefetchScalarGridSpec(
        num_scalar_prefetch=1,
        grid=(b,),
        in_specs=[pl.BlockSpec((1, seq, A_COLS), lambda i, s: (i, 0, 0)), tab, tab, tab],
        out_specs=pl.BlockSpec((1, seq, SWA_WIDTH), lambda i, s: (i, 0, 0)),
        scratch_shapes=[pltpu.VMEM((SWA_HEADS, seq, SWA_HEAD_DIM), BF16),
                        pltpu.VMEM((SWA_KV_HEADS, seq + 2 * SWA_BLOCK, SWA_HEAD_DIM), BF16),
                        pltpu.VMEM((SWA_KV_HEADS, seq + 2 * SWA_BLOCK, SWA_HEAD_DIM), BF16)],
    )
    return pl.pallas_call(
        _swa_kernel,
        grid_spec=grid_spec,
        out_shape=jax.ShapeDtypeStruct((b, seq, SWA_WIDTH), F32),
        compiler_params=_params("parallel"),
        name="swa",
    )(sink, a3, cos_t, up_t, dn_t)


def _layer_norm(x, g, b):
    mu = jnp.mean(x, axis=-1, keepdims=True)
    d = x - mu
    var = jnp.mean(d * d, axis=-1, keepdims=True)
    return d * lax.rsqrt(var + LN_EPS) * g + b


def _out_ffn_kernel(x_ref, ya_ref, yb_ref, yc_ref, woa_ref, wob_ref, woc_ref,
                    g1_ref, b1_ref, w1_ref, w2_ref, g2_ref, b2_ref, o_ref, acc_ref, *, alpha):
    mix = (_mm(ya_ref[...].astype(BF16), woa_ref[...]) + _mm(yb_ref[...].astype(BF16), wob_ref[...])
           + _mm(yc_ref[...].astype(BF16), woc_ref[...]))
    x1 = _layer_norm(alpha * x_ref[...] + mix, g1_ref[...], b1_ref[...])
    x1b = x1.astype(BF16)
    for ci in range(D_FF // FF_CHUNK):
        sl = slice(ci * FF_CHUNK, (ci + 1) * FF_CHUNK)
        hid = jnp.maximum(_mm(x1b, w1_ref[:, sl]), 0.0)
        part = _mm((hid * hid).astype(BF16), w2_ref[sl, :])
        if ci == 0:
            acc_ref[...] = part
        else:
            acc_ref[...] += part
    o_ref[...] = _layer_norm(alpha * x1 + acc_ref[...], g2_ref[...], b2_ref[...])


def _out_ffn(x2d, ya2d, yb2d, yc2d, woa, wob, woc, g1, b1, w1, w2, g2, b2, alpha):
    t = x2d.shape[0]
    tm = min(ROW_TILE, t)
    row = lambda n: pl.BlockSpec((tm, n), lambda i: (i, 0))
    once = lambda shape: pl.BlockSpec(shape, lambda i: (0, 0), pipeline_mode=pl.Buffered(1))
    return pl.pallas_call(
        functools.partial(_out_ffn_kernel, alpha=alpha),
        grid=(t // tm,),
        in_specs=[row(D_MODEL), row(S5_WIDTH), row(GLA_WIDTH), row(SWA_WIDTH),
                  once((S5_WIDTH, D_MODEL)), once((GLA_WIDTH, D_MODEL)), once((SWA_WIDTH, D_MODEL)),
                  once((1, D_MODEL)), once((1, D_MODEL)),
                  once((D_MODEL, D_FF)), once((D_FF, D_MODEL)),
                  once((1, D_MODEL)), once((1, D_MODEL))],
        out_specs=row(D_MODEL),
        out_shape=jax.ShapeDtypeStruct((t, D_MODEL), F32),
        scratch_shapes=[pltpu.VMEM((tm, D_MODEL), F32)],
        compiler_params=_params("parallel"),
        name="out_ffn",
    )(x2d, ya2d, yb2d, yc2d, woa, wob, woc, g1, b1, w1, w2, g2, b2)


def _pack_w_in(w_in):
    points = np.cumsum([S5_WIDTH, GLA_QK, GLA_QK, GLA_WIDTH, GLA_WIDTH, GLA_RANK, GLA_RANK,
                        SWA_WIDTH, SWA_KV]).tolist()
    s5, gq, gk, gv, gr, glf, glb, aq, ak, av = jnp.split(w_in, points, axis=-1)
    pad = jnp.zeros((w_in.shape[0], 128 - 2 * GLA_RANK), w_in.dtype)
    packed = jnp.concatenate([gq, gk, gv, gr, glf, glb, pad, aq, ak, av], axis=-1).astype(BF16)
    return s5.T.astype(BF16), packed


def _pad_gate_w(w_a):
    z = jnp.zeros((128, GLA_QK), F32)
    waf = z.at[0:GLA_RANK].set(w_a[0].astype(F32))
    wab = z.at[GLA_RANK:2 * GLA_RANK].set(w_a[1].astype(F32))
    return waf.astype(BF16), wab.astype(BF16)


def _layer(x2d, bsz, seq, alpha, w_in, a_re, a_im, log_step, b_re, b_im, c_re, c_im, d_skip,
           w_glu, b_glu, gla_w_a, gla_b_a, gla_ln_g, swa_sink, w_out, ln1_g, ln1_b, w_ff1, w_ff2,
           ln2_g, ln2_b, gla_consts, rope_tabs):
    t = bsz * seq
    row = lambda v: v.astype(F32)[None, :]
    wu_t, w_packed = _pack_w_in(w_in)
    g, a = _inproj(x2d, w_packed)

    mats = _s5_matrices(a_re, a_im, log_step, b_re, b_im, c_re, c_im, d_skip, seq // S5_CHUNK)
    ya = _s5(x2d.reshape(bsz, seq, D_MODEL), wu_t, mats, w_glu.astype(BF16), row(b_glu))

    waf, wab = _pad_gate_w(gla_w_a)
    yb = _gla(g.reshape(bsz, seq, G_COLS), waf, wab, gla_b_a.astype(F32), gla_consts, row(gla_ln_g))
    yc = _swa(a.reshape(bsz, seq, A_COLS), swa_sink.astype(F32), rope_tabs)

    wo = w_out.astype(BF16)
    return _out_ffn(x2d, ya.reshape(t, S5_WIDTH), yb.reshape(t, GLA_WIDTH), yc.reshape(t, SWA_WIDTH),
                    wo[0:S5_WIDTH], wo[S5_WIDTH:S5_WIDTH + GLA_WIDTH], wo[S5_WIDTH + GLA_WIDTH:],
                    row(ln1_g), row(ln1_b), w_ff1.astype(BF16), w_ff2.astype(BF16),
                    row(ln2_g), row(ln2_b), alpha)


def kernel(x, w_in, s5_a_re, s5_a_im, s5_log_step, s5_b_re, s5_b_im, s5_c_re, s5_c_im, s5_d, s5_w_glu, s5_b_glu, gla_w_a, gla_b_a, gla_ln_g, swa_sink, w_out, ln1_g, ln1_b, w_ff1, w_ff2, ln2_g, ln2_b):
    bsz, seq, _ = x.shape
    depth = w_in.shape[0]
    alpha = (2 * depth) ** 0.25
    gla_consts = _gla_constants()
    rope_tabs = _rope_tables(seq)
    h = x.reshape(bsz * seq, D_MODEL)
    for l in range(depth):
        h = _layer(h, bsz, seq, alpha, w_in[l], s5_a_re[l], s5_a_im[l], s5_log_step[l],
                   s5_b_re[l], s5_b_im[l], s5_c_re[l], s5_c_im[l], s5_d[l], s5_w_glu[l], s5_b_glu[l],
                   gla_w_a[l], gla_b_a[l], gla_ln_g[l], swa_sink[l], w_out[l], ln1_g[l], ln1_b[l],
                   w_ff1[l], w_ff2[l], ln2_g[l], ln2_b[l], gla_consts, rope_tabs)
    return h.reshape(bsz, seq, D_MODEL)
```

```python
import functools
import math

import jax
import jax.numpy as jnp
import numpy as np
from jax import lax
from jax.experimental import pallas as pl
from jax.experimental.pallas import tpu as pltpu

F32 = jnp.float32
BF16 = jnp.bfloat16

D_MODEL = 1024
S5_WIDTH = 256
S5_GROUP = 16
S5_GROUPS = 16
S5_STATE = 64
S5_CHUNK = 16
GLA_HEADS = 4
GLA_DK = 32
GLA_DV = 64
GLA_QK = 128
GLA_WIDTH = 256
GLA_RANK = 16
GLA_TAU = 16.0
GLA_CHUNK = 64
SWA_WIDTH = 512
SWA_HEAD_DIM = 64
SWA_HEADS = 8
SWA_KV_HEADS = 2
SWA_GROUP = SWA_HEADS // SWA_KV_HEADS
SWA_KV = 128
SWA_WINDOW = 128
SWA_BLOCK = 128
ROT_DIM = 16
ROPE_THETA = 500000.0
D_FF = 4096
LN_EPS = 1e-5
NEG_BIG = -1e30

U_COLS = S5_WIDTH
G_COLS = GLA_QK + GLA_QK + GLA_WIDTH + GLA_WIDTH + 128
A_COLS = SWA_WIDTH + SWA_KV + SWA_KV
IN_COLS = U_COLS + G_COLS + A_COLS

VMEM_LIMIT = 56 * 1024 * 1024
ROW_TILE = 512
FF_CHUNK = 512

_NT = (((1,), (1,)), ((), ()))
_TN = (((0,), (0,)), ((), ()))


def _mm(a, b):
    return jnp.dot(a, b, preferred_element_type=F32)


def _params(*sem):
    return pltpu.CompilerParams(dimension_semantics=sem, vmem_limit_bytes=VMEM_LIMIT)


def _resident(shape):
    nd = len(shape)
    return pl.BlockSpec(shape, lambda *_: (0,) * nd)


def _inproj_kernel(x_ref, w_ref, ul_ref, uh_ref, g_ref, a_ref):
    x = x_ref[...].astype(BF16)
    u = _mm(x, w_ref[:, 0:U_COLS])
    ul_ref[...] = u[:, :128]
    uh_ref[...] = u[:, 128:]
    g_ref[...] = _mm(x, w_ref[:, U_COLS:U_COLS + G_COLS])
    a_ref[...] = _mm(x, w_ref[:, U_COLS + G_COLS:IN_COLS])


def _inproj(x2d, w_packed):
    t = x2d.shape[0]
    tm = min(ROW_TILE, t)
    row = lambda n: pl.BlockSpec((tm, n), lambda i: (i, 0))
    return pl.pallas_call(
        _inproj_kernel,
        grid=(t // tm,),
        in_specs=[row(D_MODEL), _resident((D_MODEL, IN_COLS))],
        out_specs=[row(128), row(128), row(G_COLS), row(A_COLS)],
        out_shape=[jax.ShapeDtypeStruct((t, 128), F32), jax.ShapeDtypeStruct((t, 128), F32),
                   jax.ShapeDtypeStruct((t, G_COLS), F32), jax.ShapeDtypeStruct((t, A_COLS), F32)],
        compiler_params=_params("parallel"),
        name="inproj",
    )(x2d, w_packed)


def _s5_matrices(a_re, a_im, log_step, b_re, b_im, c_re, c_im, d_skip, nch):
    hp = lax.Precision.HIGHEST
    c = S5_CHUNK
    ar, ai = a_re.astype(F32), a_im.astype(F32)
    step = jnp.exp(log_step.astype(F32))
    lr, li = ar * step, ai * step
    kk = jnp.arange(c + 1, dtype=F32)[:, None, None, None]
    mag = jnp.exp(kk * lr[None])
    pr, pi = mag * jnp.cos(kk * li[None]), mag * jnp.sin(kk * li[None])
    nr, ni = pr[1] - 1.0, pi[1]
    den = ar * ar + ai * ai
    qr, qi = (nr * ar + ni * ai) / den, (ni * ar - nr * ai) / den
    br, bi = b_re.astype(F32), b_im.astype(F32)
    bbr = qr[..., None] * br - qi[..., None] * bi
    bbi = qr[..., None] * bi + qi[..., None] * br
    cr, ci = c_re.astype(F32), c_im.astype(F32)

    cpr = cr[None] * pr[:c, :, :, None, :] - ci[None] * pi[:c, :, :, None, :]
    cpi = cr[None] * pi[:c, :, :, None, :] + ci[None] * pr[:c, :, :, None, :]
    kern = (jnp.einsum('kzgop,zgpi->zkgoi', cpr, bbr, precision=hp)
            - jnp.einsum('kzgop,zgpi->zkgoi', cpi, bbi, precision=hp))
    kf, kb = kern[0], kern[1]
    lag = jnp.concatenate([kb[1:][::-1], (kf[0] + kb[0])[None], kf[1:]], axis=0)
    jj = np.arange(c)[:, None]
    ii = np.arange(c)[None, :]
    t4 = lag[(ii - jj) + c - 1]
    tt_mat = jnp.transpose(t4, (2, 1, 3, 0, 4)).reshape(S5_GROUPS, c * S5_GROUP, c * S5_GROUP)

    def cmul_b(wr, wi, z):
        return (wr[..., None] * bbr[z][None] - wi[..., None] * bbi[z][None],
                wr[..., None] * bbi[z][None] + wi[..., None] * bbr[z][None])
    sfr, sfi = cmul_b(pr[:c][::-1, 0], pi[:c][::-1, 0], 0)
    sbr, sbi = cmul_b(pr[:c][:, 1], pi[:c][:, 1], 1)
    s4 = jnp.stack([sfr, sbr, sfi, sbi], axis=0)
    s_mat = jnp.transpose(s4, (2, 1, 4, 0, 3)).reshape(S5_GROUPS, c * S5_GROUP, 4 * S5_STATE)

    def cmul_c(wr, wi, z):
        return (cr[z][None] * wr[:, :, None, :] - ci[z][None] * wi[:, :, None, :],
                cr[z][None] * wi[:, :, None, :] + ci[z][None] * wr[:, :, None, :])
    ofr, ofi = cmul_c(pr[1:c + 1, 0], pi[1:c + 1, 0], 0)
    obr, obi = cmul_c(pr[1:c + 1][::-1, 1], pi[1:c + 1][::-1, 1], 1)
    o4 = jnp.stack([ofr, obr, -ofi, -obi], axis=0)
    ot_mat = jnp.transpose(o4, (2, 1, 3, 0, 4)).reshape(S5_GROUPS, c * S5_GROUP, 4 * S5_STATE)

    a_r = jnp.concatenate([pr[c, 0], pr[c, 1]], axis=-1)
    a_i = jnp.concatenate([pi[c, 0], pi[c, 1]], axis=-1)
    d_col = jnp.broadcast_to(jnp.tile(d_skip.astype(F32), (1, c))[:, :, None], (S5_GROUPS, c * S5_GROUP, nch))
    return tt_mat.astype(BF16), s_mat.astype(BF16), ot_mat.astype(BF16), a_r, a_i, d_col


def _gelu_tanh(x):
    return 0.5 * x * (1.0 + jnp.tanh(math.sqrt(2.0 / math.pi) * (x + 0.044715 * (x * x * x))))


def _sigmoid(x):
    return 1.0 / (1.0 + jnp.exp(-x))


def _s5_kernel(ul_ref, uh_ref, tt_ref, s_ref, ot_ref, ar_ref, ai_ref, dcol_ref, wglu_ref, bglu_ref,
               yl_ref, yh_ref, d_scr, cr_scr, ci_scr, efr_scr, efi_scr, ebr_scr, ebi_scr, zt_scr):
    c, ng = S5_CHUNK, S5_GROUPS
    nch = ul_ref.shape[1] // c
    for j in range(c):
        tok = pl.ds(j, nch, stride=c)
        uj = jnp.concatenate([ul_ref[0, tok, :], uh_ref[0, tok, :]], axis=1)
        d_scr[:, j * S5_GROUP:(j + 1) * S5_GROUP, :] = uj.T.reshape(ng, S5_GROUP, nch)
    for g in range(ng):
        dg = d_scr[g].astype(BF16)
        contrib = lax.dot_general(dg, s_ref[g], _TN, preferred_element_type=F32)
        cr_scr[g * nch:(g + 1) * nch, :] = contrib[:, :128]
        ci_scr[g * nch:(g + 1) * nch, :] = contrib[:, 128:]

    ar, ai = ar_ref[...], ai_ref[...]
    fwd = lax.broadcasted_iota(jnp.int32, (ng, 128), 1) < S5_STATE

    def step(k, carry):
        sre, sim = carry
        up = pl.ds(k, ng, stride=nch)
        dn = pl.ds(nch - 1 - k, ng, stride=nch)
        efr_scr[up, :] = sre
        efi_scr[up, :] = sim
        ebr_scr[dn, :] = sre
        ebi_scr[dn, :] = sim
        cre = jnp.where(fwd, cr_scr[up, :], cr_scr[dn, :])
        cim = jnp.where(fwd, ci_scr[up, :], ci_scr[dn, :])
        return ar * sre - ai * sim + cre, ar * sim + ai * sre + cim

    zero = jnp.zeros((ng, 128), F32)
    lax.fori_loop(0, nch, step, (zero, zero), unroll=8)

    fwd_all = lax.broadcasted_iota(jnp.int32, (nch, 128), 1) < S5_STATE
    for g in range(ng):
        dg32 = d_scr[g]
        rows = slice(g * nch, (g + 1) * nch)
        enter = jnp.concatenate([jnp.where(fwd_all, efr_scr[rows, :], ebr_scr[rows, :]),
                                 jnp.where(fwd_all, efi_scr[rows, :], ebi_scr[rows, :])],
                                axis=1).astype(BF16)
        yt = (_mm(tt_ref[g], dg32.astype(BF16))
              + lax.dot_general(ot_ref[g], enter, _NT, preferred_element_type=F32)
              + dcol_ref[g] * dg32)
        zt_scr[:, g * S5_GROUP:(g + 1) * S5_GROUP, :] = _gelu_tanh(yt).reshape(c, S5_GROUP, nch)
    for i in range(c):
        zi = zt_scr[i].astype(BF16)
        glu = lax.dot_general(zi, wglu_ref[...], _TN, preferred_element_type=F32) + bglu_ref[...]
        ya = glu[:, :S5_WIDTH] * _sigmoid(glu[:, S5_WIDTH:])
        tok = pl.ds(i, nch, stride=c)
        yl_ref[0, tok, :] = ya[:, :128]
        yh_ref[0, tok, :] = ya[:, 128:]


def _s5(ul3, uh3, mats, wglu, bglu):
    b, seq, _ = ul3.shape
    nch = seq // S5_CHUNK
    tt_mat, s_mat, ot_mat, a_r, a_i, d_col = mats
    w = S5_CHUNK * S5_GROUP
    half = pl.BlockSpec((1, seq, 128), lambda i: (i, 0, 0))
    state_scr = pltpu.VMEM((S5_GROUPS * nch, 2 * S5_STATE), F32)
    return pl.pallas_call(
        _s5_kernel,
        grid=(b,),
        in_specs=[half, half,
                  _resident((S5_GROUPS, w, w)), _resident((S5_GROUPS, w, 4 * S5_STATE)),
                  _resident((S5_GROUPS, w, 4 * S5_STATE)),
                  _resident((S5_GROUPS, 128)), _resident((S5_GROUPS, 128)),
                  _resident((S5_GROUPS, w, nch)),
                  _resident((S5_WIDTH, 2 * S5_WIDTH)), _resident((1, 2 * S5_WIDTH))],
        out_specs=[half, half],
        out_shape=[jax.ShapeDtypeStruct((b, seq, 128), F32)] * 2,
        scratch_shapes=[pltpu.VMEM((S5_GROUPS, w, nch), F32)] + [state_scr] * 6
                       + [pltpu.VMEM((S5_CHUNK, w, nch), F32)],
        compiler_params=_params("parallel"),
        name="s5",
    )(ul3, uh3, tt_mat, s_mat, ot_mat, a_r, a_i, d_col, wglu, bglu)


def _gla_constants():
    r = np.arange(256)
    same = (r[:, None] // GLA_CHUNK) == (r[None, :] // GLA_CHUNK)
    prefix = same & (r[None, :] <= r[:, None])
    suffix = same & (r[None, :] >= r[:, None])
    avg = ((r[:, None] // GLA_DV) == (r[None, :] // GLA_DV)) / float(GLA_DV)
    as_bf = lambda m: jnp.asarray(m.astype(np.float32), BF16)
    return as_bf(prefix), as_bf(suffix), as_bf(same), as_bf(avg)


def _log_sigmoid(x):
    return jnp.minimum(x, 0.0) - jnp.log1p(jnp.exp(-jnp.abs(x)))


def _split_mm(a, b):
    hi = a.astype(BF16)
    lo = (a - hi.astype(F32)).astype(BF16)
    return _mm(hi, b) + _mm(lo, b)


def _chunk_sums(mat, la):
    hi = la.astype(BF16)
    lo = (la - hi.astype(F32)).astype(BF16)
    out = []
    for i in range(la.shape[0] // 256):
        sl = slice(i * 256, (i + 1) * 256)
        out.append(_mm(mat, hi[sl]) + _mm(mat, lo[sl]))
    return jnp.concatenate(out, axis=0)


def _gla_kernel(g_ref, waf_ref, wab_ref, ba_ref, pre_ref, suf_ref, tot_ref, avg_ref, lng_ref, o_ref,
                qf_s, kfi_s, kfs_s, qb_s, kbi_s, kbs_s, v_s, totf_s, totb_s, of_s, ob_s):
    seq = g_ref.shape[1]
    nchunk = seq // GLA_CHUNK
    c = GLA_CHUNK
    q = g_ref[0, :, 0:128] * (GLA_DK ** -0.5)
    k = g_ref[0, :, 128:256]
    lfb = g_ref[0, :, 768:896].astype(BF16)
    la_f = _log_sigmoid(_mm(lfb, waf_ref[...]) + ba_ref[0:1, :]) * (1.0 / GLA_TAU)
    la_b = _log_sigmoid(_mm(lfb, wab_ref[...]) + ba_ref[1:2, :]) * (1.0 / GLA_TAU)

    bf = _chunk_sums(pre_ref[...], la_f)
    tf = _chunk_sums(tot_ref[...], la_f)
    qf_s[...] = (q * jnp.exp(bf)).astype(BF16)
    kfi_s[...] = (k * jnp.exp(-bf)).astype(BF16)
    kfs_s[...] = (k * jnp.exp(tf - bf)).astype(BF16)
    totf_s[...] = tf
    bs = _chunk_sums(suf_ref[...], la_b)
    tb = _chunk_sums(tot_ref[...], la_b)
    qb_s[...] = (q * jnp.exp(bs)).astype(BF16)
    kbi_s[...] = (k * jnp.exp(-bs)).astype(BF16)
    kbs_s[...] = (k * jnp.exp(tb - bs)).astype(BF16)
    totb_s[...] = tb
    v_s[...] = g_ref[0, :, 256:512].astype(BF16)

    r256 = lax.broadcasted_iota(jnp.int32, (4 * c, 128), 0)
    l128 = lax.broadcasted_iota(jnp.int32, (4 * c, 128), 1)
    head_k = (r256 // c) == (l128 // GLA_DK)
    r256v = lax.broadcasted_iota(jnp.int32, (4 * c, 256), 0)
    l256v = lax.broadcasted_iota(jnp.int32, (4 * c, 256), 1)
    head_v = (r256v // c) == (l256v // GLA_DV)
    qi = lax.broadcasted_iota(jnp.int32, (c, 4 * c), 0)
    kj = lax.broadcasted_iota(jnp.int32, (c, 4 * c), 1) % c
    causal_f = kj <= qi
    causal_b = kj > qi

    def one_dir(r0, q_s, ki_s, ks_s, tot_s, out_s, causal, state):
        qq = q_s[pl.ds(r0, c), :]
        ki = ki_s[pl.ds(r0, c), :]
        ks = ks_s[pl.ds(r0, c), :]
        vv = v_s[pl.ds(r0, c), :]
        kexp = jnp.where(head_k, jnp.concatenate([ki] * 4, axis=0), jnp.zeros((), BF16))
        vexp = jnp.where(head_v, jnp.concatenate([vv] * 4, axis=0), jnp.zeros((), BF16))
        sc = lax.dot_general(qq, kexp, _NT, preferred_element_type=F32)
        pm = jnp.where(causal, sc, 0.0).astype(BF16)
        o = _mm(pm, vexp) + lax.dot_general(qq, state.astype(BF16), _NT, preferred_element_type=F32)
        out_s[pl.ds(r0, c), :] = o
        kv_t = lax.dot_general(vv, ks, _TN, preferred_element_type=F32)
        dec = jnp.exp(tot_s[pl.ds(r0, 1), :])
        return dec * state + jnp.where(head_k, kv_t, 0.0)

    def body(n, carry):
        st_f, st_b = carry
        r0 = pl.multiple_of(n * c, c)
        r1 = pl.multiple_of((nchunk - 1 - n) * c, c)
        st_f = one_dir(r0, qf_s, kfi_s, kfs_s, totf_s, of_s, causal_f, st_f)
        st_b = one_dir(r1, qb_s, kbi_s, kbs_s, totb_s, ob_s, causal_b, st_b)
        return st_f, st_b

    zero = jnp.zeros((GLA_WIDTH, GLA_QK), F32)
    lax.fori_loop(0, nchunk, body, (zero, zero))

    o = of_s[...] + ob_s[...]
    avg = avg_ref[...]
    mu = _split_mm(o, avg)
    dlt = o - mu
    var = _split_mm(dlt * dlt, avg)
    on = dlt * lax.rsqrt(var + LN_EPS)
    r = g_ref[0, :, 512:768]
    o_ref[0] = on * lng_ref[...] * (r * _sigmoid(r))


def _gla(g3, waf, wab, ba, consts, ln_g):
    b, seq, _ = g3.shape
    pre, suf, tot, avg = consts
    bf = lambda w: pltpu.VMEM((seq, w), BF16)
    ff = lambda w: pltpu.VMEM((seq, w), F32)
    return pl.pallas_call(
        _gla_kernel,
        grid=(b,),
        in_specs=[pl.BlockSpec((1, seq, G_COLS), lambda i: (i, 0, 0)),
                  _resident((128, 128)), _resident((128, 128)), _resident((2, 128)),
                  _resident((256, 256)), _resident((256, 256)), _resident((256, 256)),
                  _resident((256, 256)), _resident((1, 256))],
        out_specs=pl.BlockSpec((1, seq, GLA_WIDTH), lambda i: (i, 0, 0)),
        out_shape=jax.ShapeDtypeStruct((b, seq, GLA_WIDTH), F32),
        scratch_shapes=[bf(128)] * 6 + [bf(256), ff(128), ff(128), ff(256), ff(256)],
        compiler_params=_params("parallel"),
        name="gla",
    )(g3, waf, wab, ba, pre, suf, tot, avg, ln_g)


def _rope_tables(seq):
    pos = jnp.arange(seq, dtype=F32)
    inv_freq = ROPE_THETA ** (-jnp.arange(0, ROT_DIM, 2, dtype=F32) / ROT_DIM)
    ang = pos[:, None] * inv_freq[None, :]
    cos, sin = jnp.cos(ang), jnp.sin(ang)
    half = ROT_DIM // 2
    pad = jnp.zeros((seq, SWA_HEAD_DIM - ROT_DIM), F32)
    zeros = jnp.zeros((seq, half), F32)
    c_head = jnp.concatenate([cos, cos, pad + 1.0], axis=1)
    up_head = jnp.concatenate([-sin, zeros, pad], axis=1)
    dn_head = jnp.concatenate([zeros, sin, pad], axis=1)
    two = lambda t: jnp.concatenate([t, t], axis=1)
    return two(c_head), two(up_head), two(dn_head)


def _swa_kernel(sink_ref, a_ref, cos_ref, up_ref, dn_ref, o_ref, q_s, k_s, v_s):
    seq = a_ref.shape[1]
    blk = SWA_BLOCK
    hd = SWA_HEAD_DIM
    cosv, upv, dnv = cos_ref[...], up_ref[...], dn_ref[...]

    def rope(x):
        return x * cosv + pltpu.roll(x, 128 - ROT_DIM // 2, 1) * upv + pltpu.roll(x, ROT_DIM // 2, 1) * dnv

    for slab in range(SWA_WIDTH // 128):
        qr = (rope(a_ref[0, :, slab * 128:(slab + 1) * 128]) * (hd ** -0.5)).astype(BF16)
        q_s[2 * slab] = qr[:, :hd]
        q_s[2 * slab + 1] = qr[:, hd:]
    kr = rope(a_ref[0, :, SWA_WIDTH:SWA_WIDTH + SWA_KV]).astype(BF16)
    vr = a_ref[0, :, SWA_WIDTH + SWA_KV:A_COLS].astype(BF16)
    zpad = jnp.zeros((blk, hd), BF16)
    for kh in range(SWA_KV_HEADS):
        k_s[kh, 0:blk, :] = zpad
        k_s[kh, blk + seq:2 * blk + seq, :] = zpad
        v_s[kh, 0:blk, :] = zpad
        v_s[kh, blk + seq:2 * blk + seq, :] = zpad
        k_s[kh, blk:blk + seq, :] = kr[:, kh * hd:(kh + 1) * hd]
        v_s[kh, blk:blk + seq, :] = vr[:, kh * hd:(kh + 1) * hd]

    qi = lax.broadcasted_iota(jnp.int32, (blk, 3 * blk), 0)
    kk = lax.broadcasted_iota(jnp.int32, (blk, 3 * blk), 1)
    band = jnp.abs(qi + blk - kk) <= SWA_WINDOW

    def body(n, carry):
        r0 = pl.multiple_of(n * blk, blk)
        kpos = kk + (n - 1) * blk
        valid = band & (kpos >= 0) & (kpos < seq)
        for kh in range(SWA_KV_HEADS):
            k3 = k_s[kh, pl.ds(r0, 3 * blk), :]
            v3 = v_s[kh, pl.ds(r0, 3 * blk), :]
            outs = []
            for g in range(SWA_GROUP):
                h = kh * SWA_GROUP + g
                sink = sink_ref[h]
                qq = q_s[h, pl.ds(r0, blk), :]
                s = lax.dot_general(qq, k3, _NT, preferred_element_type=F32)
                s = jnp.where(valid, s, NEG_BIG)
                m = jnp.maximum(jnp.max(s, axis=1, keepdims=True), sink)
                p = jnp.exp(s - m)
                den = jnp.sum(p, axis=1, keepdims=True) + jnp.exp(sink - m)
                outs.append(_mm(p.astype(BF16), v3) / den)
            o_ref[0, pl.ds(r0, blk), kh * SWA_GROUP * hd:(kh + 1) * SWA_GROUP * hd] = jnp.concatenate(outs, axis=1)
        return carry

    lax.fori_loop(0, seq // blk, body, 0)


def _swa(a3, sink, tables):
    b, seq, _ = a3.shape
    cos_t, up_t, dn_t = tables
    tab = pl.BlockSpec((seq, 128), lambda i, s: (0, 0))
    grid_spec = pltpu.PrefetchScalarGridSpec(
        num_scalar_prefetch=1,
        grid=(b,),
        in_specs=[pl.BlockSpec((1, seq, A_COLS), lambda i, s: (i, 0, 0)), tab, tab, tab],
        out_specs=pl.BlockSpec((1, seq, SWA_WIDTH), lambda i, s: (i, 0, 0)),
        scratch_shapes=[pltpu.VMEM((SWA_HEADS, seq, SWA_HEAD_DIM), BF16),
                        pltpu.VMEM((SWA_KV_HEADS, seq + 2 * SWA_BLOCK, SWA_HEAD_DIM), BF16),
                        pltpu.VMEM((SWA_KV_HEADS, seq + 2 * SWA_BLOCK, SWA_HEAD_DIM), BF16)],
    )
    return pl.pallas_call(
        _swa_kernel,
        grid_spec=grid_spec,
        out_shape=jax.ShapeDtypeStruct((b, seq, SWA_WIDTH), F32),
        compiler_params=_params("parallel"),
        name="swa",
    )(sink, a3, cos_t, up_t, dn_t)


def _layer_norm(x, g, b):
    mu = jnp.mean(x, axis=-1, keepdims=True)
    d = x - mu
    var = jnp.mean(d * d, axis=-1, keepdims=True)
    return d * lax.rsqrt(var + LN_EPS) * g + b


def _out_ffn_kernel(x_ref, yal_ref, yah_ref, yb_ref, yc_ref, woa_ref, wob_ref, woc_ref,
                    g1_ref, b1_ref, w1_ref, w2_ref, g2_ref, b2_ref, o_ref, acc_ref, *, alpha):
    ya = jnp.concatenate([yal_ref[...], yah_ref[...]], axis=1)
    mix = (_mm(ya.astype(BF16), woa_ref[...]) + _mm(yb_ref[...].astype(BF16), wob_ref[...])
           + _mm(yc_ref[...].astype(BF16), woc_ref[...]))
    x1 = _layer_norm(alpha * x_ref[...] + mix, g1_ref[...], b1_ref[...])
    x1b = x1.astype(BF16)
    for ci in range(D_FF // FF_CHUNK):
        sl = slice(ci * FF_CHUNK, (ci + 1) * FF_CHUNK)
        hid = jnp.maximum(_mm(x1b, w1_ref[:, sl]), 0.0)
        part = _mm((hid * hid).astype(BF16), w2_ref[sl, :])
        if ci == 0:
            acc_ref[...] = part
        else:
            acc_ref[...] += part
    o_ref[...] = _layer_norm(alpha * x1 + acc_ref[...], g2_ref[...], b2_ref[...])


def _out_ffn(x2d, yal2d, yah2d, yb2d, yc2d, woa, wob, woc, g1, b1, w1, w2, g2, b2, alpha):
    t = x2d.shape[0]
    tm = min(ROW_TILE, t)
    row = lambda n: pl.BlockSpec((tm, n), lambda i: (i, 0))
    once = lambda shape: pl.BlockSpec(shape, lambda i: (0, 0), pipeline_mode=pl.Buffered(1))
    return pl.pallas_call(
        functools.partial(_out_ffn_kernel, alpha=alpha),
        grid=(t // tm,),
        in_specs=[row(D_MODEL), row(128), row(128), row(GLA_WIDTH), row(SWA_WIDTH),
                  once((S5_WIDTH, D_MODEL)), once((GLA_WIDTH, D_MODEL)), once((SWA_WIDTH, D_MODEL)),
                  once((1, D_MODEL)), once((1, D_MODEL)),
                  once((D_MODEL, D_FF)), once((D_FF, D_MODEL)),
                  once((1, D_MODEL)), once((1, D_MODEL))],
        out_specs=row(D_MODEL),
        out_shape=jax.ShapeDtypeStruct((t, D_MODEL), F32),
        scratch_shapes=[pltpu.VMEM((tm, D_MODEL), F32)],
        compiler_params=_params("parallel"),
        name="out_ffn",
    )(x2d, yal2d, yah2d, yb2d, yc2d, woa, wob, woc, g1, b1, w1, w2, g2, b2)


def _pack_w_in(w_in):
    points = np.cumsum([S5_WIDTH, GLA_QK, GLA_QK, GLA_WIDTH, GLA_WIDTH, GLA_RANK, GLA_RANK,
                        SWA_WIDTH, SWA_KV]).tolist()
    s5, gq, gk, gv, gr, glf, glb, aq, ak, av = jnp.split(w_in, points, axis=-1)
    pad = jnp.zeros((w_in.shape[0], 128 - 2 * GLA_RANK), w_in.dtype)
    return jnp.concatenate([s5, gq, gk, gv, gr, glf, glb, pad, aq, ak, av], axis=-1).astype(BF16)


def _pad_gate_w(w_a):
    z = jnp.zeros((128, GLA_QK), F32)
    waf = z.at[0:GLA_RANK].set(w_a[0].astype(F32))
    wab = z.at[GLA_RANK:2 * GLA_RANK].set(w_a[1].astype(F32))
    return waf.astype(BF16), wab.astype(BF16)


def _layer(x2d, bsz, seq, alpha, w_in, a_re, a_im, log_step, b_re, b_im, c_re, c_im, d_skip,
           w_glu, b_glu, gla_w_a, gla_b_a, gla_ln_g, swa_sink, w_out, ln1_g, ln1_b, w_ff1, w_ff2,
           ln2_g, ln2_b, gla_consts, rope_tabs):
    t = bsz * seq
    row = lambda v: v.astype(F32)[None, :]
    ul, uh, g, a = _inproj(x2d, _pack_w_in(w_in))

    mats = _s5_matrices(a_re, a_im, log_step, b_re, b_im, c_re, c_im, d_skip, seq // S5_CHUNK)
    yal, yah = _s5(ul.reshape(bsz, seq, 128), uh.reshape(bsz, seq, 128), mats, w_glu.astype(BF16), row(b_glu))

    waf, wab = _pad_gate_w(gla_w_a)
    yb = _gla(g.reshape(bsz, seq, G_COLS), waf, wab, gla_b_a.astype(F32), gla_consts, row(gla_ln_g))
    yc = _swa(a.reshape(bsz, seq, A_COLS), swa_sink.astype(F32), rope_tabs)

    wo = w_out.astype(BF16)
    return _out_ffn(x2d, yal.reshape(t, 128), yah.reshape(t, 128), yb.reshape(t, GLA_WIDTH), yc.reshape(t, SWA_WIDTH),
                    wo[0:S5_WIDTH], wo[S5_WIDTH:S5_WIDTH + GLA_WIDTH], wo[S5_WIDTH + GLA_WIDTH:],
                    row(ln1_g), row(ln1_b), w_ff1.astype(BF16), w_ff2.astype(BF16),
                    row(ln2_g), row(ln2_b), alpha)


def kernel(x, w_in, s5_a_re, s5_a_im, s5_log_step, s5_b_re, s5_b_im, s5_c_re, s5_c_im, s5_d, s5_w_glu, s5_b_glu, gla_w_a, gla_b_a, gla_ln_g, swa_sink, w_out, ln1_g, ln1_b, w_ff1, w_ff2, ln2_g, ln2_b):
    bsz, seq, _ = x.shape
    depth = w_in.shape[0]
    alpha = (2 * depth) ** 0.25
    gla_consts = _gla_constants()
    rope_tabs = _rope_tables(seq)
    h = x.reshape(bsz * seq, D_MODEL)
    for l in range(depth):
        h = _layer(h, bsz, seq, alpha, w_in[l], s5_a_re[l], s5_a_im[l], s5_log_step[l],
                   s5_b_re[l], s5_b_im[l], s5_c_re[l], s5_c_im[l], s5_d[l], s5_w_glu[l], s5_b_glu[l],
                   gla_w_a[l], gla_b_a[l], gla_ln_g[l], swa_sink[l], w_out[l], ln1_g[l], ln1_b[l],
                   w_ff1[l], w_ff2[l], ln2_g[l], ln2_b[l], gla_consts, rope_tabs)
    return h.reshape(bsz, seq, D_MODEL)
```

```python
import functools
import math

import jax
import jax.numpy as jnp
import numpy as np
from jax import lax
from jax.experimental import pallas as pl
from jax.experimental.pallas import tpu as pltpu

F32 = jnp.float32
BF16 = jnp.bfloat16

D_MODEL = 1024
S5_WIDTH = 256
S5_GROUP = 16
S5_GROUPS = 16
S5_STATE = 64
S5_CHUNK = 16
GLA_HEADS = 4
GLA_DK = 32
GLA_DV = 64
GLA_QK = 128
GLA_WIDTH = 256
GLA_RANK = 16
GLA_TAU = 16.0
GLA_CHUNK = 64
SWA_WIDTH = 512
SWA_HEAD_DIM = 64
SWA_HEADS = 8
SWA_KV_HEADS = 2
SWA_GROUP = SWA_HEADS // SWA_KV_HEADS
SWA_KV = 128
SWA_WINDOW = 128
SWA_BLOCK = 128
ROT_DIM = 16
ROPE_THETA = 500000.0
D_FF = 4096
LN_EPS = 1e-5
NEG_BIG = -1e30

U_COLS = S5_WIDTH
G_COLS = GLA_QK + GLA_QK + GLA_WIDTH + GLA_WIDTH + 128
A_COLS = SWA_WIDTH + SWA_KV + SWA_KV
IN_COLS = U_COLS + G_COLS + A_COLS

VMEM_LIMIT = 56 * 1024 * 1024
ROW_TILE = 512
FF_CHUNK = 512
SWA_ROWS = 64

_NT = (((1,), (1,)), ((), ()))
_TN = (((0,), (0,)), ((), ()))


def _mm(a, b):
    return jnp.dot(a, b, preferred_element_type=F32)


def _params(*sem):
    return pltpu.CompilerParams(dimension_semantics=sem, vmem_limit_bytes=VMEM_LIMIT)


def _resident(shape):
    nd = len(shape)
    return pl.BlockSpec(shape, lambda *_: (0,) * nd)


def _inproj_kernel(x_ref, w_ref, ul_ref, uh_ref, g_ref, a_ref):
    x = x_ref[...].astype(BF16)
    u = _mm(x, w_ref[:, 0:U_COLS])
    ul_ref[...] = u[:, :128]
    uh_ref[...] = u[:, 128:]
    g_ref[...] = _mm(x, w_ref[:, U_COLS:U_COLS + G_COLS])
    a_ref[...] = _mm(x, w_ref[:, U_COLS + G_COLS:IN_COLS])


def _inproj(x2d, w_packed):
    t = x2d.shape[0]
    tm = min(ROW_TILE, t)
    row = lambda n: pl.BlockSpec((tm, n), lambda i: (i, 0))
    return pl.pallas_call(
        _inproj_kernel,
        grid=(t // tm,),
        in_specs=[row(D_MODEL), _resident((D_MODEL, IN_COLS))],
        out_specs=[row(128), row(128), row(G_COLS), row(A_COLS)],
        out_shape=[jax.ShapeDtypeStruct((t, 128), F32), jax.ShapeDtypeStruct((t, 128), F32),
                   jax.ShapeDtypeStruct((t, G_COLS), F32), jax.ShapeDtypeStruct((t, A_COLS), F32)],
        compiler_params=_params("parallel"),
        name="inproj",
    )(x2d, w_packed)


def _s5_matrices(a_re, a_im, log_step, b_re, b_im, c_re, c_im, d_skip, nch):
    hp = lax.Precision.HIGHEST
    c = S5_CHUNK
    ar, ai = a_re.astype(F32), a_im.astype(F32)
    step = jnp.exp(log_step.astype(F32))
    lr, li = ar * step, ai * step
    kk = jnp.arange(c + 1, dtype=F32)[:, None, None, None]
    mag = jnp.exp(kk * lr[None])
    pr, pi = mag * jnp.cos(kk * li[None]), mag * jnp.sin(kk * li[None])
    nr, ni = pr[1] - 1.0, pi[1]
    den = ar * ar + ai * ai
    qr, qi = (nr * ar + ni * ai) / den, (ni * ar - nr * ai) / den
    br, bi = b_re.astype(F32), b_im.astype(F32)
    bbr = qr[..., None] * br - qi[..., None] * bi
    bbi = qr[..., None] * bi + qi[..., None] * br
    cr, ci = c_re.astype(F32), c_im.astype(F32)

    cpr = cr[None] * pr[:c, :, :, None, :] - ci[None] * pi[:c, :, :, None, :]
    cpi = cr[None] * pi[:c, :, :, None, :] + ci[None] * pr[:c, :, :, None, :]
    kern = (jnp.einsum('kzgop,zgpi->zkgoi', cpr, bbr, precision=hp)
            - jnp.einsum('kzgop,zgpi->zkgoi', cpi, bbi, precision=hp))
    kf, kb = kern[0], kern[1]
    lag = jnp.concatenate([kb[1:][::-1], (kf[0] + kb[0])[None], kf[1:]], axis=0)
    jj = np.arange(c)[:, None]
    ii = np.arange(c)[None, :]
    t4 = lag[(ii - jj) + c - 1]
    tt_mat = jnp.transpose(t4, (2, 1, 3, 0, 4)).reshape(S5_GROUPS, c * S5_GROUP, c * S5_GROUP)

    def cmul_b(wr, wi, z):
        return (wr[..., None] * bbr[z][None] - wi[..., None] * bbi[z][None],
                wr[..., None] * bbi[z][None] + wi[..., None] * bbr[z][None])
    sfr, sfi = cmul_b(pr[:c][::-1, 0], pi[:c][::-1, 0], 0)
    sbr, sbi = cmul_b(pr[:c][:, 1], pi[:c][:, 1], 1)
    s4 = jnp.stack([sfr, sbr, sfi, sbi], axis=0)
    s_mat = jnp.transpose(s4, (2, 1, 4, 0, 3)).reshape(S5_GROUPS, c * S5_GROUP, 4 * S5_STATE)

    def cmul_c(wr, wi, z):
        return (cr[z][None] * wr[:, :, None, :] - ci[z][None] * wi[:, :, None, :],
                cr[z][None] * wi[:, :, None, :] + ci[z][None] * wr[:, :, None, :])
    ofr, ofi = cmul_c(pr[1:c + 1, 0], pi[1:c + 1, 0], 0)
    obr, obi = cmul_c(pr[1:c + 1][::-1, 1], pi[1:c + 1][::-1, 1], 1)
    o4 = jnp.stack([ofr, obr, -ofi, -obi], axis=0)
    ot_mat = jnp.transpose(o4, (2, 1, 3, 0, 4)).reshape(S5_GROUPS, c * S5_GROUP, 4 * S5_STATE)

    a_r = jnp.concatenate([pr[c, 0], pr[c, 1]], axis=-1)
    a_i = jnp.concatenate([pi[c, 0], pi[c, 1]], axis=-1)
    d_col = jnp.broadcast_to(jnp.tile(d_skip.astype(F32), (1, c))[:, :, None], (S5_GROUPS, c * S5_GROUP, nch))
    return tt_mat.astype(BF16), s_mat.astype(BF16), ot_mat.astype(BF16), a_r, a_i, d_col


def _gelu_tanh(x):
    return 0.5 * x * (1.0 + jnp.tanh(math.sqrt(2.0 / math.pi) * (x + 0.044715 * (x * x * x))))


def _sigmoid(x):
    return 1.0 / (1.0 + jnp.exp(-x))


def _s5_kernel(ul_ref, uh_ref, tt_ref, s_ref, ot_ref, ar_ref, ai_ref, dcol_ref, wglu_ref, bglu_ref,
               yl_ref, yh_ref, d_scr, cr_scr, ci_scr, efr_scr, efi_scr, ebr_scr, ebi_scr, zt_scr):
    c, ng = S5_CHUNK, S5_GROUPS
    nch = ul_ref.shape[1] // c
    for j in range(c):
        tok = pl.ds(j, nch, stride=c)
        uj = jnp.concatenate([ul_ref[0, tok, :], uh_ref[0, tok, :]], axis=1)
        d_scr[:, j * S5_GROUP:(j + 1) * S5_GROUP, :] = uj.T.reshape(ng, S5_GROUP, nch)
    for g in range(ng):
        dg = d_scr[g].astype(BF16)
        contrib = lax.dot_general(dg, s_ref[g], _TN, preferred_element_type=F32)
        cr_scr[g * nch:(g + 1) * nch, :] = contrib[:, :128]
        ci_scr[g * nch:(g + 1) * nch, :] = contrib[:, 128:]

    ar, ai = ar_ref[...], ai_ref[...]
    fwd = lax.broadcasted_iota(jnp.int32, (ng, 128), 1) < S5_STATE

    def step(k, carry):
        sre, sim = carry
        up = pl.ds(k, ng, stride=nch)
        dn = pl.ds(nch - 1 - k, ng, stride=nch)
        efr_scr[up, :] = sre
        efi_scr[up, :] = sim
        ebr_scr[dn, :] = sre
        ebi_scr[dn, :] = sim
        cre = jnp.where(fwd, cr_scr[up, :], cr_scr[dn, :])
        cim = jnp.where(fwd, ci_scr[up, :], ci_scr[dn, :])
        return ar * sre - ai * sim + cre, ar * sim + ai * sre + cim

    zero = jnp.zeros((ng, 128), F32)
    lax.fori_loop(0, nch, step, (zero, zero), unroll=8)

    fwd_all = lax.broadcasted_iota(jnp.int32, (nch, 128), 1) < S5_STATE
    for g in range(ng):
        dg32 = d_scr[g]
        rows = slice(g * nch, (g + 1) * nch)
        enter = jnp.concatenate([jnp.where(fwd_all, efr_scr[rows, :], ebr_scr[rows, :]),
                                 jnp.where(fwd_all, efi_scr[rows, :], ebi_scr[rows, :])],
                                axis=1).astype(BF16)
        yt = (_mm(tt_ref[g], dg32.astype(BF16))
              + lax.dot_general(ot_ref[g], enter, _NT, preferred_element_type=F32)
              + dcol_ref[g] * dg32)
        zt_scr[:, g * S5_GROUP:(g + 1) * S5_GROUP, :] = _gelu_tanh(yt).reshape(c, S5_GROUP, nch)
    for i in range(c):
        zi = zt_scr[i].astype(BF16)
        glu = lax.dot_general(zi, wglu_ref[...], _TN, preferred_element_type=F32) + bglu_ref[...]
        ya = glu[:, :S5_WIDTH] * _sigmoid(glu[:, S5_WIDTH:])
        tok = pl.ds(i, nch, stride=c)
        yl_ref[0, tok, :] = ya[:, :128]
        yh_ref[0, tok, :] = ya[:, 128:]


def _s5(ul3, uh3, mats, wglu, bglu):
    b, seq, _ = ul3.shape
    nch = seq // S5_CHUNK
    tt_mat, s_mat, ot_mat, a_r, a_i, d_col = mats
    w = S5_CHUNK * S5_GROUP
    half = pl.BlockSpec((1, seq, 128), lambda i: (i, 0, 0))
    state_scr = pltpu.VMEM((S5_GROUPS * nch, 2 * S5_STATE), F32)
    return pl.pallas_call(
        _s5_kernel,
        grid=(b,),
        in_specs=[half, half,
                  _resident((S5_GROUPS, w, w)), _resident((S5_GROUPS, w, 4 * S5_STATE)),
                  _resident((S5_GROUPS, w, 4 * S5_STATE)),
                  _resident((S5_GROUPS, 128)), _resident((S5_GROUPS, 128)),
                  _resident((S5_GROUPS, w, nch)),
                  _resident((S5_WIDTH, 2 * S5_WIDTH)), _resident((1, 2 * S5_WIDTH))],
        out_specs=[half, half],
        out_shape=[jax.ShapeDtypeStruct((b, seq, 128), F32)] * 2,
        scratch_shapes=[pltpu.VMEM((S5_GROUPS, w, nch), F32)] + [state_scr] * 6
                       + [pltpu.VMEM((S5_CHUNK, w, nch), F32)],
        compiler_params=_params("parallel"),
        name="s5",
    )(ul3, uh3, tt_mat, s_mat, ot_mat, a_r, a_i, d_col, wglu, bglu)


def _gla_constants():
    r = np.arange(256)
    same = (r[:, None] // GLA_CHUNK) == (r[None, :] // GLA_CHUNK)
    prefix = same & (r[None, :] <= r[:, None])
    suffix = same & (r[None, :] >= r[:, None])
    avg = ((r[:, None] // GLA_DV) == (r[None, :] // GLA_DV)) / float(GLA_DV)
    as_bf = lambda m: jnp.asarray(m.astype(np.float32), BF16)
    return as_bf(prefix), as_bf(suffix), as_bf(same), as_bf(avg)


def _log_sigmoid(x):
    return jnp.minimum(x, 0.0) - jnp.log1p(jnp.exp(-jnp.abs(x)))


def _split_mm(a, b):
    hi = a.astype(BF16)
    lo = (a - hi.astype(F32)).astype(BF16)
    return _mm(hi, b) + _mm(lo, b)


def _chunk_sums(mat, la):
    hi = la.astype(BF16)
    lo = (la - hi.astype(F32)).astype(BF16)
    out = []
    for i in range(la.shape[0] // 256):
        sl = slice(i * 256, (i + 1) * 256)
        out.append(_mm(mat, hi[sl]) + _mm(mat, lo[sl]))
    return jnp.concatenate(out, axis=0)


def _gla_kernel(g_ref, waf_ref, wab_ref, ba_ref, pre_ref, suf_ref, tot_ref, avg_ref, lng_ref, o_ref,
                qf_s, kfi_s, kfs_s, qb_s, kbi_s, kbs_s, v_s, totf_s, totb_s, of_s, ob_s):
    seq = g_ref.shape[1]
    nchunk = seq // GLA_CHUNK
    c = GLA_CHUNK
    q = g_ref[0, :, 0:128] * (GLA_DK ** -0.5)
    k = g_ref[0, :, 128:256]
    lfb = g_ref[0, :, 768:896].astype(BF16)
    la_f = _log_sigmoid(_mm(lfb, waf_ref[...]) + ba_ref[0:1, :]) * (1.0 / GLA_TAU)
    la_b = _log_sigmoid(_mm(lfb, wab_ref[...]) + ba_ref[1:2, :]) * (1.0 / GLA_TAU)

    bf = _chunk_sums(pre_ref[...], la_f)
    tf = _chunk_sums(tot_ref[...], la_f)
    qf_s[...] = (q * jnp.exp(bf)).astype(BF16)
    kfi_s[...] = (k * jnp.exp(-bf)).astype(BF16)
    kfs_s[...] = (k * jnp.exp(tf - bf)).astype(BF16)
    totf_s[...] = tf
    bs = _chunk_sums(suf_ref[...], la_b)
    tb = _chunk_sums(tot_ref[...], la_b)
    qb_s[...] = (q * jnp.exp(bs)).astype(BF16)
    kbi_s[...] = (k * jnp.exp(-bs)).astype(BF16)
    kbs_s[...] = (k * jnp.exp(tb - bs)).astype(BF16)
    totb_s[...] = tb
    v_s[...] = g_ref[0, :, 256:512].astype(BF16)

    r256 = lax.broadcasted_iota(jnp.int32, (4 * c, 128), 0)
    l128 = lax.broadcasted_iota(jnp.int32, (4 * c, 128), 1)
    head_k = (r256 // c) == (l128 // GLA_DK)
    r256v = lax.broadcasted_iota(jnp.int32, (4 * c, 256), 0)
    l256v = lax.broadcasted_iota(jnp.int32, (4 * c, 256), 1)
    head_v = (r256v // c) == (l256v // GLA_DV)
    qi = lax.broadcasted_iota(jnp.int32, (c, 4 * c), 0)
    kj = lax.broadcasted_iota(jnp.int32, (c, 4 * c), 1) % c
    causal_f = kj <= qi
    causal_b = kj > qi

    def one_dir(r0, q_s, ki_s, ks_s, tot_s, out_s, causal, state):
        qq = q_s[pl.ds(r0, c), :]
        ki = ki_s[pl.ds(r0, c), :]
        ks = ks_s[pl.ds(r0, c), :]
        vv = v_s[pl.ds(r0, c), :]
        kexp = jnp.where(head_k, jnp.concatenate([ki] * 4, axis=0), jnp.zeros((), BF16))
        vexp = jnp.where(head_v, jnp.concatenate([vv] * 4, axis=0), jnp.zeros((), BF16))
        sc = lax.dot_general(qq, kexp, _NT, preferred_element_type=F32)
        pm = jnp.where(causal, sc, 0.0).astype(BF16)
        o = _mm(pm, vexp) + lax.dot_general(qq, state.astype(BF16), _NT, preferred_element_type=F32)
        out_s[pl.ds(r0, c), :] = o
        kv_t = lax.dot_general(vv, ks, _TN, preferred_element_type=F32)
        dec = jnp.exp(tot_s[pl.ds(r0, 1), :])
        return dec * state + jnp.where(head_k, kv_t, 0.0)

    def body(n, carry):
        st_f, st_b = carry
        r0 = pl.multiple_of(n * c, c)
        r1 = pl.multiple_of((nchunk - 1 - n) * c, c)
        st_f = one_dir(r0, qf_s, kfi_s, kfs_s, totf_s, of_s, causal_f, st_f)
        st_b = one_dir(r1, qb_s, kbi_s, kbs_s, totb_s, ob_s, causal_b, st_b)
        return st_f, st_b

    zero = jnp.zeros((GLA_WIDTH, GLA_QK), F32)
    lax.fori_loop(0, nchunk, body, (zero, zero))

    o = of_s[...] + ob_s[...]
    avg = avg_ref[...]
    mu = _split_mm(o, avg)
    dlt = o - mu
    var = _split_mm(dlt * dlt, avg)
    on = dlt * lax.rsqrt(var + LN_EPS)
    r = g_ref[0, :, 512:768]
    o_ref[0] = on * lng_ref[...] * (r * _sigmoid(r))


def _gla(g3, waf, wab, ba, consts, ln_g):
    b, seq, _ = g3.shape
    pre, suf, tot, avg = consts
    bf = lambda w: pltpu.VMEM((seq, w), BF16)
    ff = lambda w: pltpu.VMEM((seq, w), F32)
    return pl.pallas_call(
        _gla_kernel,
        grid=(b,),
        in_specs=[pl.BlockSpec((1, seq, G_COLS), lambda i: (i, 0, 0)),
                  _resident((128, 128)), _resident((128, 128)), _resident((2, 128)),
                  _resident((256, 256)), _resident((256, 256)), _resident((256, 256)),
                  _resident((256, 256)), _resident((1, 256))],
        out_specs=pl.BlockSpec((1, seq, GLA_WIDTH), lambda i: (i, 0, 0)),
        out_shape=jax.ShapeDtypeStruct((b, seq, GLA_WIDTH), F32),
        scratch_shapes=[bf(128)] * 6 + [bf(256), ff(128), ff(128), ff(256), ff(256)],
        compiler_params=_params("parallel"),
        name="gla",
    )(g3, waf, wab, ba, pre, suf, tot, avg, ln_g)


def _rope_tables(seq):
    pos = jnp.arange(seq, dtype=F32)
    inv_freq = ROPE_THETA ** (-jnp.arange(0, ROT_DIM, 2, dtype=F32) / ROT_DIM)
    ang = pos[:, None] * inv_freq[None, :]
    cos, sin = jnp.cos(ang), jnp.sin(ang)
    half = ROT_DIM // 2
    pad = jnp.zeros((seq, SWA_HEAD_DIM - ROT_DIM), F32)
    zeros = jnp.zeros((seq, half), F32)
    c_head = jnp.concatenate([cos, cos, pad + 1.0], axis=1)
    up_head = jnp.concatenate([-sin, zeros, pad], axis=1)
    dn_head = jnp.concatenate([zeros, sin, pad], axis=1)
    two = lambda t: jnp.concatenate([t, t], axis=1)
    return two(c_head), two(up_head), two(dn_head)


def _swa_kernel(sink_ref, a_ref, cos_ref, up_ref, dn_ref, o_ref,
                q_s, klo_s, khi_s, vd_s, s_s, p_s, bias_s, m_s, l_s):
    seq = a_ref.shape[1]
    blk = SWA_BLOCK
    hd = SWA_HEAD_DIM
    pairs = SWA_GROUP // 2
    cosv, upv, dnv = cos_ref[...], up_ref[...], dn_ref[...]

    def rope(x):
        return x * cosv + pltpu.roll(x, 128 - ROT_DIM // 2, 1) * upv + pltpu.roll(x, ROT_DIM // 2, 1) * dnv

    for slab in range(SWA_WIDTH // 128):
        q_s[slab] = (rope(a_ref[0, :, slab * 128:(slab + 1) * 128]) * (hd ** -0.5)).astype(BF16)
    kr = rope(a_ref[0, :, SWA_WIDTH:SWA_WIDTH + SWA_KV])
    vr = a_ref[0, :, SWA_WIDTH + SWA_KV:A_COLS]
    ksw = pltpu.roll(kr, hd, 1)
    vsw = pltpu.roll(vr, hd, 1)
    low = lax.broadcasted_iota(jnp.int32, (seq, 128), 1) < hd
    body = slice(blk, blk + seq)
    zpad = jnp.zeros((blk, 128), BF16)
    for ref in (klo_s, khi_s, vd_s):
        for kh in range(SWA_KV_HEADS):
            ref[kh, 0:blk, :] = zpad
            ref[kh, blk + seq:2 * blk + seq, :] = zpad
    klo_s[0, body, :] = jnp.where(low, kr, 0.0).astype(BF16)
    khi_s[0, body, :] = jnp.where(low, 0.0, ksw).astype(BF16)
    klo_s[1, body, :] = jnp.where(low, ksw, 0.0).astype(BF16)
    khi_s[1, body, :] = jnp.where(low, 0.0, kr).astype(BF16)
    vd_s[0, body, :] = jnp.where(low, vr, vsw).astype(BF16)
    vd_s[1, body, :] = jnp.where(low, vsw, vr).astype(BF16)

    qi = lax.broadcasted_iota(jnp.int32, (blk, 3 * blk), 0)
    kk = lax.broadcasted_iota(jnp.int32, (blk, 3 * blk), 1)
    band = jnp.abs(qi + blk - kk) <= SWA_WINDOW
    low_o = lax.broadcasted_iota(jnp.int32, (blk, 128), 1) < hd
    rows = SWA_GROUP * blk
    chunks = [(kh, rb) for kh in range(SWA_KV_HEADS) for rb in range(rows // SWA_ROWS)]

    def biased(kh, rb):
        boff = (rb * SWA_ROWS) % blk
        return s_s[kh, rb * SWA_ROWS:(rb + 1) * SWA_ROWS, :] + bias_s[boff:boff + SWA_ROWS, :]

    def body_fn(n, carry):
        r0 = pl.multiple_of(n * blk, blk)
        kpos = kk + (n - 1) * blk
        bias_s[...] = jnp.where(band & (kpos >= 0) & (kpos < seq), 0.0, NEG_BIG)
        for kh in range(SWA_KV_HEADS):
            kcat = jnp.concatenate([klo_s[kh, pl.ds(r0, 3 * blk), :], khi_s[kh, pl.ds(r0, 3 * blk), :]], axis=0)
            for pr in range(pairs):
                qq = q_s[kh * pairs + pr, pl.ds(r0, blk), :]
                s2 = lax.dot_general(qq, kcat, _NT, preferred_element_type=F32)
                s_s[kh, (2 * pr) * blk:(2 * pr + 1) * blk, :] = s2[:, :3 * blk]
                s_s[kh, (2 * pr + 1) * blk:(2 * pr + 2) * blk, :] = s2[:, 3 * blk:]
        for kh, rb in chunks:
            rs = slice(rb * SWA_ROWS, (rb + 1) * SWA_ROWS)
            sink = sink_ref[kh * SWA_GROUP + (rb * SWA_ROWS) // blk]
            m = jnp.maximum(jnp.max(biased(kh, rb), axis=1, keepdims=True), sink)
            m_s[kh, rs, :] = jnp.broadcast_to(m, (SWA_ROWS, 128))
        for kh, rb in chunks:
            rs = slice(rb * SWA_ROWS, (rb + 1) * SWA_ROWS)
            sink = sink_ref[kh * SWA_GROUP + (rb * SWA_ROWS) // blk]
            m = m_s[kh, rs, :]
            p = jnp.exp(biased(kh, rb) - jnp.concatenate([m] * 3, axis=1))
            l_s[kh, rs, :] = jnp.sum(p, axis=1, keepdims=True) + jnp.exp(sink - m)
            p_s[kh, rs, :] = p.astype(BF16)
        for kh in range(SWA_KV_HEADS):
            o2 = _mm(p_s[kh], vd_s[kh, pl.ds(r0, 3 * blk), :]) * (1.0 / l_s[kh])
            o_ref[0, pl.ds(r0, blk), kh * SWA_GROUP * hd:(kh + 1) * SWA_GROUP * hd] = jnp.concatenate(
                [jnp.where(low_o, o2[(2 * pr) * blk:(2 * pr + 1) * blk], o2[(2 * pr + 1) * blk:(2 * pr + 2) * blk])
                 for pr in range(pairs)], axis=1)
        return carry

    lax.fori_loop(0, seq // blk, body_fn, 0)


def _swa(a3, sink, tables):
    b, seq, _ = a3.shape
    cos_t, up_t, dn_t = tables
    tab = pl.BlockSpec((seq, 128), lambda i, s: (0, 0))
    banded = pltpu.VMEM((SWA_KV_HEADS, seq + 2 * SWA_BLOCK, 128), BF16)
    stat = pltpu.VMEM((SWA_KV_HEADS, SWA_GROUP * SWA_BLOCK, 128), F32)
    grid_spec = pltpu.PrefetchScalarGridSpec(
        num_scalar_prefetch=1,
        grid=(b,),
        in_specs=[pl.BlockSpec((1, seq, A_COLS), lambda i, s: (i, 0, 0)), tab, tab, tab],
        out_specs=pl.BlockSpec((1, seq, SWA_WIDTH), lambda i, s: (i, 0, 0)),
        scratch_shapes=[pltpu.VMEM((SWA_WIDTH // 128, seq, 128), BF16), banded, banded, banded,
                        pltpu.VMEM((SWA_KV_HEADS, SWA_GROUP * SWA_BLOCK, 3 * SWA_BLOCK), F32),
                        pltpu.VMEM((SWA_KV_HEADS, SWA_GROUP * SWA_BLOCK, 3 * SWA_BLOCK), BF16),
                        pltpu.VMEM((SWA_BLOCK, 3 * SWA_BLOCK), F32), stat, stat],
    )
    return pl.pallas_call(
        _swa_kernel,
        grid_spec=grid_spec,
        out_shape=jax.ShapeDtypeStruct((b, seq, SWA_WIDTH), F32),
        compiler_params=_params("parallel"),
        name="swa",
    )(sink, a3, cos_t, up_t, dn_t)


def _layer_norm(x, g, b):
    mu = jnp.mean(x, axis=-1, keepdims=True)
    d = x - mu
    var = jnp.mean(d * d, axis=-1, keepdims=True)
    return d * lax.rsqrt(var + LN_EPS) * g + b


def _out_ffn_kernel(x_ref, yal_ref, yah_ref, yb_ref, yc_ref, woa_ref, wob_ref, woc_ref,
                    g1_ref, b1_ref, w1_ref, w2_ref, g2_ref, b2_ref, o_ref, acc_ref, *, alpha):
    ya = jnp.concatenate([yal_ref[...], yah_ref[...]], axis=1)
    mix = (_mm(ya.astype(BF16), woa_ref[...]) + _mm(yb_ref[...].astype(BF16), wob_ref[...])
           + _mm(yc_ref[...].astype(BF16), woc_ref[...]))
    x1 = _layer_norm(alpha * x_ref[...] + mix, g1_ref[...], b1_ref[...])
    x1b = x1.astype(BF16)
    for ci in range(D_FF // FF_CHUNK):
        sl = slice(ci * FF_CHUNK, (ci + 1) * FF_CHUNK)
        hid = jnp.maximum(_mm(x1b, w1_ref[:, sl]), 0.0)
        part = _mm((hid * hid).astype(BF16), w2_ref[sl, :])
        if ci == 0:
            acc_ref[...] = part
        else:
            acc_ref[...] += part
    o_ref[...] = _layer_norm(alpha * x1 + acc_ref[...], g2_ref[...], b2_ref[...])


def _out_ffn(x2d, yal2d, yah2d, yb2d, yc2d, woa, wob, woc, g1, b1, w1, w2, g2, b2, alpha):
    t = x2d.shape[0]
    tm = min(ROW_TILE, t)
    row = lambda n: pl.BlockSpec((tm, n), lambda i: (i, 0))
    once = lambda shape: pl.BlockSpec(shape, lambda i: (0, 0), pipeline_mode=pl.Buffered(1))
    return pl.pallas_call(
        functools.partial(_out_ffn_kernel, alpha=alpha),
        grid=(t // tm,),
        in_specs=[row(D_MODEL), row(128), row(128), row(GLA_WIDTH), row(SWA_WIDTH),
                  once((S5_WIDTH, D_MODEL)), once((GLA_WIDTH, D_MODEL)), once((SWA_WIDTH, D_MODEL)),
                  once((1, D_MODEL)), once((1, D_MODEL)),
                  once((D_MODEL, D_FF)), once((D_FF, D_MODEL)),
                  once((1, D_MODEL)), once((1, D_MODEL))],
        out_specs=row(D_MODEL),
        out_shape=jax.ShapeDtypeStruct((t, D_MODEL), F32),
        scratch_shapes=[pltpu.VMEM((tm, D_MODEL), F32)],
        compiler_params=_params("parallel"),
        name="out_ffn",
    )(x2d, yal2d, yah2d, yb2d, yc2d, woa, wob, woc, g1, b1, w1, w2, g2, b2)


def _pack_w_in(w_in):
    points = np.cumsum([S5_WIDTH, GLA_QK, GLA_QK, GLA_WIDTH, GLA_WIDTH, GLA_RANK, GLA_RANK,
                        SWA_WIDTH, SWA_KV]).tolist()
    s5, gq, gk, gv, gr, glf, glb, aq, ak, av = jnp.split(w_in, points, axis=-1)
    pad = jnp.zeros((w_in.shape[0], 128 - 2 * GLA_RANK), w_in.dtype)
    return jnp.concatenate([s5, gq, gk, gv, gr, glf, glb, pad, aq, ak, av], axis=-1).astype(BF16)


def _pad_gate_w(w_a):
    z = jnp.zeros((128, GLA_QK), F32)
    waf = z.at[0:GLA_RANK].set(w_a[0].astype(F32))
    wab = z.at[GLA_RANK:2 * GLA_RANK].set(w_a[1].astype(F32))
    return waf.astype(BF16), wab.astype(BF16)


def _layer(x2d, bsz, seq, alpha, w_in, a_re, a_im, log_step, b_re, b_im, c_re, c_im, d_skip,
           w_glu, b_glu, gla_w_a, gla_b_a, gla_ln_g, swa_sink, w_out, ln1_g, ln1_b, w_ff1, w_ff2,
           ln2_g, ln2_b, gla_consts, rope_tabs):
    t = bsz * seq
    row = lambda v: v.astype(F32)[None, :]
    ul, uh, g, a = _inproj(x2d, _pack_w_in(w_in))

    mats = _s5_matrices(a_re, a_im, log_step, b_re, b_im, c_re, c_im, d_skip, seq // S5_CHUNK)
    yal, yah = _s5(ul.reshape(bsz, seq, 128), uh.reshape(bsz, seq, 128), mats, w_glu.astype(BF16), row(b_glu))

    waf, wab = _pad_gate_w(gla_w_a)
    yb = _gla(g.reshape(bsz, seq, G_COLS), waf, wab, gla_b_a.astype(F32), gla_consts, row(gla_ln_g))
    yc = _swa(a.reshape(bsz, seq, A_COLS), swa_sink.astype(F32), rope_tabs)

    wo = w_out.astype(BF16)
    return _out_ffn(x2d, yal.reshape(t, 128), yah.reshape(t, 128), yb.reshape(t, GLA_WIDTH), yc.reshape(t, SWA_WIDTH),
                    wo[0:S5_WIDTH], wo[S5_WIDTH:S5_WIDTH + GLA_WIDTH], wo[S5_WIDTH + GLA_WIDTH:],
                    row(ln1_g), row(ln1_b), w_ff1.astype(BF16), w_ff2.astype(BF16),
                    row(ln2_g), row(ln2_b), alpha)


def kernel(x, w_in, s5_a_re, s5_a_im, s5_log_step, s5_b_re, s5_b_im, s5_c_re, s5_c_im, s5_d, s5_w_glu, s5_b_glu, gla_w_a, gla_b_a, gla_ln_g, swa_sink, w_out, ln1_g, ln1_b, w_ff1, w_ff2, ln2_g, ln2_b):
    bsz, seq, _ = x.shape
    depth = w_in.shape[0]
    alpha = (2 * depth) ** 0.25
    gla_consts = _gla_constants()
    rope_tabs = _rope_tables(seq)
    h = x.reshape(bsz * seq, D_MODEL)
    for l in range(depth):
        h = _layer(h, bsz, seq, alpha, w_in[l], s5_a_re[l], s5_a_im[l], s5_log_step[l],
                   s5_b_re[l], s5_b_im[l], s5_c_re[l], s5_c_im[l], s5_d[l], s5_w_glu[l], s5_b_glu[l],
                   gla_w_a[l], gla_b_a[l], gla_ln_g[l], swa_sink[l], w_out[l], ln1_g[l], ln1_b[l],
                   w_ff1[l], w_ff2[l], ln2_g[l], ln2_b[l], gla_consts, rope_tabs)
    return h.reshape(bsz, seq, D_MODEL)
```

```python
import functools
import math

import jax
import jax.numpy as jnp
import numpy as np
from jax import lax
from jax.experimental import pallas as pl
from jax.experimental.pallas import tpu as pltpu

F32 = jnp.float32
BF16 = jnp.bfloat16

D_MODEL = 1024
S5_WIDTH = 256
S5_GROUP = 16
S5_GROUPS = 16
S5_STATE = 64
S5_CHUNK = 16
GLA_HEADS = 4
GLA_DK = 32
GLA_DV = 64
GLA_QK = 128
GLA_WIDTH = 256
GLA_RANK = 16
GLA_TAU = 16.0
GLA_CHUNK = 64
SWA_WIDTH = 512
SWA_HEAD_DIM = 64
SWA_HEADS = 8
SWA_KV_HEADS = 2
SWA_GROUP = SWA_HEADS // SWA_KV_HEADS
SWA_KV = 128
SWA_WINDOW = 128
SWA_BLOCK = 128
ROT_DIM = 16
ROPE_THETA = 500000.0
D_FF = 4096
LN_EPS = 1e-5
NEG_BIG = -1e30

U_COLS = S5_WIDTH
G_COLS = GLA_QK + GLA_QK + GLA_WIDTH + GLA_WIDTH + 128
A_COLS = SWA_WIDTH + SWA_KV + SWA_KV
IN_COLS = U_COLS + G_COLS + A_COLS

VMEM_LIMIT = 56 * 1024 * 1024
ROW_TILE = 512
FF_CHUNK = 512
SWA_ROWS = 64

_NT = (((1,), (1,)), ((), ()))
_TN = (((0,), (0,)), ((), ()))


def _mm(a, b):
    return jnp.dot(a, b, preferred_element_type=F32)


def _params(*sem):
    return pltpu.CompilerParams(dimension_semantics=sem, vmem_limit_bytes=VMEM_LIMIT)


def _resident(shape):
    nd = len(shape)
    return pl.BlockSpec(shape, lambda *_: (0,) * nd)


def _inproj_kernel(x_ref, w_ref, ul_ref, uh_ref, g_ref, a_ref):
    x = x_ref[...].astype(BF16)
    u = _mm(x, w_ref[:, 0:U_COLS])
    ul_ref[...] = u[:, :128]
    uh_ref[...] = u[:, 128:]
    g_ref[...] = _mm(x, w_ref[:, U_COLS:U_COLS + G_COLS])
    a_ref[...] = _mm(x, w_ref[:, U_COLS + G_COLS:IN_COLS])


def _inproj(x2d, w_packed):
    t = x2d.shape[0]
    tm = min(ROW_TILE, t)
    row = lambda n: pl.BlockSpec((tm, n), lambda i: (i, 0))
    return pl.pallas_call(
        _inproj_kernel,
        grid=(t // tm,),
        in_specs=[row(D_MODEL), _resident((D_MODEL, IN_COLS))],
        out_specs=[row(128), row(128), row(G_COLS), row(A_COLS)],
        out_shape=[jax.ShapeDtypeStruct((t, 128), F32), jax.ShapeDtypeStruct((t, 128), F32),
                   jax.ShapeDtypeStruct((t, G_COLS), F32), jax.ShapeDtypeStruct((t, A_COLS), F32)],
        compiler_params=_params("parallel"),
        name="inproj",
    )(x2d, w_packed)


def _s5_matrices(a_re, a_im, log_step, b_re, b_im, c_re, c_im, d_skip, nch):
    c, ng, w = S5_CHUNK, S5_GROUPS, S5_CHUNK * S5_GROUP
    ar, ai = a_re.astype(F32), a_im.astype(F32)
    step = jnp.exp(log_step.astype(F32))
    lr, li = ar * step, ai * step

    def lam_pow(z, k):
        kk = jnp.asarray(k, F32)[None, :, None, None]
        mag = jnp.exp(kk * lr[z][:, None, None, :])
        ang = kk * li[z][:, None, None, :]
        return mag * jnp.cos(ang), mag * jnp.sin(ang)

    def cmul(xr, xi, yr, yi):
        return xr * yr - xi * yi, xr * yi + xi * yr

    l1r, l1i = jnp.exp(lr) * jnp.cos(li), jnp.exp(lr) * jnp.sin(li)
    nr, ni = l1r - 1.0, l1i
    den = ar * ar + ai * ai
    qr, qi = (nr * ar + ni * ai) / den, (ni * ar - nr * ai) / den
    bt_r = jnp.swapaxes(b_re.astype(F32), -1, -2)
    bt_i = jnp.swapaxes(b_im.astype(F32), -1, -2)
    bbr, bbi = cmul(qr[:, :, None, :], qi[:, :, None, :], bt_r, bt_i)
    cr, ci = c_re.astype(F32), c_im.astype(F32)

    idx = np.arange(c)
    flat = lambda t: t.reshape(ng, w, S5_STATE)

    def with_c(z, k):
        pr, pi = lam_pow(z, k)
        xr, xi = cmul(pr, pi, cr[z][:, None], ci[z][:, None])
        return flat(xr), flat(xi)

    def with_b(z, k):
        pr, pi = lam_pow(z, k)
        yr, yi = cmul(pr, pi, bbr[z][:, None], bbi[z][:, None])
        return flat(yr), flat(yi)

    def re_prod(x, y):
        return jnp.einsum('gmk,gnk->gmn', jnp.concatenate([x[0], -x[1]], axis=-1),
                          jnp.concatenate([y[0], y[1]], axis=-1), precision=lax.Precision.HIGHEST)

    ri = np.arange(w)[:, None] // S5_GROUP
    cj = np.arange(w)[None, :] // S5_GROUP
    tt_mat = (jnp.where(ri >= cj, re_prod(with_c(0, idx), with_b(0, -idx)), 0.0)
              + jnp.where(cj >= ri, re_prod(with_c(1, -idx), with_b(1, idx)), 0.0))

    sf, sb = with_b(0, c - 1 - idx), with_b(1, idx)
    s_mat = jnp.concatenate([sf[0], sb[0], sf[1], sb[1]], axis=-1)
    of, ob = with_c(0, idx + 1), with_c(1, c - idx)
    ot_mat = jnp.concatenate([of[0], ob[0], -of[1], -ob[1]], axis=-1)

    pcr = [lam_pow(z, [c]) for z in range(2)]
    a_r = jnp.concatenate([pcr[0][0][:, 0, 0, :], pcr[1][0][:, 0, 0, :]], axis=-1)
    a_i = jnp.concatenate([pcr[0][1][:, 0, 0, :], pcr[1][1][:, 0, 0, :]], axis=-1)
    d_col = jnp.broadcast_to(jnp.tile(d_skip.astype(F32), (1, c))[:, :, None], (ng, w, nch))
    return tt_mat.astype(BF16), s_mat.astype(BF16), ot_mat.astype(BF16), a_r, a_i, d_col


def _gelu_tanh(x):
    return 0.5 * x * (1.0 + jnp.tanh(math.sqrt(2.0 / math.pi) * (x + 0.044715 * (x * x * x))))


def _sigmoid(x):
    return 1.0 / (1.0 + jnp.exp(-x))


def _s5_kernel(ul_ref, uh_ref, tt_ref, s_ref, ot_ref, ar_ref, ai_ref, dcol_ref, wglu_ref, bglu_ref,
               yl_ref, yh_ref, d_scr, cr_scr, ci_scr, efr_scr, efi_scr, ebr_scr, ebi_scr, zt_scr):
    c, ng = S5_CHUNK, S5_GROUPS
    nch = ul_ref.shape[1] // c
    for j in range(c):
        tok = pl.ds(j, nch, stride=c)
        uj = jnp.concatenate([ul_ref[0, tok, :], uh_ref[0, tok, :]], axis=1)
        d_scr[:, j * S5_GROUP:(j + 1) * S5_GROUP, :] = uj.T.reshape(ng, S5_GROUP, nch)
    for g in range(ng):
        dg = d_scr[g].astype(BF16)
        contrib = lax.dot_general(dg, s_ref[g], _TN, preferred_element_type=F32)
        cr_scr[g * nch:(g + 1) * nch, :] = contrib[:, :128]
        ci_scr[g * nch:(g + 1) * nch, :] = contrib[:, 128:]

    ar, ai = ar_ref[...], ai_ref[...]
    fwd = lax.broadcasted_iota(jnp.int32, (ng, 128), 1) < S5_STATE

    def step(k, carry):
        sre, sim = carry
        up = pl.ds(k, ng, stride=nch)
        dn = pl.ds(nch - 1 - k, ng, stride=nch)
        efr_scr[up, :] = sre
        efi_scr[up, :] = sim
        ebr_scr[dn, :] = sre
        ebi_scr[dn, :] = sim
        cre = jnp.where(fwd, cr_scr[up, :], cr_scr[dn, :])
        cim = jnp.where(fwd, ci_scr[up, :], ci_scr[dn, :])
        return ar * sre - ai * sim + cre, ar * sim + ai * sre + cim

    zero = jnp.zeros((ng, 128), F32)
    lax.fori_loop(0, nch, step, (zero, zero), unroll=8)

    fwd_all = lax.broadcasted_iota(jnp.int32, (nch, 128), 1) < S5_STATE
    for g in range(ng):
        dg32 = d_scr[g]
        rows = slice(g * nch, (g + 1) * nch)
        enter = jnp.concatenate([jnp.where(fwd_all, efr_scr[rows, :], ebr_scr[rows, :]),
                                 jnp.where(fwd_all, efi_scr[rows, :], ebi_scr[rows, :])],
                                axis=1).astype(BF16)
        yt = (_mm(tt_ref[g], dg32.astype(BF16))
              + lax.dot_general(ot_ref[g], enter, _NT, preferred_element_type=F32)
              + dcol_ref[g] * dg32)
        zt_scr[:, g * S5_GROUP:(g + 1) * S5_GROUP, :] = _gelu_tanh(yt).reshape(c, S5_GROUP, nch)
    for i in range(c):
        zi = zt_scr[i].astype(BF16)
        glu = lax.dot_general(zi, wglu_ref[...], _TN, preferred_element_type=F32) + bglu_ref[...]
        ya = glu[:, :S5_WIDTH] * _sigmoid(glu[:, S5_WIDTH:])
        tok = pl.ds(i, nch, stride=c)
        yl_ref[0, tok, :] = ya[:, :128]
        yh_ref[0, tok, :] = ya[:, 128:]


def _s5(ul3, uh3, mats, wglu, bglu):
    b, seq, _ = ul3.shape
    nch = seq // S5_CHUNK
    tt_mat, s_mat, ot_mat, a_r, a_i, d_col = mats
    w = S5_CHUNK * S5_GROUP
    half = pl.BlockSpec((1, seq, 128), lambda i: (i, 0, 0))
    state_scr = pltpu.VMEM((S5_GROUPS * nch, 2 * S5_STATE), F32)
    return pl.pallas_call(
        _s5_kernel,
        grid=(b,),
        in_specs=[half, half,
                  _resident((S5_GROUPS, w, w)), _resident((S5_GROUPS, w, 4 * S5_STATE)),
                  _resident((S5_GROUPS, w, 4 * S5_STATE)),
                  _resident((S5_GROUPS, 128)), _resident((S5_GROUPS, 128)),
                  _resident((S5_GROUPS, w, nch)),
                  _resident((S5_WIDTH, 2 * S5_WIDTH)), _resident((1, 2 * S5_WIDTH))],
        out_specs=[half, half],
        out_shape=[jax.ShapeDtypeStruct((b, seq, 128), F32)] * 2,
        scratch_shapes=[pltpu.VMEM((S5_GROUPS, w, nch), F32)] + [state_scr] * 6
                       + [pltpu.VMEM((S5_CHUNK, w, nch), F32)],
        compiler_params=_params("parallel"),
        name="s5",
    )(ul3, uh3, tt_mat, s_mat, ot_mat, a_r, a_i, d_col, wglu, bglu)


def _gla_constants():
    r = np.arange(256)
    same = (r[:, None] // GLA_CHUNK) == (r[None, :] // GLA_CHUNK)
    prefix = same & (r[None, :] <= r[:, None])
    suffix = same & (r[None, :] >= r[:, None])
    avg = ((r[:, None] // GLA_DV) == (r[None, :] // GLA_DV)) / float(GLA_DV)
    as_bf = lambda m: jnp.asarray(m.astype(np.float32), BF16)
    return as_bf(prefix), as_bf(suffix), as_bf(same), as_bf(avg)


def _log_sigmoid(x):
    return jnp.minimum(x, 0.0) - jnp.log(1.0 + jnp.exp(-jnp.abs(x)))


def _split_mm(a, b):
    hi = a.astype(BF16)
    lo = (a - hi.astype(F32)).astype(BF16)
    return _mm(hi, b) + _mm(lo, b)


def _chunk_sums(mat, la):
    hi = la.astype(BF16)
    lo = (la - hi.astype(F32)).astype(BF16)
    out = []
    for i in range(la.shape[0] // 256):
        sl = slice(i * 256, (i + 1) * 256)
        out.append(_mm(mat, hi[sl]) + _mm(mat, lo[sl]))
    return jnp.concatenate(out, axis=0)


def _gla_kernel(g_ref, waf_ref, wab_ref, ba_ref, pre_ref, suf_ref, tot_ref, avg_ref, lng_ref, o_ref,
                q_s, kin_s, kst_s, v_s, totf_s, totb_s, o_s, kv_s, st_s):
    seq = g_ref.shape[1]
    nchunk = seq // GLA_CHUNK
    c = GLA_CHUNK
    q = g_ref[0, :, 0:128] * (GLA_DK ** -0.5)
    k = g_ref[0, :, 128:256]
    lfb = g_ref[0, :, 768:896].astype(BF16)
    la_f = _log_sigmoid(_mm(lfb, waf_ref[...]) + ba_ref[0:1, :]) * (1.0 / GLA_TAU)
    la_b = _log_sigmoid(_mm(lfb, wab_ref[...]) + ba_ref[1:2, :]) * (1.0 / GLA_TAU)

    bf = _chunk_sums(pre_ref[...], la_f)
    tf = _chunk_sums(tot_ref[...], la_f)
    q_s[:, 0:128] = (q * jnp.exp(bf)).astype(BF16)
    kin_s[:, 0:128] = (k * jnp.exp(-bf)).astype(BF16)
    kst_s[:, 0:128] = (k * jnp.exp(tf - bf)).astype(BF16)
    totf_s[...] = tf
    bs = _chunk_sums(suf_ref[...], la_b)
    tb = _chunk_sums(tot_ref[...], la_b)
    q_s[:, 128:256] = (q * jnp.exp(bs)).astype(BF16)
    kin_s[:, 128:256] = (k * jnp.exp(-bs)).astype(BF16)
    kst_s[:, 128:256] = (k * jnp.exp(tb - bs)).astype(BF16)
    totb_s[...] = tb
    v_s[...] = g_ref[0, :, 256:512].astype(BF16)

    r256 = lax.broadcasted_iota(jnp.int32, (4 * c, 256), 0)
    l256 = lax.broadcasted_iota(jnp.int32, (4 * c, 256), 1)
    head_k = (r256 // c) == ((l256 % 128) // GLA_DK)
    head_v = (r256 // c) == (l256 // GLA_DV)
    qi = lax.broadcasted_iota(jnp.int32, (c, 4 * c), 0)
    kj = lax.broadcasted_iota(jnp.int32, (c, 4 * c), 1) % c
    zero_bf = jnp.zeros((), BF16)

    def local(n, carry):
        r0 = pl.multiple_of(n * c, c)
        qq = q_s[pl.ds(r0, c), :]
        vv = v_s[pl.ds(r0, c), :]
        kexp = jnp.where(head_k, jnp.concatenate([kin_s[pl.ds(r0, c), :]] * 4, axis=0), zero_bf)
        vexp = jnp.where(head_v, jnp.concatenate([vv] * 4, axis=0), zero_bf)
        sc_f = lax.dot_general(qq[:, 0:128], kexp[:, 0:128], _NT, preferred_element_type=F32)
        sc_b = lax.dot_general(qq[:, 128:256], kexp[:, 128:256], _NT, preferred_element_type=F32)
        pm = jnp.where(kj <= qi, sc_f, sc_b).astype(BF16)
        o_s[pl.ds(r0, c), :] = _mm(pm, vexp)
        kv_t = lax.dot_general(vv, kst_s[pl.ds(r0, c), :], _TN, preferred_element_type=F32)
        kv_s[n] = jnp.where(head_k, kv_t, 0.0)
        return carry

    lax.fori_loop(0, nchunk, local, 0, unroll=4)

    def carry_states(n, carry):
        st_f, st_b = carry
        m = nchunk - 1 - n
        st_s[n, :, 0:128] = st_f.astype(BF16)
        st_s[m, :, 128:256] = st_b.astype(BF16)
        st_f = jnp.exp(totf_s[pl.ds(n * c, 1), :]) * st_f + kv_s[n, :, 0:128]
        st_b = jnp.exp(totb_s[pl.ds(m * c, 1), :]) * st_b + kv_s[m, :, 128:256]
        return st_f, st_b

    zero = jnp.zeros((GLA_WIDTH, GLA_QK), F32)
    lax.fori_loop(0, nchunk, carry_states, (zero, zero))

    def cross(n, carry):
        r0 = pl.multiple_of(n * c, c)
        o_s[pl.ds(r0, c), :] += lax.dot_general(q_s[pl.ds(r0, c), :], st_s[n], _NT, preferred_element_type=F32)
        return carry

    lax.fori_loop(0, nchunk, cross, 0, unroll=8)

    o = o_s[...]
    avg = avg_ref[...]
    mu = _split_mm(o, avg)
    dlt = o - mu
    var = _split_mm(dlt * dlt, avg)
    on = dlt * lax.rsqrt(var + LN_EPS)
    r = g_ref[0, :, 512:768]
    o_ref[0] = on * lng_ref[...] * (r * _sigmoid(r))


def _gla(g3, waf, wab, ba, consts, ln_g):
    b, seq, _ = g3.shape
    pre, suf, tot, avg = consts
    nchunk = seq // GLA_CHUNK
    both = pltpu.VMEM((seq, 2 * GLA_QK), BF16)
    return pl.pallas_call(
        _gla_kernel,
        grid=(b,),
        in_specs=[pl.BlockSpec((1, seq, G_COLS), lambda i: (i, 0, 0)),
                  _resident((128, 128)), _resident((128, 128)), _resident((2, 128)),
                  _resident((256, 256)), _resident((256, 256)), _resident((256, 256)),
                  _resident((256, 256)), _resident((1, 256))],
        out_specs=pl.BlockSpec((1, seq, GLA_WIDTH), lambda i: (i, 0, 0)),
        out_shape=jax.ShapeDtypeStruct((b, seq, GLA_WIDTH), F32),
        scratch_shapes=[both, both, both, pltpu.VMEM((seq, GLA_WIDTH), BF16),
                        pltpu.VMEM((seq, GLA_QK), F32), pltpu.VMEM((seq, GLA_QK), F32),
                        pltpu.VMEM((seq, GLA_WIDTH), F32),
                        pltpu.VMEM((nchunk, GLA_WIDTH, 2 * GLA_QK), F32),
                        pltpu.VMEM((nchunk, GLA_WIDTH, 2 * GLA_QK), BF16)],
        compiler_params=_params("parallel"),
        name="gla",
    )(g3, waf, wab, ba, pre, suf, tot, avg, ln_g)


def _rope_tables(seq):
    pos = jnp.arange(seq, dtype=F32)
    inv_freq = ROPE_THETA ** (-jnp.arange(0, ROT_DIM, 2, dtype=F32) / ROT_DIM)
    ang = pos[:, None] * inv_freq[None, :]
    cos, sin = jnp.cos(ang), jnp.sin(ang)
    half = ROT_DIM // 2
    pad = jnp.zeros((seq, SWA_HEAD_DIM - ROT_DIM), F32)
    c_head = jnp.concatenate([cos, cos, pad + 1.0], axis=1)
    s_head = jnp.concatenate([sin, sin, pad], axis=1)
    two = lambda t: jnp.concatenate([t, t], axis=1)
    rot = np.zeros((128, 128), np.float32)
    for base in (0, SWA_HEAD_DIM):
        for d in range(half):
            rot[base + d + half, base + d] = -1.0
            rot[base + d, base + d + half] = 1.0
    return two(c_head), two(s_head), jnp.asarray(rot, BF16)


def _swa_kernel(sink_ref, a_ref, cos_ref, sin_ref, rot_ref, o_ref,
                q_s, klo_s, khi_s, vd_s, s_s, p_s, bias_s, m_s, l_s):
    seq = a_ref.shape[1]
    blk = SWA_BLOCK
    hd = SWA_HEAD_DIM
    pairs = SWA_GROUP // 2
    cosv, sinv, rot = cos_ref[...], sin_ref[...], rot_ref[...]

    def rope(x):
        return x * cosv + _split_mm(x, rot) * sinv

    for slab in range(SWA_WIDTH // 128):
        q_s[slab] = (rope(a_ref[0, :, slab * 128:(slab + 1) * 128]) * (hd ** -0.5)).astype(BF16)
    kr = rope(a_ref[0, :, SWA_WIDTH:SWA_WIDTH + SWA_KV])
    vr = a_ref[0, :, SWA_WIDTH + SWA_KV:A_COLS]
    ksw = pltpu.roll(kr, hd, 1)
    vsw = pltpu.roll(vr, hd, 1)
    low = lax.broadcasted_iota(jnp.int32, (seq, 128), 1) < hd
    body = slice(blk, blk + seq)
    zpad = jnp.zeros((blk, 128), BF16)
    for ref in (klo_s, khi_s, vd_s):
        for kh in range(SWA_KV_HEADS):
            ref[kh, 0:blk, :] = zpad
            ref[kh, blk + seq:2 * blk + seq, :] = zpad
    klo_s[0, body, :] = jnp.where(low, kr, 0.0).astype(BF16)
    khi_s[0, body, :] = jnp.where(low, 0.0, ksw).astype(BF16)
    klo_s[1, body, :] = jnp.where(low, ksw, 0.0).astype(BF16)
    khi_s[1, body, :] = jnp.where(low, 0.0, kr).astype(BF16)
    vd_s[0, body, :] = jnp.where(low, vr, vsw).astype(BF16)
    vd_s[1, body, :] = jnp.where(low, vsw, vr).astype(BF16)

    qi = lax.broadcasted_iota(jnp.int32, (blk, 3 * blk), 0)
    kk = lax.broadcasted_iota(jnp.int32, (blk, 3 * blk), 1)
    band = jnp.abs(qi + blk - kk) <= SWA_WINDOW
    low_o = lax.broadcasted_iota(jnp.int32, (blk, 128), 1) < hd
    rows = SWA_GROUP * blk
    chunks = [(kh, rb) for kh in range(SWA_KV_HEADS) for rb in range(rows // SWA_ROWS)]

    def biased(kh, rb):
        boff = (rb * SWA_ROWS) % blk
        return s_s[kh, rb * SWA_ROWS:(rb + 1) * SWA_ROWS, :] + bias_s[boff:boff + SWA_ROWS, :]

    def body_fn(n, carry):
        r0 = pl.multiple_of(n * blk, blk)
        kpos = kk + (n - 1) * blk
        bias_s[...] = jnp.where(band & (kpos >= 0) & (kpos < seq), 0.0, NEG_BIG)
        for kh in range(SWA_KV_HEADS):
            kcat = jnp.concatenate([klo_s[kh, pl.ds(r0, 3 * blk), :], khi_s[kh, pl.ds(r0, 3 * blk), :]], axis=0)
            for pr in range(pairs):
                qq = q_s[kh * pairs + pr, pl.ds(r0, blk), :]
                s2 = lax.dot_general(qq, kcat, _NT, preferred_element_type=F32)
                s_s[kh, (2 * pr) * blk:(2 * pr + 1) * blk, :] = s2[:, :3 * blk]
                s_s[kh, (2 * pr + 1) * blk:(2 * pr + 2) * blk, :] = s2[:, 3 * blk:]
        for kh, rb in chunks:
            rs = slice(rb * SWA_ROWS, (rb + 1) * SWA_ROWS)
            sink = sink_ref[kh * SWA_GROUP + (rb * SWA_ROWS) // blk]
            m = jnp.maximum(jnp.max(biased(kh, rb), axis=1, keepdims=True), sink)
            m_s[kh, rs, :] = jnp.broadcast_to(m, (SWA_ROWS, 128))
        for kh, rb in chunks:
            rs = slice(rb * SWA_ROWS, (rb + 1) * SWA_ROWS)
            sink = sink_ref[kh * SWA_GROUP + (rb * SWA_ROWS) // blk]
            m = m_s[kh, rs, :]
            p = jnp.exp(biased(kh, rb) - jnp.concatenate([m] * 3, axis=1))
            l_s[kh, rs, :] = jnp.sum(p, axis=1, keepdims=True) + jnp.exp(sink - m)
            p_s[kh, rs, :] = p.astype(BF16)
        for kh in range(SWA_KV_HEADS):
            o2 = _mm(p_s[kh], vd_s[kh, pl.ds(r0, 3 * blk), :]) * (1.0 / l_s[kh])
            o_ref[0, pl.ds(r0, blk), kh * SWA_GROUP * hd:(kh + 1) * SWA_GROUP * hd] = jnp.concatenate(
                [jnp.where(low_o, o2[(2 * pr) * blk:(2 * pr + 1) * blk], o2[(2 * pr + 1) * blk:(2 * pr + 2) * blk])
                 for pr in range(pairs)], axis=1)
        return carry

    lax.fori_loop(0, seq // blk, body_fn, 0)


def _swa(a3, sink, tables):
    b, seq, _ = a3.shape
    cos_t, sin_t, rot = tables
    tab = pl.BlockSpec((seq, 128), lambda i, s: (0, 0))
    rot_spec = pl.BlockSpec((128, 128), lambda i, s: (0, 0))
    banded = pltpu.VMEM((SWA_KV_HEADS, seq + 2 * SWA_BLOCK, 128), BF16)
    stat = pltpu.VMEM((SWA_KV_HEADS, SWA_GROUP * SWA_BLOCK, 128), F32)
    grid_spec = pltpu.PrefetchScalarGridSpec(
        num_scalar_prefetch=1,
        grid=(b,),
        in_specs=[pl.BlockSpec((1, seq, A_COLS), lambda i, s: (i, 0, 0)), tab, tab, rot_spec],
        out_specs=pl.BlockSpec((1, seq, SWA_WIDTH), lambda i, s: (i, 0, 0)),
        scratch_shapes=[pltpu.VMEM((SWA_WIDTH // 128, seq, 128), BF16), banded, banded, banded,
                        pltpu.VMEM((SWA_KV_HEADS, SWA_GROUP * SWA_BLOCK, 3 * SWA_BLOCK), F32),
                        pltpu.VMEM((SWA_KV_HEADS, SWA_GROUP * SWA_BLOCK, 3 * SWA_BLOCK), BF16),
                        pltpu.VMEM((SWA_BLOCK, 3 * SWA_BLOCK), F32), stat, stat],
    )
    return pl.pallas_call(
        _swa_kernel,
        grid_spec=grid_spec,
        out_shape=jax.ShapeDtypeStruct((b, seq, SWA_WIDTH), F32),
        compiler_params=_params("parallel"),
        name="swa",
    )(sink, a3, cos_t, sin_t, rot)


def _layer_norm(x, g, b):
    mu = jnp.mean(x, axis=-1, keepdims=True)
    d = x - mu
    var = jnp.mean(d * d, axis=-1, keepdims=True)
    return d * lax.rsqrt(var + LN_EPS) * g + b


def _out_ffn_kernel(x_ref, yal_ref, yah_ref, yb_ref, yc_ref, woa_ref, wob_ref, woc_ref,
                    g1_ref, b1_ref, w1_ref, w2_ref, g2_ref, b2_ref, o_ref, acc_ref, *, alpha):
    ya = jnp.concatenate([yal_ref[...], yah_ref[...]], axis=1)
    mix = (_mm(ya.astype(BF16), woa_ref[...]) + _mm(yb_ref[...].astype(BF16), wob_ref[...])
           + _mm(yc_ref[...].astype(BF16), woc_ref[...]))
    x1 = _layer_norm(alpha * x_ref[...] + mix, g1_ref[...], b1_ref[...])
    x1b = x1.astype(BF16)
    for ci in range(D_FF // FF_CHUNK):
        sl = slice(ci * FF_CHUNK, (ci + 1) * FF_CHUNK)
        hid = jnp.maximum(_mm(x1b, w1_ref[:, sl]), 0.0)
        part = _mm((hid * hid).astype(BF16), w2_ref[sl, :])
        if ci == 0:
            acc_ref[...] = part
        else:
            acc_ref[...] += part
    o_ref[...] = _layer_norm(alpha * x1 + acc_ref[...], g2_ref[...], b2_ref[...])


def _out_ffn(x2d, yal2d, yah2d, yb2d, yc2d, woa, wob, woc, g1, b1, w1, w2, g2, b2, alpha):
    t = x2d.shape[0]
    tm = min(ROW_TILE, t)
    row = lambda n: pl.BlockSpec((tm, n), lambda i: (i, 0))
    once = lambda shape: pl.BlockSpec(shape, lambda i: (0, 0), pipeline_mode=pl.Buffered(1))
    return pl.pallas_call(
        functools.partial(_out_ffn_kernel, alpha=alpha),
        grid=(t // tm,),
        in_specs=[row(D_MODEL), row(128), row(128), row(GLA_WIDTH), row(SWA_WIDTH),
                  once((S5_WIDTH, D_MODEL)), once((GLA_WIDTH, D_MODEL)), once((SWA_WIDTH, D_MODEL)),
                  once((1, D_MODEL)), once((1, D_MODEL)),
                  once((D_MODEL, D_FF)), once((D_FF, D_MODEL)),
                  once((1, D_MODEL)), once((1, D_MODEL))],
        out_specs=row(D_MODEL),
        out_shape=jax.ShapeDtypeStruct((t, D_MODEL), F32),
        scratch_shapes=[pltpu.VMEM((tm, D_MODEL), F32)],
        compiler_params=_params("parallel"),
        name="out_ffn",
    )(x2d, yal2d, yah2d, yb2d, yc2d, woa, wob, woc, g1, b1, w1, w2, g2, b2)


def _pack_w_in(w_in):
    points = np.cumsum([S5_WIDTH, GLA_QK, GLA_QK, GLA_WIDTH, GLA_WIDTH, GLA_RANK, GLA_RANK,
                        SWA_WIDTH, SWA_KV]).tolist()
    s5, gq, gk, gv, gr, glf, glb, aq, ak, av = jnp.split(w_in, points, axis=-1)
    pad = jnp.zeros((w_in.shape[0], 128 - 2 * GLA_RANK), w_in.dtype)
    return jnp.concatenate([s5, gq, gk, gv, gr, glf, glb, pad, aq, ak, av], axis=-1).astype(BF16)


def _pad_gate_w(w_a):
    z = jnp.zeros((128, GLA_QK), F32)
    waf = z.at[0:GLA_RANK].set(w_a[0].astype(F32))
    wab = z.at[GLA_RANK:2 * GLA_RANK].set(w_a[1].astype(F32))
    return waf.astype(BF16), wab.astype(BF16)


def _layer(x2d, bsz, seq, alpha, w_in, a_re, a_im, log_step, b_re, b_im, c_re, c_im, d_skip,
           w_glu, b_glu, gla_w_a, gla_b_a, gla_ln_g, swa_sink, w_out, ln1_g, ln1_b, w_ff1, w_ff2,
           ln2_g, ln2_b, gla_consts, rope_tabs):
    t = bsz * seq
    row = lambda v: v.astype(F32)[None, :]
    ul, uh, g, a = _inproj(x2d, _pack_w_in(w_in))

    mats = _s5_matrices(a_re, a_im, log_step, b_re, b_im, c_re, c_im, d_skip, seq // S5_CHUNK)
    yal, yah = _s5(ul.reshape(bsz, seq, 128), uh.reshape(bsz, seq, 128), mats, w_glu.astype(BF16), row(b_glu))

    waf, wab = _pad_gate_w(gla_w_a)
    yb = _gla(g.reshape(bsz, seq, G_COLS), waf, wab, gla_b_a.astype(F32), gla_consts, row(gla_ln_g))
    yc = _swa(a.reshape(bsz, seq, A_COLS), swa_sink.astype(F32), rope_tabs)

    wo = w_out.astype(BF16)
    return _out_ffn(x2d, yal.reshape(t, 128), yah.reshape(t, 128), yb.reshape(t, GLA_WIDTH), yc.reshape(t, SWA_WIDTH),
                    wo[0:S5_WIDTH], wo[S5_WIDTH:S5_WIDTH + GLA_WIDTH], wo[S5_WIDTH + GLA_WIDTH:],
                    row(ln1_g), row(ln1_b), w_ff1.astype(BF16), w_ff2.astype(BF16),
                    row(ln2_g), row(ln2_b), alpha)


def kernel(x, w_in, s5_a_re, s5_a_im, s5_log_step, s5_b_re, s5_b_im, s5_c_re, s5_c_im, s5_d, s5_w_glu, s5_b_glu, gla_w_a, gla_b_a, gla_ln_g, swa_sink, w_out, ln1_g, ln1_b, w_ff1, w_ff2, ln2_g, ln2_b):
    bsz, seq, _ = x.shape
    depth = w_in.shape[0]
    alpha = (2 * depth) ** 0.25
    gla_consts = _gla_constants()
    rope_tabs = _rope_tables(seq)
    h = x.reshape(bsz * seq, D_MODEL)
    for l in range(depth):
        h = _layer(h, bsz, seq, alpha, w_in[l], s5_a_re[l], s5_a_im[l], s5_log_step[l],
                   s5_b_re[l], s5_b_im[l], s5_c_re[l], s5_c_im[l], s5_d[l], s5_w_glu[l], s5_b_glu[l],
                   gla_w_a[l], gla_b_a[l], gla_ln_g[l], swa_sink[l], w_out[l], ln1_g[l], ln1_b[l],
                   w_ff1[l], w_ff2[l], ln2_g[l], ln2_b[l], gla_consts, rope_tabs)
    return h.reshape(bsz, seq, D_MODEL)
```

```python
import functools
import math

import jax
import jax.numpy as jnp
import numpy as np
from jax import lax
from jax.experimental import pallas as pl
from jax.experimental.pallas import tpu as pltpu

F32 = jnp.float32
BF16 = jnp.bfloat16

D_MODEL = 1024
S5_WIDTH = 256
S5_GROUP = 16
S5_GROUPS = 16
S5_STATE = 64
S5_CHUNK = 16
GLA_HEADS = 4
GLA_DK = 32
GLA_DV = 64
GLA_QK = 128
GLA_WIDTH = 256
GLA_RANK = 16
GLA_TAU = 16.0
GLA_CHUNK = 64
SWA_WIDTH = 512
SWA_HEAD_DIM = 64
SWA_HEADS = 8
SWA_KV_HEADS = 2
SWA_GROUP = SWA_HEADS // SWA_KV_HEADS
SWA_KV = 128
SWA_WINDOW = 128
SWA_BLOCK = 128
ROT_DIM = 16
ROPE_THETA = 500000.0
D_FF = 4096
LN_EPS = 1e-5
NEG_BIG = -1e30

U_COLS = S5_WIDTH
G_COLS = GLA_QK + GLA_QK + GLA_WIDTH + GLA_WIDTH + 128
A_COLS = SWA_WIDTH + SWA_KV + SWA_KV
IN_COLS = U_COLS + G_COLS + A_COLS

VMEM_LIMIT = 56 * 1024 * 1024
ROW_TILE = 512
FF_CHUNK = 512
FFN_TILE = 1024
FFN_SUB = 512
SWA_ROWS = 64

_NT = (((1,), (1,)), ((), ()))
_TN = (((0,), (0,)), ((), ()))


def _mm(a, b):
    return jnp.dot(a, b, preferred_element_type=F32)


def _params(*sem):
    return pltpu.CompilerParams(dimension_semantics=sem, vmem_limit_bytes=VMEM_LIMIT)


def _resident(shape):
    nd = len(shape)
    return pl.BlockSpec(shape, lambda *_: (0,) * nd)


def _inproj_kernel(x_ref, w_ref, ul_ref, uh_ref, g_ref, a_ref):
    x = x_ref[...].astype(BF16)
    u = _mm(x, w_ref[:, 0:U_COLS])
    ul_ref[...] = u[:, :128]
    uh_ref[...] = u[:, 128:]
    g_ref[...] = _mm(x, w_ref[:, U_COLS:U_COLS + G_COLS])
    a_ref[...] = _mm(x, w_ref[:, U_COLS + G_COLS:IN_COLS])


def _inproj(x2d, w_packed):
    t = x2d.shape[0]
    tm = min(ROW_TILE, t)
    row = lambda n: pl.BlockSpec((tm, n), lambda i: (i, 0))
    return pl.pallas_call(
        _inproj_kernel,
        grid=(t // tm,),
        in_specs=[row(D_MODEL), _resident((D_MODEL, IN_COLS))],
        out_specs=[row(128), row(128), row(G_COLS), row(A_COLS)],
        out_shape=[jax.ShapeDtypeStruct((t, 128), F32), jax.ShapeDtypeStruct((t, 128), F32),
                   jax.ShapeDtypeStruct((t, G_COLS), F32), jax.ShapeDtypeStruct((t, A_COLS), F32)],
        compiler_params=_params("parallel"),
        name="inproj",
    )(x2d, w_packed)


def _s5_matrices(a_re, a_im, log_step, b_re, b_im, c_re, c_im, d_skip, nch):
    c, ng, w = S5_CHUNK, S5_GROUPS, S5_CHUNK * S5_GROUP
    ar, ai = a_re.astype(F32), a_im.astype(F32)
    step = jnp.exp(log_step.astype(F32))
    lr, li = ar * step, ai * step

    def lam_pow(z, k):
        kk = jnp.asarray(k, F32)[None, :, None, None]
        mag = jnp.exp(kk * lr[z][:, None, None, :])
        ang = kk * li[z][:, None, None, :]
        return mag * jnp.cos(ang), mag * jnp.sin(ang)

    def cmul(xr, xi, yr, yi):
        return xr * yr - xi * yi, xr * yi + xi * yr

    l1r, l1i = jnp.exp(lr) * jnp.cos(li), jnp.exp(lr) * jnp.sin(li)
    nr, ni = l1r - 1.0, l1i
    den = ar * ar + ai * ai
    qr, qi = (nr * ar + ni * ai) / den, (ni * ar - nr * ai) / den
    bt_r = jnp.swapaxes(b_re.astype(F32), -1, -2)
    bt_i = jnp.swapaxes(b_im.astype(F32), -1, -2)
    bbr, bbi = cmul(qr[:, :, None, :], qi[:, :, None, :], bt_r, bt_i)
    cr, ci = c_re.astype(F32), c_im.astype(F32)

    idx = np.arange(c)
    flat = lambda t: t.reshape(ng, w, S5_STATE)

    def with_c(z, k):
        pr, pi = lam_pow(z, k)
        xr, xi = cmul(pr, pi, cr[z][:, None], ci[z][:, None])
        return flat(xr), flat(xi)

    def with_b(z, k):
        pr, pi = lam_pow(z, k)
        yr, yi = cmul(pr, pi, bbr[z][:, None], bbi[z][:, None])
        return flat(yr), flat(yi)

    def re_prod(x, y):
        return jnp.einsum('gmk,gnk->gmn', jnp.concatenate([x[0], -x[1]], axis=-1),
                          jnp.concatenate([y[0], y[1]], axis=-1), precision=lax.Precision.HIGHEST)

    ri = np.arange(w)[:, None] // S5_GROUP
    cj = np.arange(w)[None, :] // S5_GROUP
    tt_mat = (jnp.where(ri >= cj, re_prod(with_c(0, idx), with_b(0, -idx)), 0.0)
              + jnp.where(cj >= ri, re_prod(with_c(1, -idx), with_b(1, idx)), 0.0))

    sf, sb = with_b(0, c - 1 - idx), with_b(1, idx)
    s_mat = jnp.concatenate([sf[0], sb[0], sf[1], sb[1]], axis=-1)
    of, ob = with_c(0, idx + 1), with_c(1, c - idx)
    ot_mat = jnp.concatenate([of[0], ob[0], -of[1], -ob[1]], axis=-1)

    pcr = [lam_pow(z, [c]) for z in range(2)]
    a_r = jnp.concatenate([pcr[0][0][:, 0, 0, :], pcr[1][0][:, 0, 0, :]], axis=-1)
    a_i = jnp.concatenate([pcr[0][1][:, 0, 0, :], pcr[1][1][:, 0, 0, :]], axis=-1)
    d_col = jnp.broadcast_to(jnp.tile(d_skip.astype(F32), (1, c))[:, :, None], (ng, w, nch))
    return tt_mat.astype(BF16), s_mat.astype(BF16), ot_mat.astype(BF16), a_r, a_i, d_col


def _gelu_tanh(x):
    return 0.5 * x * (1.0 + jnp.tanh(math.sqrt(2.0 / math.pi) * (x + 0.044715 * (x * x * x))))


def _sigmoid(x):
    return 1.0 / (1.0 + jnp.exp(-x))


def _s5_kernel(ul_ref, uh_ref, tt_ref, s_ref, ot_ref, ar_ref, ai_ref, dcol_ref, wglu_ref, bglu_ref,
               yl_ref, yh_ref, d_scr, cr_scr, ci_scr, efr_scr, efi_scr, ebr_scr, ebi_scr, zt_scr):
    c, ng = S5_CHUNK, S5_GROUPS
    nch = ul_ref.shape[1] // c
    for j in range(c):
        tok = pl.ds(j, nch, stride=c)
        uj = jnp.concatenate([ul_ref[0, tok, :], uh_ref[0, tok, :]], axis=1)
        d_scr[:, j * S5_GROUP:(j + 1) * S5_GROUP, :] = uj.T.reshape(ng, S5_GROUP, nch)
    for g in range(ng):
        dg = d_scr[g].astype(BF16)
        contrib = lax.dot_general(dg, s_ref[g], _TN, preferred_element_type=F32)
        grp = pl.ds(g, nch, stride=ng)
        cr_scr[grp, :] = contrib[:, :128]
        ci_scr[grp, :] = contrib[:, 128:]

    ar, ai = ar_ref[...], ai_ref[...]
    fwd = lax.broadcasted_iota(jnp.int32, (ng, 128), 1) < S5_STATE

    def step(k, carry):
        sre, sim = carry
        up = pl.ds(pl.multiple_of(k * ng, ng), ng)
        dn = pl.ds(pl.multiple_of((nch - 1 - k) * ng, ng), ng)
        efr_scr[up, :] = sre
        efi_scr[up, :] = sim
        ebr_scr[dn, :] = sre
        ebi_scr[dn, :] = sim
        cre = jnp.where(fwd, cr_scr[up, :], cr_scr[dn, :])
        cim = jnp.where(fwd, ci_scr[up, :], ci_scr[dn, :])
        return ar * sre - ai * sim + cre, ar * sim + ai * sre + cim

    zero = jnp.zeros((ng, 128), F32)
    lax.fori_loop(0, nch, step, (zero, zero), unroll=8)

    fwd_all = lax.broadcasted_iota(jnp.int32, (nch, 128), 1) < S5_STATE
    for g in range(ng):
        dg32 = d_scr[g]
        grp = pl.ds(g, nch, stride=ng)
        enter = jnp.concatenate([jnp.where(fwd_all, efr_scr[grp, :], ebr_scr[grp, :]),
                                 jnp.where(fwd_all, efi_scr[grp, :], ebi_scr[grp, :])],
                                axis=1).astype(BF16)
        yt = (_mm(tt_ref[g], dg32.astype(BF16))
              + lax.dot_general(ot_ref[g], enter, _NT, preferred_element_type=F32)
              + dcol_ref[g] * dg32)
        zt_scr[:, g * S5_GROUP:(g + 1) * S5_GROUP, :] = _gelu_tanh(yt).reshape(c, S5_GROUP, nch)
    for i in range(c):
        zi = zt_scr[i].astype(BF16)
        glu = lax.dot_general(zi, wglu_ref[...], _TN, preferred_element_type=F32) + bglu_ref[...]
        ya = glu[:, :S5_WIDTH] * _sigmoid(glu[:, S5_WIDTH:])
        tok = pl.ds(i, nch, stride=c)
        yl_ref[0, tok, :] = ya[:, :128]
        yh_ref[0, tok, :] = ya[:, 128:]


def _s5(ul3, uh3, mats, wglu, bglu):
    b, seq, _ = ul3.shape
    nch = seq // S5_CHUNK
    tt_mat, s_mat, ot_mat, a_r, a_i, d_col = mats
    w = S5_CHUNK * S5_GROUP
    half = pl.BlockSpec((1, seq, 128), lambda i: (i, 0, 0))
    state_scr = pltpu.VMEM((S5_GROUPS * nch, 2 * S5_STATE), F32)
    return pl.pallas_call(
        _s5_kernel,
        grid=(b,),
        in_specs=[half, half,
                  _resident((S5_GROUPS, w, w)), _resident((S5_GROUPS, w, 4 * S5_STATE)),
                  _resident((S5_GROUPS, w, 4 * S5_STATE)),
                  _resident((S5_GROUPS, 128)), _resident((S5_GROUPS, 128)),
                  _resident((S5_GROUPS, w, nch)),
                  _resident((S5_WIDTH, 2 * S5_WIDTH)), _resident((1, 2 * S5_WIDTH))],
        out_specs=[half, half],
        out_shape=[jax.ShapeDtypeStruct((b, seq, 128), F32)] * 2,
        scratch_shapes=[pltpu.VMEM((S5_GROUPS, w, nch), F32)] + [state_scr] * 6
                       + [pltpu.VMEM((S5_CHUNK, w, nch), F32)],
        compiler_params=_params("parallel"),
        name="s5",
    )(ul3, uh3, tt_mat, s_mat, ot_mat, a_r, a_i, d_col, wglu, bglu)


def _gla_constants():
    r = np.arange(256)
    same = (r[:, None] // GLA_CHUNK) == (r[None, :] // GLA_CHUNK)
    prefix = same & (r[None, :] <= r[:, None])
    suffix = same & (r[None, :] >= r[:, None])
    avg = ((r[:, None] // GLA_DV) == (r[None, :] // GLA_DV)) / float(GLA_DV)
    as_bf = lambda m: jnp.asarray(m.astype(np.float32), BF16)
    return as_bf(prefix), as_bf(suffix), as_bf(same), as_bf(avg)


def _log_sigmoid(x):
    return jnp.minimum(x, 0.0) - jnp.log(1.0 + jnp.exp(-jnp.abs(x)))


def _split_mm(a, b):
    hi = a.astype(BF16)
    lo = (a - hi.astype(F32)).astype(BF16)
    return _mm(hi, b) + _mm(lo, b)


def _chunk_sums(mat, la):
    hi = la.astype(BF16)
    lo = (la - hi.astype(F32)).astype(BF16)
    out = []
    for i in range(la.shape[0] // 256):
        sl = slice(i * 256, (i + 1) * 256)
        out.append(_mm(mat, hi[sl]) + _mm(mat, lo[sl]))
    return jnp.concatenate(out, axis=0)


def _gla_kernel(g_ref, waf_ref, wab_ref, ba_ref, pre_ref, suf_ref, tot_ref, avg_ref, lng_ref, o_ref,
                q_s, kin_s, kst_s, v_s, totf_s, totb_s, o_s, kv_s, st_s):
    seq = g_ref.shape[1]
    nchunk = seq // GLA_CHUNK
    c = GLA_CHUNK
    q = g_ref[0, :, 0:128] * (GLA_DK ** -0.5)
    k = g_ref[0, :, 128:256]
    lfb = g_ref[0, :, 768:896].astype(BF16)
    la_f = _log_sigmoid(_mm(lfb, waf_ref[...]) + ba_ref[0:1, :]) * (1.0 / GLA_TAU)
    la_b = _log_sigmoid(_mm(lfb, wab_ref[...]) + ba_ref[1:2, :]) * (1.0 / GLA_TAU)

    bf = _chunk_sums(pre_ref[...], la_f)
    tf = _chunk_sums(tot_ref[...], la_f)
    q_s[:, 0:128] = (q * jnp.exp(bf)).astype(BF16)
    kin_s[:, 0:128] = (k * jnp.exp(-bf)).astype(BF16)
    kst_s[:, 0:128] = (k * jnp.exp(tf - bf)).astype(BF16)
    totf_s[...] = tf
    bs = _chunk_sums(suf_ref[...], la_b)
    tb = _chunk_sums(tot_ref[...], la_b)
    q_s[:, 128:256] = (q * jnp.exp(bs)).astype(BF16)
    kin_s[:, 128:256] = (k * jnp.exp(-bs)).astype(BF16)
    kst_s[:, 128:256] = (k * jnp.exp(tb - bs)).astype(BF16)
    totb_s[...] = tb
    v_s[...] = g_ref[0, :, 256:512].astype(BF16)

    r256 = lax.broadcasted_iota(jnp.int32, (4 * c, 256), 0)
    l256 = lax.broadcasted_iota(jnp.int32, (4 * c, 256), 1)
    head_k = (r256 // c) == ((l256 % 128) // GLA_DK)
    head_v = (r256 // c) == (l256 // GLA_DV)
    qi = lax.broadcasted_iota(jnp.int32, (c, 4 * c), 0)
    kj = lax.broadcasted_iota(jnp.int32, (c, 4 * c), 1) % c
    zero_bf = jnp.zeros((), BF16)

    def local(n, carry):
        r0 = pl.multiple_of(n * c, c)
        qq = q_s[pl.ds(r0, c), :]
        vv = v_s[pl.ds(r0, c), :]
        kexp = jnp.where(head_k, jnp.concatenate([kin_s[pl.ds(r0, c), :]] * 4, axis=0), zero_bf)
        vexp = jnp.where(head_v, jnp.concatenate([vv] * 4, axis=0), zero_bf)
        sc_f = lax.dot_general(qq[:, 0:128], kexp[:, 0:128], _NT, preferred_element_type=F32)
        sc_b = lax.dot_general(qq[:, 128:256], kexp[:, 128:256], _NT, preferred_element_type=F32)
        pm = jnp.where(kj <= qi, sc_f, sc_b).astype(BF16)
        o_s[pl.ds(r0, c), :] = _mm(pm, vexp)
        kv_t = lax.dot_general(vv, kst_s[pl.ds(r0, c), :], _TN, preferred_element_type=F32)
        kv_s[n] = jnp.where(head_k, kv_t, 0.0)
        return carry

    lax.fori_loop(0, nchunk, local, 0, unroll=4)

    def carry_states(n, carry):
        st_f, st_b = carry
        m = nchunk - 1 - n
        st_s[n, :, 0:128] = st_f.astype(BF16)
        st_s[m, :, 128:256] = st_b.astype(BF16)
        st_f = jnp.exp(totf_s[pl.ds(n * c, 1), :]) * st_f + kv_s[n, :, 0:128]
        st_b = jnp.exp(totb_s[pl.ds(m * c, 1), :]) * st_b + kv_s[m, :, 128:256]
        return st_f, st_b

    zero = jnp.zeros((GLA_WIDTH, GLA_QK), F32)
    lax.fori_loop(0, nchunk, carry_states, (zero, zero))

    def cross(n, carry):
        r0 = pl.multiple_of(n * c, c)
        o_s[pl.ds(r0, c), :] += lax.dot_general(q_s[pl.ds(r0, c), :], st_s[n], _NT, preferred_element_type=F32)
        return carry

    lax.fori_loop(0, nchunk, cross, 0, unroll=8)

    o = o_s[...]
    avg = avg_ref[...]
    mu = _split_mm(o, avg)
    dlt = o - mu
    var = _split_mm(dlt * dlt, avg)
    on = dlt * lax.rsqrt(var + LN_EPS)
    r = g_ref[0, :, 512:768]
    o_ref[0] = on * lng_ref[...] * (r * _sigmoid(r))


def _gla(g3, waf, wab, ba, consts, ln_g):
    b, seq, _ = g3.shape
    pre, suf, tot, avg = consts
    nchunk = seq // GLA_CHUNK
    both = pltpu.VMEM((seq, 2 * GLA_QK), BF16)
    return pl.pallas_call(
        _gla_kernel,
        grid=(b,),
        in_specs=[pl.BlockSpec((1, seq, G_COLS), lambda i: (i, 0, 0)),
                  _resident((128, 128)), _resident((128, 128)), _resident((2, 128)),
                  _resident((256, 256)), _resident((256, 256)), _resident((256, 256)),
                  _resident((256, 256)), _resident((1, 256))],
        out_specs=pl.BlockSpec((1, seq, GLA_WIDTH), lambda i: (i, 0, 0)),
        out_shape=jax.ShapeDtypeStruct((b, seq, GLA_WIDTH), F32),
        scratch_shapes=[both, both, both, pltpu.VMEM((seq, GLA_WIDTH), BF16),
                        pltpu.VMEM((seq, GLA_QK), F32), pltpu.VMEM((seq, GLA_QK), F32),
                        pltpu.VMEM((seq, GLA_WIDTH), F32),
                        pltpu.VMEM((nchunk, GLA_WIDTH, 2 * GLA_QK), F32),
                        pltpu.VMEM((nchunk, GLA_WIDTH, 2 * GLA_QK), BF16)],
        compiler_params=_params("parallel"),
        name="gla",
    )(g3, waf, wab, ba, pre, suf, tot, avg, ln_g)


def _rope_tables(seq):
    pos = jnp.arange(seq, dtype=F32)
    inv_freq = ROPE_THETA ** (-jnp.arange(0, ROT_DIM, 2, dtype=F32) / ROT_DIM)
    ang = pos[:, None] * inv_freq[None, :]
    cos, sin = jnp.cos(ang), jnp.sin(ang)
    half = ROT_DIM // 2
    pad = jnp.zeros((seq, SWA_HEAD_DIM - ROT_DIM), F32)
    c_head = jnp.concatenate([cos, cos, pad + 1.0], axis=1)
    s_head = jnp.concatenate([sin, sin, pad], axis=1)
    two = lambda t: jnp.concatenate([t, t], axis=1)
    rot = np.zeros((128, 128), np.float32)
    for base in (0, SWA_HEAD_DIM):
        for d in range(half):
            rot[base + d + half, base + d] = -1.0
            rot[base + d, base + d + half] = 1.0
    return two(c_head), two(s_head), jnp.asarray(rot, BF16)


def _swa_kernel(sink_ref, a_ref, cos_ref, sin_ref, rot_ref, o_ref,
                q_s, klo_s, khi_s, vd_s, s_s, p_s, bias_s, m_s, l_s):
    seq = a_ref.shape[1]
    blk = SWA_BLOCK
    hd = SWA_HEAD_DIM
    pairs = SWA_GROUP // 2
    cosv, sinv, rot = cos_ref[...], sin_ref[...], rot_ref[...]

    def rope(x):
        return x * cosv + _split_mm(x, rot) * sinv

    for slab in range(SWA_WIDTH // 128):
        q_s[slab] = (rope(a_ref[0, :, slab * 128:(slab + 1) * 128]) * (hd ** -0.5)).astype(BF16)
    kr = rope(a_ref[0, :, SWA_WIDTH:SWA_WIDTH + SWA_KV])
    vr = a_ref[0, :, SWA_WIDTH + SWA_KV:A_COLS]
    ksw = pltpu.roll(kr, hd, 1)
    vsw = pltpu.roll(vr, hd, 1)
    low = lax.broadcasted_iota(jnp.int32, (seq, 128), 1) < hd
    body = slice(blk, blk + seq)
    zpad = jnp.zeros((blk, 128), BF16)
    for ref in (klo_s, khi_s, vd_s):
        for kh in range(SWA_KV_HEADS):
            ref[kh, 0:blk, :] = zpad
            ref[kh, blk + seq:2 * blk + seq, :] = zpad
    klo_s[0, body, :] = jnp.where(low, kr, 0.0).astype(BF16)
    khi_s[0, body, :] = jnp.where(low, 0.0, ksw).astype(BF16)
    klo_s[1, body, :] = jnp.where(low, ksw, 0.0).astype(BF16)
    khi_s[1, body, :] = jnp.where(low, 0.0, kr).astype(BF16)
    vd_s[0, body, :] = jnp.where(low, vr, vsw).astype(BF16)
    vd_s[1, body, :] = jnp.where(low, vsw, vr).astype(BF16)

    qi = lax.broadcasted_iota(jnp.int32, (blk, 3 * blk), 0)
    kk = lax.broadcasted_iota(jnp.int32, (blk, 3 * blk), 1)
    band = jnp.abs(qi + blk - kk) <= SWA_WINDOW
    low_o = lax.broadcasted_iota(jnp.int32, (blk, 128), 1) < hd
    rows = SWA_GROUP * blk
    chunks = [(kh, rb) for kh in range(SWA_KV_HEADS) for rb in range(rows // SWA_ROWS)]

    def biased(kh, rb):
        boff = (rb * SWA_ROWS) % blk
        return s_s[kh, rb * SWA_ROWS:(rb + 1) * SWA_ROWS, :] + bias_s[boff:boff + SWA_ROWS, :]

    def body_fn(n, carry):
        r0 = pl.multiple_of(n * blk, blk)
        kpos = kk + (n - 1) * blk
        bias_s[...] = jnp.where(band & (kpos >= 0) & (kpos < seq), 0.0, NEG_BIG)
        for kh in range(SWA_KV_HEADS):
            kcat = jnp.concatenate([klo_s[kh, pl.ds(r0, 3 * blk), :], khi_s[kh, pl.ds(r0, 3 * blk), :]], axis=0)
            for pr in range(pairs):
                qq = q_s[kh * pairs + pr, pl.ds(r0, blk), :]
                s2 = lax.dot_general(qq, kcat, _NT, preferred_element_type=F32)
                s_s[kh, (2 * pr) * blk:(2 * pr + 1) * blk, :] = s2[:, :3 * blk]
                s_s[kh, (2 * pr + 1) * blk:(2 * pr + 2) * blk, :] = s2[:, 3 * blk:]
        for kh, rb in chunks:
            rs = slice(rb * SWA_ROWS, (rb + 1) * SWA_ROWS)
            sink = sink_ref[kh * SWA_GROUP + (rb * SWA_ROWS) // blk]
            m = jnp.maximum(jnp.max(biased(kh, rb), axis=1, keepdims=True), sink)
            m_s[kh, rs, :] = jnp.broadcast_to(m, (SWA_ROWS, 128))
        for kh, rb in chunks:
            rs = slice(rb * SWA_ROWS, (rb + 1) * SWA_ROWS)
            sink = sink_ref[kh * SWA_GROUP + (rb * SWA_ROWS) // blk]
            m = m_s[kh, rs, :]
            p = jnp.exp(biased(kh, rb) - jnp.concatenate([m] * 3, axis=1))
            l_s[kh, rs, :] = jnp.sum(p, axis=1, keepdims=True) + jnp.exp(sink - m)
            p_s[kh, rs, :] = p.astype(BF16)
        for kh in range(SWA_KV_HEADS):
            o2 = _mm(p_s[kh], vd_s[kh, pl.ds(r0, 3 * blk), :]) * (1.0 / l_s[kh])
            o_ref[0, pl.ds(r0, blk), kh * SWA_GROUP * hd:(kh + 1) * SWA_GROUP * hd] = jnp.concatenate(
                [jnp.where(low_o, o2[(2 * pr) * blk:(2 * pr + 1) * blk], o2[(2 * pr + 1) * blk:(2 * pr + 2) * blk])
                 for pr in range(pairs)], axis=1)
        return carry

    lax.fori_loop(0, seq // blk, body_fn, 0)


def _swa(a3, sink, tables):
    b, seq, _ = a3.shape
    cos_t, sin_t, rot = tables
    tab = pl.BlockSpec((seq, 128), lambda i, s: (0, 0))
    rot_spec = pl.BlockSpec((128, 128), lambda i, s: (0, 0))
    banded = pltpu.VMEM((SWA_KV_HEADS, seq + 2 * SWA_BLOCK, 128), BF16)
    stat = pltpu.VMEM((SWA_KV_HEADS, SWA_GROUP * SWA_BLOCK, 128), F32)
    grid_spec = pltpu.PrefetchScalarGridSpec(
        num_scalar_prefetch=1,
        grid=(b,),
        in_specs=[pl.BlockSpec((1, seq, A_COLS), lambda i, s: (i, 0, 0)), tab, tab, rot_spec],
        out_specs=pl.BlockSpec((1, seq, SWA_WIDTH), lambda i, s: (i, 0, 0)),
        scratch_shapes=[pltpu.VMEM((SWA_WIDTH // 128, seq, 128), BF16), banded, banded, banded,
                        pltpu.VMEM((SWA_KV_HEADS, SWA_GROUP * SWA_BLOCK, 3 * SWA_BLOCK), F32),
                        pltpu.VMEM((SWA_KV_HEADS, SWA_GROUP * SWA_BLOCK, 3 * SWA_BLOCK), BF16),
                        pltpu.VMEM((SWA_BLOCK, 3 * SWA_BLOCK), F32), stat, stat],
    )
    return pl.pallas_call(
        _swa_kernel,
        grid_spec=grid_spec,
        out_shape=jax.ShapeDtypeStruct((b, seq, SWA_WIDTH), F32),
        compiler_params=_params("parallel"),
        name="swa",
    )(sink, a3, cos_t, sin_t, rot)


def _layer_norm(x, g, b):
    mu = jnp.mean(x, axis=-1, keepdims=True)
    d = x - mu
    var = jnp.mean(d * d, axis=-1, keepdims=True)
    return d * lax.rsqrt(var + LN_EPS) * g + b


def _out_ffn_kernel(x_ref, yal_ref, yah_ref, yb_ref, yc_ref, woa_ref, wob_ref, woc_ref,
                    g1_ref, b1_ref, w1_ref, w2_ref, g2_ref, b2_ref, o_ref, acc_ref, x1b_ref, *, alpha):
    subs = [slice(r, r + FFN_SUB) for r in range(0, x_ref.shape[0], FFN_SUB)]
    for rows in subs:
        ya = jnp.concatenate([yal_ref[rows, :], yah_ref[rows, :]], axis=1)
        mix = (_mm(ya.astype(BF16), woa_ref[...]) + _mm(yb_ref[rows, :].astype(BF16), wob_ref[...])
               + _mm(yc_ref[rows, :].astype(BF16), woc_ref[...]))
        x1 = _layer_norm(alpha * x_ref[rows, :] + mix, g1_ref[...], b1_ref[...])
        o_ref[rows, :] = x1
        x1b_ref[rows, :] = x1.astype(BF16)
    for rows in subs:
        for ci in range(D_FF // FF_CHUNK):
            sl = slice(ci * FF_CHUNK, (ci + 1) * FF_CHUNK)
            hid = jnp.maximum(_mm(x1b_ref[rows, :], w1_ref[:, sl]), 0.0)
            part = _mm((hid * hid).astype(BF16), w2_ref[sl, :])
            if ci == 0:
                acc_ref[rows, :] = part
            else:
                acc_ref[rows, :] += part
    for rows in subs:
        o_ref[rows, :] = _layer_norm(alpha * o_ref[rows, :] + acc_ref[rows, :], g2_ref[...], b2_ref[...])


def _out_ffn(x2d, yal2d, yah2d, yb2d, yc2d, woa, wob, woc, g1, b1, w1, w2, g2, b2, alpha):
    t = x2d.shape[0]
    tm = min(FFN_TILE, t)
    row = lambda n: pl.BlockSpec((tm, n), lambda i: (i, 0))
    once = lambda shape: pl.BlockSpec(shape, lambda i: (0, 0), pipeline_mode=pl.Buffered(1))
    return pl.pallas_call(
        functools.partial(_out_ffn_kernel, alpha=alpha),
        grid=(t // tm,),
        in_specs=[row(D_MODEL), row(128), row(128), row(GLA_WIDTH), row(SWA_WIDTH),
                  once((S5_WIDTH, D_MODEL)), once((GLA_WIDTH, D_MODEL)), once((SWA_WIDTH, D_MODEL)),
                  once((1, D_MODEL)), once((1, D_MODEL)),
                  once((D_MODEL, D_FF)), once((D_FF, D_MODEL)),
                  once((1, D_MODEL)), once((1, D_MODEL))],
        out_specs=row(D_MODEL),
        out_shape=jax.ShapeDtypeStruct((t, D_MODEL), F32),
        scratch_shapes=[pltpu.VMEM((tm, D_MODEL), F32), pltpu.VMEM((tm, D_MODEL), BF16)],
        compiler_params=_params("parallel"),
        name="out_ffn",
    )(x2d, yal2d, yah2d, yb2d, yc2d, woa, wob, woc, g1, b1, w1, w2, g2, b2)


def _pack_w_in(w_in):
    points = np.cumsum([S5_WIDTH, GLA_QK, GLA_QK, GLA_WIDTH, GLA_WIDTH, GLA_RANK, GLA_RANK,
                        SWA_WIDTH, SWA_KV]).tolist()
    s5, gq, gk, gv, gr, glf, glb, aq, ak, av = jnp.split(w_in, points, axis=-1)
    pad = jnp.zeros((w_in.shape[0], 128 - 2 * GLA_RANK), w_in.dtype)
    return jnp.concatenate([s5, gq, gk, gv, gr, glf, glb, pad, aq, ak, av], axis=-1).astype(BF16)


def _pad_gate_w(w_a):
    z = jnp.zeros((128, GLA_QK), F32)
    waf = z.at[0:GLA_RANK].set(w_a[0].astype(F32))
    wab = z.at[GLA_RANK:2 * GLA_RANK].set(w_a[1].astype(F32))
    return waf.astype(BF16), wab.astype(BF16)


def _layer(x2d, bsz, seq, alpha, w_in, a_re, a_im, log_step, b_re, b_im, c_re, c_im, d_skip,
           w_glu, b_glu, gla_w_a, gla_b_a, gla_ln_g, swa_sink, w_out, ln1_g, ln1_b, w_ff1, w_ff2,
           ln2_g, ln2_b, gla_consts, rope_tabs):
    t = bsz * seq
    row = lambda v: v.astype(F32)[None, :]
    ul, uh, g, a = _inproj(x2d, _pack_w_in(w_in))

    mats = _s5_matrices(a_re, a_im, log_step, b_re, b_im, c_re, c_im, d_skip, seq // S5_CHUNK)
    yal, yah = _s5(ul.reshape(bsz, seq, 128), uh.reshape(bsz, seq, 128), mats, w_glu.astype(BF16), row(b_glu))

    waf, wab = _pad_gate_w(gla_w_a)
    yb = _gla(g.reshape(bsz, seq, G_COLS), waf, wab, gla_b_a.astype(F32), gla_consts, row(gla_ln_g))
    yc = _swa(a.reshape(bsz, seq, A_COLS), swa_sink.astype(F32), rope_tabs)

    wo = w_out.astype(BF16)
    return _out_ffn(x2d, yal.reshape(t, 128), yah.reshape(t, 128), yb.reshape(t, GLA_WIDTH), yc.reshape(t, SWA_WIDTH),
                    wo[0:S5_WIDTH], wo[S5_WIDTH:S5_WIDTH + GLA_WIDTH], wo[S5_WIDTH + GLA_WIDTH:],
                    row(ln1_g), row(ln1_b), w_ff1.astype(BF16), w_ff2.astype(BF16),
                    row(ln2_g), row(ln2_b), alpha)


def kernel(x, w_in, s5_a_re, s5_a_im, s5_log_step, s5_b_re, s5_b_im, s5_c_re, s5_c_im, s5_d, s5_w_glu, s5_b_glu, gla_w_a, gla_b_a, gla_ln_g, swa_sink, w_out, ln1_g, ln1_b, w_ff1, w_ff2, ln2_g, ln2_b):
    bsz, seq, _ = x.shape
    depth = w_in.shape[0]
    alpha = (2 * depth) ** 0.25
    gla_consts = _gla_constants()
    rope_tabs = _rope_tables(seq)
    h = x.reshape(bsz * seq, D_MODEL)
    for l in range(depth):
        h = _layer(h, bsz, seq, alpha, w_in[l], s5_a_re[l], s5_a_im[l], s5_log_step[l],
                   s5_b_re[l], s5_b_im[l], s5_c_re[l], s5_c_im[l], s5_d[l], s5_w_glu[l], s5_b_glu[l],
                   gla_w_a[l], gla_b_a[l], gla_ln_g[l], swa_sink[l], w_out[l], ln1_g[l], ln1_b[l],
                   w_ff1[l], w_ff2[l], ln2_g[l], ln2_b[l], gla_consts, rope_tabs)
    return h.reshape(bsz, seq, D_MODEL)
```

```python
import functools
import math

import jax
import jax.numpy as jnp
import numpy as np
from jax import lax
from jax.experimental import pallas as pl
from jax.experimental.pallas import tpu as pltpu

F32 = jnp.float32
BF16 = jnp.bfloat16

D_MODEL = 1024
S5_WIDTH = 256
S5_GROUP = 16
S5_GROUPS = 16
S5_STATE = 64
S5_CHUNK = 16
GLA_HEADS = 4
GLA_DK = 32
GLA_DV = 64
GLA_QK = 128
GLA_WIDTH = 256
GLA_RANK = 16
GLA_TAU = 16.0
GLA_CHUNK = 64
SWA_WIDTH = 512
SWA_HEAD_DIM = 64
SWA_HEADS = 8
SWA_KV_HEADS = 2
SWA_GROUP = SWA_HEADS // SWA_KV_HEADS
SWA_KV = 128
SWA_WINDOW = 128
SWA_BLOCK = 128
ROT_DIM = 16
ROPE_THETA = 500000.0
D_FF = 4096
LN_EPS = 1e-5
NEG_BIG = -1e30
LOG2E = 1.4426950408889634

U_COLS = S5_WIDTH
G_COLS = GLA_QK + GLA_QK + GLA_WIDTH + GLA_WIDTH + 128
A_COLS = SWA_WIDTH + SWA_KV + SWA_KV
IN_COLS = U_COLS + G_COLS + A_COLS

VMEM_LIMIT = 56 * 1024 * 1024
ROW_TILE = 512
FF_CHUNK = 512
FFN_TILE = 1024
FFN_SUB = 512
SWA_ROWS = 64

_NT = (((1,), (1,)), ((), ()))
_TN = (((0,), (0,)), ((), ()))


def _mm(a, b):
    return jnp.dot(a, b, preferred_element_type=F32)


def _params(*sem):
    return pltpu.CompilerParams(dimension_semantics=sem, vmem_limit_bytes=VMEM_LIMIT)


def _resident(shape):
    nd = len(shape)
    return pl.BlockSpec(shape, lambda *_: (0,) * nd)


def _inproj_kernel(x_ref, w_ref, ul_ref, uh_ref, g_ref, a_ref):
    x = x_ref[...].astype(BF16)
    u = _mm(x, w_ref[:, 0:U_COLS])
    ul_ref[...] = u[:, :128]
    uh_ref[...] = u[:, 128:]
    g_ref[...] = _mm(x, w_ref[:, U_COLS:U_COLS + G_COLS])
    a_ref[...] = _mm(x, w_ref[:, U_COLS + G_COLS:IN_COLS])


def _inproj(x2d, w_packed):
    t = x2d.shape[0]
    tm = min(ROW_TILE, t)
    row = lambda n: pl.BlockSpec((tm, n), lambda i: (i, 0))
    return pl.pallas_call(
        _inproj_kernel,
        grid=(t // tm,),
        in_specs=[row(D_MODEL), _resident((D_MODEL, IN_COLS))],
        out_specs=[row(128), row(128), row(G_COLS), row(A_COLS)],
        out_shape=[jax.ShapeDtypeStruct((t, 128), F32), jax.ShapeDtypeStruct((t, 128), F32),
                   jax.ShapeDtypeStruct((t, G_COLS), F32), jax.ShapeDtypeStruct((t, A_COLS), F32)],
        compiler_params=_params("parallel"),
        name="inproj",
    )(x2d, w_packed)


def _s5_matrices(a_re, a_im, log_step, b_re, b_im, c_re, c_im, d_skip, nch):
    c, ng, w = S5_CHUNK, S5_GROUPS, S5_CHUNK * S5_GROUP
    ar, ai = a_re.astype(F32), a_im.astype(F32)
    step = jnp.exp(log_step.astype(F32))
    lr, li = ar * step, ai * step

    def lam_pow(z, k):
        kk = jnp.asarray(k, F32)[None, :, None, None]
        mag = jnp.exp(kk * lr[z][:, None, None, :])
        ang = kk * li[z][:, None, None, :]
        return mag * jnp.cos(ang), mag * jnp.sin(ang)

    def cmul(xr, xi, yr, yi):
        return xr * yr - xi * yi, xr * yi + xi * yr

    l1r, l1i = jnp.exp(lr) * jnp.cos(li), jnp.exp(lr) * jnp.sin(li)
    nr, ni = l1r - 1.0, l1i
    den = ar * ar + ai * ai
    qr, qi = (nr * ar + ni * ai) / den, (ni * ar - nr * ai) / den
    bt_r = jnp.swapaxes(b_re.astype(F32), -1, -2)
    bt_i = jnp.swapaxes(b_im.astype(F32), -1, -2)
    bbr, bbi = cmul(qr[:, :, None, :], qi[:, :, None, :], bt_r, bt_i)
    cr, ci = c_re.astype(F32), c_im.astype(F32)

    idx = np.arange(c)
    flat = lambda t: t.reshape(ng, w, S5_STATE)

    def with_c(z, k):
        pr, pi = lam_pow(z, k)
        xr, xi = cmul(pr, pi, cr[z][:, None], ci[z][:, None])
        return flat(xr), flat(xi)

    def with_b(z, k):
        pr, pi = lam_pow(z, k)
        yr, yi = cmul(pr, pi, bbr[z][:, None], bbi[z][:, None])
        return flat(yr), flat(yi)

    def re_prod(x, y):
        return jnp.einsum('gmk,gnk->gmn', jnp.concatenate([x[0], -x[1]], axis=-1),
                          jnp.concatenate([y[0], y[1]], axis=-1), precision=lax.Precision.HIGHEST)

    ri = np.arange(w)[:, None] // S5_GROUP
    cj = np.arange(w)[None, :] // S5_GROUP
    tt_mat = (jnp.where(ri >= cj, re_prod(with_c(0, idx), with_b(0, -idx)), 0.0)
              + jnp.where(cj >= ri, re_prod(with_c(1, -idx), with_b(1, idx)), 0.0))

    sf, sb = with_b(0, c - 1 - idx), with_b(1, idx)
    s_mat = jnp.concatenate([sf[0], sb[0], sf[1], sb[1]], axis=-1)
    of, ob = with_c(0, idx + 1), with_c(1, c - idx)
    ot_mat = jnp.concatenate([of[0], ob[0], -of[1], -ob[1]], axis=-1)

    pcr = [lam_pow(z, [c]) for z in range(2)]
    a_r = jnp.concatenate([pcr[0][0][:, 0, 0, :], pcr[1][0][:, 0, 0, :]], axis=-1)
    a_i = jnp.concatenate([pcr[0][1][:, 0, 0, :], pcr[1][1][:, 0, 0, :]], axis=-1)
    d_col = jnp.broadcast_to(jnp.tile(d_skip.astype(F32), (1, c))[:, :, None], (ng, w, nch))
    return tt_mat.astype(BF16), s_mat.astype(BF16), ot_mat.astype(BF16), a_r, a_i, d_col


def _gelu_tanh(x):
    return 0.5 * x * (1.0 + jnp.tanh(math.sqrt(2.0 / math.pi) * (x + 0.044715 * (x * x * x))))


def _sigmoid(x):
    return 1.0 / (1.0 + jnp.exp(-x))


def _s5_kernel(ul_ref, uh_ref, tt_ref, s_ref, ot_ref, ar_ref, ai_ref, dcol_ref, wglu_ref, bglu_ref,
               yl_ref, yh_ref, d_scr, cr_scr, ci_scr, efr_scr, efi_scr, ebr_scr, ebi_scr, zt_scr):
    c, ng = S5_CHUNK, S5_GROUPS
    nch = ul_ref.shape[1] // c
    for j in range(c):
        tok = pl.ds(j, nch, stride=c)
        uj = jnp.concatenate([ul_ref[0, tok, :], uh_ref[0, tok, :]], axis=1)
        d_scr[:, j * S5_GROUP:(j + 1) * S5_GROUP, :] = uj.T.reshape(ng, S5_GROUP, nch)
    for g in range(ng):
        dg = d_scr[g].astype(BF16)
        contrib = lax.dot_general(dg, s_ref[g], _TN, preferred_element_type=F32)
        grp = pl.ds(g, nch, stride=ng)
        cr_scr[grp, :] = contrib[:, :128]
        ci_scr[grp, :] = contrib[:, 128:]

    ar, ai = ar_ref[...], ai_ref[...]
    fwd = lax.broadcasted_iota(jnp.int32, (ng, 128), 1) < S5_STATE

    def step(k, carry):
        sre, sim = carry
        up = pl.ds(pl.multiple_of(k * ng, ng), ng)
        dn = pl.ds(pl.multiple_of((nch - 1 - k) * ng, ng), ng)
        efr_scr[up, :] = sre
        efi_scr[up, :] = sim
        ebr_scr[dn, :] = sre
        ebi_scr[dn, :] = sim
        cre = jnp.where(fwd, cr_scr[up, :], cr_scr[dn, :])
        cim = jnp.where(fwd, ci_scr[up, :], ci_scr[dn, :])
        return ar * sre - ai * sim + cre, ar * sim + ai * sre + cim

    zero = jnp.zeros((ng, 128), F32)
    lax.fori_loop(0, nch, step, (zero, zero), unroll=8)

    fwd_all = lax.broadcasted_iota(jnp.int32, (nch, 128), 1) < S5_STATE
    for g in range(ng):
        dg32 = d_scr[g]
        grp = pl.ds(g, nch, stride=ng)
        enter = jnp.concatenate([jnp.where(fwd_all, efr_scr[grp, :], ebr_scr[grp, :]),
                                 jnp.where(fwd_all, efi_scr[grp, :], ebi_scr[grp, :])],
                                axis=1).astype(BF16)
        yt = (_mm(tt_ref[g], dg32.astype(BF16))
              + lax.dot_general(ot_ref[g], enter, _NT, preferred_element_type=F32)
              + dcol_ref[g] * dg32)
        zt_scr[:, g * S5_GROUP:(g + 1) * S5_GROUP, :] = _gelu_tanh(yt).reshape(c, S5_GROUP, nch)
    for i in range(c):
        zi = zt_scr[i].astype(BF16)
        glu = lax.dot_general(zi, wglu_ref[...], _TN, preferred_element_type=F32) + bglu_ref[...]
        ya = glu[:, :S5_WIDTH] * _sigmoid(glu[:, S5_WIDTH:])
        tok = pl.ds(i, nch, stride=c)
        yl_ref[0, tok, :] = ya[:, :128]
        yh_ref[0, tok, :] = ya[:, 128:]


def _s5(ul3, uh3, mats, wglu, bglu):
    b, seq, _ = ul3.shape
    nch = seq // S5_CHUNK
    tt_mat, s_mat, ot_mat, a_r, a_i, d_col = mats
    w = S5_CHUNK * S5_GROUP
    half = pl.BlockSpec((1, seq, 128), lambda i: (i, 0, 0))
    state_scr = pltpu.VMEM((S5_GROUPS * nch, 2 * S5_STATE), F32)
    return pl.pallas_call(
        _s5_kernel,
        grid=(b,),
        in_specs=[half, half,
                  _resident((S5_GROUPS, w, w)), _resident((S5_GROUPS, w, 4 * S5_STATE)),
                  _resident((S5_GROUPS, w, 4 * S5_STATE)),
                  _resident((S5_GROUPS, 128)), _resident((S5_GROUPS, 128)),
                  _resident((S5_GROUPS, w, nch)),
                  _resident((S5_WIDTH, 2 * S5_WIDTH)), _resident((1, 2 * S5_WIDTH))],
        out_specs=[half, half],
        out_shape=[jax.ShapeDtypeStruct((b, seq, 128), F32)] * 2,
        scratch_shapes=[pltpu.VMEM((S5_GROUPS, w, nch), F32)] + [state_scr] * 6
                       + [pltpu.VMEM((S5_CHUNK, w, nch), F32)],
        compiler_params=_params("parallel"),
        name="s5",
    )(ul3, uh3, tt_mat, s_mat, ot_mat, a_r, a_i, d_col, wglu, bglu)


def _gla_constants():
    r = np.arange(256)
    same = (r[:, None] // GLA_CHUNK) == (r[None, :] // GLA_CHUNK)
    prefix = same & (r[None, :] <= r[:, None])
    suffix = same & (r[None, :] >= r[:, None])
    avg = ((r[:, None] // GLA_DV) == (r[None, :] // GLA_DV)) / float(GLA_DV)
    as_bf = lambda m: jnp.asarray(m.astype(np.float32), BF16)
    return as_bf(prefix), as_bf(suffix), as_bf(same), as_bf(avg)


def _log_sigmoid(x):
    return jnp.minimum(x, 0.0) - jnp.log(1.0 + jnp.exp(-jnp.abs(x)))


def _split_mm(a, b):
    hi = a.astype(BF16)
    lo = (a - hi.astype(F32)).astype(BF16)
    return _mm(hi, b) + _mm(lo, b)


def _chunk_sums(mat, la):
    hi = la.astype(BF16)
    lo = (la - hi.astype(F32)).astype(BF16)
    out = []
    for i in range(la.shape[0] // 256):
        sl = slice(i * 256, (i + 1) * 256)
        out.append(_mm(mat, hi[sl]) + _mm(mat, lo[sl]))
    return jnp.concatenate(out, axis=0)


def _gla_kernel(g_ref, waf_ref, wab_ref, ba_ref, pre_ref, suf_ref, tot_ref, avg_ref, lng_ref, o_ref,
                q_s, kin_s, kst_s, v_s, totf_s, totb_s, o_s, kv_s, st_s):
    seq = g_ref.shape[1]
    nchunk = seq // GLA_CHUNK
    c = GLA_CHUNK
    q = g_ref[0, :, 0:128] * (GLA_DK ** -0.5)
    k = g_ref[0, :, 128:256]
    lfb = g_ref[0, :, 768:896].astype(BF16)
    la_f = _log_sigmoid(_mm(lfb, waf_ref[...]) + ba_ref[0:1, :]) * (1.0 / GLA_TAU)
    la_b = _log_sigmoid(_mm(lfb, wab_ref[...]) + ba_ref[1:2, :]) * (1.0 / GLA_TAU)

    bf = _chunk_sums(pre_ref[...], la_f)
    tf = _chunk_sums(tot_ref[...], la_f)
    q_s[:, 0:128] = (q * jnp.exp(bf)).astype(BF16)
    kin_s[:, 0:128] = (k * jnp.exp(-bf)).astype(BF16)
    kst_s[:, 0:128] = (k * jnp.exp(tf - bf)).astype(BF16)
    totf_s[...] = tf
    bs = _chunk_sums(suf_ref[...], la_b)
    tb = _chunk_sums(tot_ref[...], la_b)
    q_s[:, 128:256] = (q * jnp.exp(bs)).astype(BF16)
    kin_s[:, 128:256] = (k * jnp.exp(-bs)).astype(BF16)
    kst_s[:, 128:256] = (k * jnp.exp(tb - bs)).astype(BF16)
    totb_s[...] = tb
    v_s[...] = g_ref[0, :, 256:512].astype(BF16)

    r256 = lax.broadcasted_iota(jnp.int32, (4 * c, 256), 0)
    l256 = lax.broadcasted_iota(jnp.int32, (4 * c, 256), 1)
    head_k = (r256 // c) == ((l256 % 128) // GLA_DK)
    head_v = (r256 // c) == (l256 // GLA_DV)
    qi = lax.broadcasted_iota(jnp.int32, (c, 4 * c), 0)
    kj = lax.broadcasted_iota(jnp.int32, (c, 4 * c), 1) % c
    zero_bf = jnp.zeros((), BF16)

    def local(n, carry):
        r0 = pl.multiple_of(n * c, c)
        qq = q_s[pl.ds(r0, c), :]
        vv = v_s[pl.ds(r0, c), :]
        kexp = jnp.where(head_k, jnp.concatenate([kin_s[pl.ds(r0, c), :]] * 4, axis=0), zero_bf)
        vexp = jnp.where(head_v, jnp.concatenate([vv] * 4, axis=0), zero_bf)
        sc_f = lax.dot_general(qq[:, 0:128], kexp[:, 0:128], _NT, preferred_element_type=F32)
        sc_b = lax.dot_general(qq[:, 128:256], kexp[:, 128:256], _NT, preferred_element_type=F32)
        pm = jnp.where(kj <= qi, sc_f, sc_b).astype(BF16)
        o_s[pl.ds(r0, c), :] = _mm(pm, vexp)
        kv_t = lax.dot_general(vv, kst_s[pl.ds(r0, c), :], _TN, preferred_element_type=F32)
        kv_s[n] = jnp.where(head_k, kv_t, 0.0)
        return carry

    lax.fori_loop(0, nchunk, local, 0, unroll=4)

    def carry_states(n, carry):
        st_f, st_b = carry
        m = nchunk - 1 - n
        st_s[n, :, 0:128] = st_f.astype(BF16)
        st_s[m, :, 128:256] = st_b.astype(BF16)
        st_f = jnp.exp(totf_s[pl.ds(n * c, 1), :]) * st_f + kv_s[n, :, 0:128]
        st_b = jnp.exp(totb_s[pl.ds(m * c, 1), :]) * st_b + kv_s[m, :, 128:256]
        return st_f, st_b

    zero = jnp.zeros((GLA_WIDTH, GLA_QK), F32)
    lax.fori_loop(0, nchunk, carry_states, (zero, zero))

    def cross(n, carry):
        r0 = pl.multiple_of(n * c, c)
        o_s[pl.ds(r0, c), :] += lax.dot_general(q_s[pl.ds(r0, c), :], st_s[n], _NT, preferred_element_type=F32)
        return carry

    lax.fori_loop(0, nchunk, cross, 0, unroll=8)

    o = o_s[...]
    avg = avg_ref[...]
    mu = _split_mm(o, avg)
    dlt = o - mu
    var = _split_mm(dlt * dlt, avg)
    on = dlt * lax.rsqrt(var + LN_EPS)
    r = g_ref[0, :, 512:768]
    o_ref[0] = on * lng_ref[...] * (r * _sigmoid(r))


def _gla(g3, waf, wab, ba, consts, ln_g):
    b, seq, _ = g3.shape
    pre, suf, tot, avg = consts
    nchunk = seq // GLA_CHUNK
    both = pltpu.VMEM((seq, 2 * GLA_QK), BF16)
    return pl.pallas_call(
        _gla_kernel,
        grid=(b,),
        in_specs=[pl.BlockSpec((1, seq, G_COLS), lambda i: (i, 0, 0)),
                  _resident((128, 128)), _resident((128, 128)), _resident((2, 128)),
                  _resident((256, 256)), _resident((256, 256)), _resident((256, 256)),
                  _resident((256, 256)), _resident((1, 256))],
        out_specs=pl.BlockSpec((1, seq, GLA_WIDTH), lambda i: (i, 0, 0)),
        out_shape=jax.ShapeDtypeStruct((b, seq, GLA_WIDTH), F32),
        scratch_shapes=[both, both, both, pltpu.VMEM((seq, GLA_WIDTH), BF16),
                        pltpu.VMEM((seq, GLA_QK), F32), pltpu.VMEM((seq, GLA_QK), F32),
                        pltpu.VMEM((seq, GLA_WIDTH), F32),
                        pltpu.VMEM((nchunk, GLA_WIDTH, 2 * GLA_QK), F32),
                        pltpu.VMEM((nchunk, GLA_WIDTH, 2 * GLA_QK), BF16)],
        compiler_params=_params("parallel"),
        name="gla",
    )(g3, waf, wab, ba, pre, suf, tot, avg, ln_g)


def _rope_tables(seq):
    pos = jnp.arange(seq, dtype=F32)
    inv_freq = ROPE_THETA ** (-jnp.arange(0, ROT_DIM, 2, dtype=F32) / ROT_DIM)
    ang = pos[:, None] * inv_freq[None, :]
    cos, sin = jnp.cos(ang), jnp.sin(ang)
    half = ROT_DIM // 2
    pad = jnp.zeros((seq, SWA_HEAD_DIM - ROT_DIM), F32)
    c_head = jnp.concatenate([cos, cos, pad + 1.0], axis=1)
    s_head = jnp.concatenate([sin, sin, pad], axis=1)
    two = lambda t: jnp.concatenate([t, t], axis=1)
    rot = np.zeros((128, 128), np.float32)
    for base in (0, SWA_HEAD_DIM):
        for d in range(half):
            rot[base + d + half, base + d] = -1.0
            rot[base + d, base + d + half] = 1.0
    return two(c_head), two(s_head), jnp.asarray(rot, BF16)


def _swa_kernel(sink_ref, a_ref, cos_ref, sin_ref, rot_ref, o_ref,
                q_s, klo_s, khi_s, vd_s, sa_s, sb_s, p_s, bias_s, m_s):
    seq = a_ref.shape[1]
    blk = SWA_BLOCK
    hd = SWA_HEAD_DIM
    pairs = SWA_GROUP // 2
    cosv, sinv, rot = cos_ref[...], sin_ref[...], rot_ref[...]

    def rope(x):
        return x * cosv + _split_mm(x, rot) * sinv

    for slab in range(SWA_WIDTH // 128):
        q_s[slab] = (rope(a_ref[0, :, slab * 128:(slab + 1) * 128]) * (hd ** -0.5 * LOG2E)).astype(BF16)
    kr = rope(a_ref[0, :, SWA_WIDTH:SWA_WIDTH + SWA_KV])
    vr = a_ref[0, :, SWA_WIDTH + SWA_KV:A_COLS]
    ksw = pltpu.roll(kr, hd, 1)
    vsw = pltpu.roll(vr, hd, 1)
    low = lax.broadcasted_iota(jnp.int32, (seq, 128), 1) < hd
    body = slice(blk, blk + seq)
    zpad = jnp.zeros((blk, 128), BF16)
    for kh in range(SWA_KV_HEADS):
        for ref in (klo_s, khi_s):
            ref[kh, 0:blk, :] = zpad
            ref[kh, blk + seq:2 * blk + seq, :] = zpad
        vd_s[kh, 0:blk, 0:128] = zpad
        vd_s[kh, blk + seq:2 * blk + seq, 0:128] = zpad
        vd_s[kh, :, 128:256] = jnp.ones((seq + 2 * blk, 128), BF16)
    klo_s[0, body, :] = jnp.where(low, kr, 0.0).astype(BF16)
    khi_s[0, body, :] = jnp.where(low, 0.0, ksw).astype(BF16)
    klo_s[1, body, :] = jnp.where(low, ksw, 0.0).astype(BF16)
    khi_s[1, body, :] = jnp.where(low, 0.0, kr).astype(BF16)
    vd_s[0, body, 0:128] = jnp.where(low, vr, vsw).astype(BF16)
    vd_s[1, body, 0:128] = jnp.where(low, vsw, vr).astype(BF16)

    qi = lax.broadcasted_iota(jnp.int32, (blk, 3 * blk), 0)
    kk = lax.broadcasted_iota(jnp.int32, (blk, 3 * blk), 1)
    band = jnp.abs(qi + blk - kk) <= SWA_WINDOW
    low_o = lax.broadcasted_iota(jnp.int32, (blk, 128), 1) < hd
    rows = SWA_GROUP * blk
    chunks = [(kh, rb) for kh in range(SWA_KV_HEADS) for rb in range(rows // SWA_ROWS)]

    def scores(n, s_s):
        r0 = pl.multiple_of(n * blk, blk)
        for kh in range(SWA_KV_HEADS):
            kcat = jnp.concatenate([klo_s[kh, pl.ds(r0, 3 * blk), :], khi_s[kh, pl.ds(r0, 3 * blk), :]], axis=0)
            for pr in range(pairs):
                qq = q_s[kh * pairs + pr, pl.ds(r0, blk), :]
                s2 = lax.dot_general(qq, kcat, _NT, preferred_element_type=F32)
                s_s[kh, (2 * pr) * blk:(2 * pr + 1) * blk, :] = s2[:, :3 * blk]
                s_s[kh, (2 * pr + 1) * blk:(2 * pr + 2) * blk, :] = s2[:, 3 * blk:]

    nblk = seq // blk

    def attend(n, s_s, s_next):
        scores(jnp.minimum(n + 1, nblk - 1), s_next)
        r0 = pl.multiple_of(n * blk, blk)
        kpos = kk + (n - 1) * blk
        bias_s[...] = jnp.where(band & (kpos >= 0) & (kpos < seq), 0.0, NEG_BIG)
        for kh, rb in chunks:
            rs = slice(rb * SWA_ROWS, (rb + 1) * SWA_ROWS)
            bs = slice((rb * SWA_ROWS) % blk, (rb * SWA_ROWS) % blk + SWA_ROWS)
            sink2 = sink_ref[kh * SWA_GROUP + (rb * SWA_ROWS) // blk] * LOG2E
            lo = s_s[kh, rs, 0:blk] + bias_s[bs, 0:blk]
            hi = s_s[kh, rs, 2 * blk:3 * blk] + bias_s[bs, 2 * blk:3 * blk]
            s_s[kh, rs, 0:blk] = lo
            s_s[kh, rs, 2 * blk:3 * blk] = hi
            top = jnp.maximum(jnp.maximum(lo, s_s[kh, rs, blk:2 * blk]), hi)
            m_s[kh, rs, :] = jnp.broadcast_to(jnp.maximum(jnp.max(top, axis=1, keepdims=True), sink2), (SWA_ROWS, 128))
        for kh, rb in chunks:
            rs = slice(rb * SWA_ROWS, (rb + 1) * SWA_ROWS)
            p = jnp.exp2(s_s[kh, rs, :] - jnp.concatenate([m_s[kh, rs, :]] * 3, axis=1))
            p_s[kh, rs, :] = p.astype(BF16)
        for kh in range(SWA_KV_HEADS):
            o3 = _mm(p_s[kh], vd_s[kh, pl.ds(r0, 3 * blk), :])
            sink2 = jnp.concatenate([jnp.full((blk, 128), sink_ref[kh * SWA_GROUP + g] * LOG2E, F32)
                                     for g in range(SWA_GROUP)], axis=0)
            o2 = o3[:, 0:128] / (o3[:, 128:256] + jnp.exp2(sink2 - m_s[kh]))
            o_ref[0, pl.ds(r0, blk), kh * SWA_GROUP * hd:(kh + 1) * SWA_GROUP * hd] = jnp.concatenate(
                [jnp.where(low_o, o2[(2 * pr) * blk:(2 * pr + 1) * blk], o2[(2 * pr + 1) * blk:(2 * pr + 2) * blk])
                 for pr in range(pairs)], axis=1)

    def body_fn(t, carry):
        attend(2 * t, sa_s, sb_s)
        attend(2 * t + 1, sb_s, sa_s)
        return carry

    scores(0, sa_s)
    lax.fori_loop(0, nblk // 2, body_fn, 0)


def _swa(a3, sink, tables):
    b, seq, _ = a3.shape
    cos_t, sin_t, rot = tables
    tab = pl.BlockSpec((seq, 128), lambda i, s: (0, 0))
    rot_spec = pl.BlockSpec((128, 128), lambda i, s: (0, 0))
    banded = pltpu.VMEM((SWA_KV_HEADS, seq + 2 * SWA_BLOCK, 128), BF16)
    stat = pltpu.VMEM((SWA_KV_HEADS, SWA_GROUP * SWA_BLOCK, 128), F32)
    grid_spec = pltpu.PrefetchScalarGridSpec(
        num_scalar_prefetch=1,
        grid=(b,),
        in_specs=[pl.BlockSpec((1, seq, A_COLS), lambda i, s: (i, 0, 0)), tab, tab, rot_spec],
        out_specs=pl.BlockSpec((1, seq, SWA_WIDTH), lambda i, s: (i, 0, 0)),
        scratch_shapes=[pltpu.VMEM((SWA_WIDTH // 128, seq, 128), BF16), banded, banded,
                        pltpu.VMEM((SWA_KV_HEADS, seq + 2 * SWA_BLOCK, 256), BF16),
                        pltpu.VMEM((SWA_KV_HEADS, SWA_GROUP * SWA_BLOCK, 3 * SWA_BLOCK), F32),
                        pltpu.VMEM((SWA_KV_HEADS, SWA_GROUP * SWA_BLOCK, 3 * SWA_BLOCK), F32),
                        pltpu.VMEM((SWA_KV_HEADS, SWA_GROUP * SWA_BLOCK, 3 * SWA_BLOCK), BF16),
                        pltpu.VMEM((SWA_BLOCK, 3 * SWA_BLOCK), F32), stat],
    )
    return pl.pallas_call(
        _swa_kernel,
        grid_spec=grid_spec,
        out_shape=jax.ShapeDtypeStruct((b, seq, SWA_WIDTH), F32),
        compiler_params=_params("parallel"),
        name="swa",
    )(sink, a3, cos_t, sin_t, rot)


def _layer_norm(x, g, b):
    mu = jnp.mean(x, axis=-1, keepdims=True)
    d = x - mu
    var = jnp.mean(d * d, axis=-1, keepdims=True)
    return d * lax.rsqrt(var + LN_EPS) * g + b


def _out_ffn_kernel(x_ref, yal_ref, yah_ref, yb_ref, yc_ref, woa_ref, wob_ref, woc_ref,
                    g1_ref, b1_ref, w1_ref, w2_ref, g2_ref, b2_ref, o_ref, acc_ref, x1b_ref, *, alpha):
    subs = [slice(r, r + FFN_SUB) for r in range(0, x_ref.shape[0], FFN_SUB)]
    for rows in subs:
        ya = jnp.concatenate([yal_ref[rows, :], yah_ref[rows, :]], axis=1)
        mix = (_mm(ya.astype(BF16), woa_ref[...]) + _mm(yb_ref[rows, :].astype(BF16), wob_ref[...])
               + _mm(yc_ref[rows, :].astype(BF16), woc_ref[...]))
        x1 = _layer_norm(alpha * x_ref[rows, :] + mix, g1_ref[...], b1_ref[...])
        o_ref[rows, :] = x1
        x1b_ref[rows, :] = x1.astype(BF16)
    for rows in subs:
        for ci in range(D_FF // FF_CHUNK):
            sl = slice(ci * FF_CHUNK, (ci + 1) * FF_CHUNK)
            hid = jnp.maximum(_mm(x1b_ref[rows, :], w1_ref[:, sl]), 0.0)
            part = _mm((hid * hid).astype(BF16), w2_ref[sl, :])
            if ci == 0:
                acc_ref[rows, :] = part
            else:
                acc_ref[rows, :] += part
    for rows in subs:
        o_ref[rows, :] = _layer_norm(alpha * o_ref[rows, :] + acc_ref[rows, :], g2_ref[...], b2_ref[...])


def _out_ffn(x2d, yal2d, yah2d, yb2d, yc2d, woa, wob, woc, g1, b1, w1, w2, g2, b2, alpha):
    t = x2d.shape[0]
    tm = min(FFN_TILE, t)
    row = lambda n: pl.BlockSpec((tm, n), lambda i: (i, 0))
    once = lambda shape: pl.BlockSpec(shape, lambda i: (0, 0), pipeline_mode=pl.Buffered(1))
    return pl.pallas_call(
        functools.partial(_out_ffn_kernel, alpha=alpha),
        grid=(t // tm,),
        in_specs=[row(D_MODEL), row(128), row(128), row(GLA_WIDTH), row(SWA_WIDTH),
                  once((S5_WIDTH, D_MODEL)), once((GLA_WIDTH, D_MODEL)), once((SWA_WIDTH, D_MODEL)),
                  once((1, D_MODEL)), once((1, D_MODEL)),
                  once((D_MODEL, D_FF)), once((D_FF, D_MODEL)),
                  once((1, D_MODEL)), once((1, D_MODEL))],
        out_specs=row(D_MODEL),
        out_shape=jax.ShapeDtypeStruct((t, D_MODEL), F32),
        scratch_shapes=[pltpu.VMEM((tm, D_MODEL), F32), pltpu.VMEM((tm, D_MODEL), BF16)],
        compiler_params=_params("parallel"),
        name="out_ffn",
    )(x2d, yal2d, yah2d, yb2d, yc2d, woa, wob, woc, g1, b1, w1, w2, g2, b2)


def _pack_w_in(w_in):
    points = np.cumsum([S5_WIDTH, GLA_QK, GLA_QK, GLA_WIDTH, GLA_WIDTH, GLA_RANK, GLA_RANK,
                        SWA_WIDTH, SWA_KV]).tolist()
    s5, gq, gk, gv, gr, glf, glb, aq, ak, av = jnp.split(w_in, points, axis=-1)
    pad = jnp.zeros((w_in.shape[0], 128 - 2 * GLA_RANK), w_in.dtype)
    return jnp.concatenate([s5, gq, gk, gv, gr, glf, glb, pad, aq, ak, av], axis=-1).astype(BF16)


def _pad_gate_w(w_a):
    z = jnp.zeros((128, GLA_QK), F32)
    waf = z.at[0:GLA_RANK].set(w_a[0].astype(F32))
    wab = z.at[GLA_RANK:2 * GLA_RANK].set(w_a[1].astype(F32))
    return waf.astype(BF16), wab.astype(BF16)


def _layer(x2d, bsz, seq, alpha, w_in, a_re, a_im, log_step, b_re, b_im, c_re, c_im, d_skip,
           w_glu, b_glu, gla_w_a, gla_b_a, gla_ln_g, swa_sink, w_out, ln1_g, ln1_b, w_ff1, w_ff2,
           ln2_g, ln2_b, gla_consts, rope_tabs):
    t = bsz * seq
    row = lambda v: v.astype(F32)[None, :]
    ul, uh, g, a = _inproj(x2d, _pack_w_in(w_in))

    mats = _s5_matrices(a_re, a_im, log_step, b_re, b_im, c_re, c_im, d_skip, seq // S5_CHUNK)
    yal, yah = _s5(ul.reshape(bsz, seq, 128), uh.reshape(bsz, seq, 128), mats, w_glu.astype(BF16), row(b_glu))

    waf, wab = _pad_gate_w(gla_w_a)
    yb = _gla(g.reshape(bsz, seq, G_COLS), waf, wab, gla_b_a.astype(F32), gla_consts, row(gla_ln_g))
    yc = _swa(a.reshape(bsz, seq, A_COLS), swa_sink.astype(F32), rope_tabs)

    wo = w_out.astype(BF16)
    return _out_ffn(x2d, yal.reshape(t, 128), yah.reshape(t, 128), yb.reshape(t, GLA_WIDTH), yc.reshape(t, SWA_WIDTH),
                    wo[0:S5_WIDTH], wo[S5_WIDTH:S5_WIDTH + GLA_WIDTH], wo[S5_WIDTH + GLA_WIDTH:],
                    row(ln1_g), row(ln1_b), w_ff1.astype(BF16), w_ff2.astype(BF16),
                    row(ln2_g), row(ln2_b), alpha)


def kernel(x, w_in, s5_a_re, s5_a_im, s5_log_step, s5_b_re, s5_b_im, s5_c_re, s5_c_im, s5_d, s5_w_glu, s5_b_glu, gla_w_a, gla_b_a, gla_ln_g, swa_sink, w_out, ln1_g, ln1_b, w_ff1, w_ff2, ln2_g, ln2_b):
    bsz, seq, _ = x.shape
    depth = w_in.shape[0]
    alpha = (2 * depth) ** 0.25
    gla_consts = _gla_constants()
    rope_tabs = _rope_tables(seq)
    h = x.reshape(bsz * seq, D_MODEL)
    for l in range(depth):
        h = _layer(h, bsz, seq, alpha, w_in[l], s5_a_re[l], s5_a_im[l], s5_log_step[l],
                   s5_b_re[l], s5_b_im[l], s5_c_re[l], s5_c_im[l], s5_d[l], s5_w_glu[l], s5_b_glu[l],
                   gla_w_a[l], gla_b_a[l], gla_ln_g[l], swa_sink[l], w_out[l], ln1_g[l], ln1_b[l],
                   w_ff1[l], w_ff2[l], ln2_g[l], ln2_b[l], gla_consts, rope_tabs)
    return h.reshape(bsz, seq, D_MODEL)
```

```python
import functools
import math

import jax
import jax.numpy as jnp
import numpy as np
from jax import lax
from jax.experimental import pallas as pl
from jax.experimental.pallas import tpu as pltpu

F32 = jnp.float32
BF16 = jnp.bfloat16

D_MODEL = 1024
S5_WIDTH = 256
S5_GROUP = 16
S5_GROUPS = 16
S5_STATE = 64
S5_CHUNK = 16
GLA_HEADS = 4
GLA_DK = 32
GLA_DV = 64
GLA_QK = 128
GLA_WIDTH = 256
GLA_RANK = 16
GLA_TAU = 16.0
GLA_CHUNK = 64
SWA_WIDTH = 512
SWA_HEAD_DIM = 64
SWA_HEADS = 8
SWA_KV_HEADS = 2
SWA_GROUP = SWA_HEADS // SWA_KV_HEADS
SWA_KV = 128
SWA_WINDOW = 128
SWA_BLOCK = 128
ROT_DIM = 16
ROPE_THETA = 500000.0
D_FF = 4096
LN_EPS = 1e-5
NEG_BIG = -1e30
LOG2E = 1.4426950408889634

U_COLS = S5_WIDTH
G_COLS = GLA_QK + GLA_QK + GLA_WIDTH + GLA_WIDTH + 128
A_COLS = SWA_WIDTH + SWA_KV + SWA_KV
IN_COLS = U_COLS + G_COLS + A_COLS

VMEM_LIMIT = 56 * 1024 * 1024
ROW_TILE = 1024
FF_CHUNK = 512
FFN_TILE = 1024
FFN_SUB = 512
SWA_ROWS = 64

_NT = (((1,), (1,)), ((), ()))
_TN = (((0,), (0,)), ((), ()))


def _mm(a, b):
    return jnp.dot(a, b, preferred_element_type=F32)


def _params(*sem):
    return pltpu.CompilerParams(dimension_semantics=sem, vmem_limit_bytes=VMEM_LIMIT)


def _resident(shape):
    nd = len(shape)
    return pl.BlockSpec(shape, lambda *_: (0,) * nd)


def _inproj_kernel(x_ref, w_ref, ul_ref, uh_ref, g_ref, a_ref):
    x = x_ref[...].astype(BF16)
    u = _mm(x, w_ref[:, 0:U_COLS])
    ul_ref[...] = u[:, :128]
    uh_ref[...] = u[:, 128:]
    g_ref[...] = _mm(x, w_ref[:, U_COLS:U_COLS + G_COLS])
    a_ref[...] = _mm(x, w_ref[:, U_COLS + G_COLS:IN_COLS])


def _inproj(x2d, w_packed):
    t = x2d.shape[0]
    tm = min(ROW_TILE, t)
    row = lambda n: pl.BlockSpec((tm, n), lambda i: (i, 0))
    return pl.pallas_call(
        _inproj_kernel,
        grid=(t // tm,),
        in_specs=[row(D_MODEL), _resident((D_MODEL, IN_COLS))],
        out_specs=[row(128), row(128), row(G_COLS), row(A_COLS)],
        out_shape=[jax.ShapeDtypeStruct((t, 128), F32), jax.ShapeDtypeStruct((t, 128), F32),
                   jax.ShapeDtypeStruct((t, G_COLS), F32), jax.ShapeDtypeStruct((t, A_COLS), F32)],
        compiler_params=_params("parallel"),
        name="inproj",
    )(x2d, w_packed)


def _s5_matrices(a_re, a_im, log_step, b_re, b_im, c_re, c_im, d_skip, nch):
    c, ng, w = S5_CHUNK, S5_GROUPS, S5_CHUNK * S5_GROUP
    ar, ai = a_re.astype(F32), a_im.astype(F32)
    step = jnp.exp(log_step.astype(F32))
    lr, li = ar * step, ai * step

    def lam_pow(z, k):
        kk = jnp.asarray(k, F32)[None, :, None, None]
        mag = jnp.exp(kk * lr[z][:, None, None, :])
        ang = kk * li[z][:, None, None, :]
        return mag * jnp.cos(ang), mag * jnp.sin(ang)

    def cmul(xr, xi, yr, yi):
        return xr * yr - xi * yi, xr * yi + xi * yr

    l1r, l1i = jnp.exp(lr) * jnp.cos(li), jnp.exp(lr) * jnp.sin(li)
    nr, ni = l1r - 1.0, l1i
    den = ar * ar + ai * ai
    qr, qi = (nr * ar + ni * ai) / den, (ni * ar - nr * ai) / den
    bt_r = jnp.swapaxes(b_re.astype(F32), -1, -2)
    bt_i = jnp.swapaxes(b_im.astype(F32), -1, -2)
    bbr, bbi = cmul(qr[:, :, None, :], qi[:, :, None, :], bt_r, bt_i)
    cr, ci = c_re.astype(F32), c_im.astype(F32)

    idx = np.arange(c)
    flat = lambda t: t.reshape(ng, w, S5_STATE)

    def with_c(z, k):
        pr, pi = lam_pow(z, k)
        xr, xi = cmul(pr, pi, cr[z][:, None], ci[z][:, None])
        return flat(xr), flat(xi)

    def with_b(z, k):
        pr, pi = lam_pow(z, k)
        yr, yi = cmul(pr, pi, bbr[z][:, None], bbi[z][:, None])
        return flat(yr), flat(yi)

    def re_prod(x, y):
        return jnp.einsum('gmk,gnk->gmn', jnp.concatenate([x[0], -x[1]], axis=-1),
                          jnp.concatenate([y[0], y[1]], axis=-1), precision=lax.Precision.HIGHEST)

    ri = np.arange(w)[:, None] // S5_GROUP
    cj = np.arange(w)[None, :] // S5_GROUP
    tt_mat = (jnp.where(ri >= cj, re_prod(with_c(0, idx), with_b(0, -idx)), 0.0)
              + jnp.where(cj >= ri, re_prod(with_c(1, -idx), with_b(1, idx)), 0.0))

    sf, sb = with_b(0, c - 1 - idx), with_b(1, idx)
    s_mat = jnp.concatenate([sf[0], sb[0], sf[1], sb[1]], axis=-1)
    of, ob = with_c(0, idx + 1), with_c(1, c - idx)
    ot_mat = jnp.concatenate([of[0], ob[0], -of[1], -ob[1]], axis=-1)

    pcr = [lam_pow(z, [c]) for z in range(2)]
    a_r = jnp.concatenate([pcr[0][0][:, 0, 0, :], pcr[1][0][:, 0, 0, :]], axis=-1)
    a_i = jnp.concatenate([pcr[0][1][:, 0, 0, :], pcr[1][1][:, 0, 0, :]], axis=-1)
    d_col = jnp.broadcast_to(jnp.tile(d_skip.astype(F32), (1, c))[:, :, None], (ng, w, nch))
    return tt_mat.astype(BF16), s_mat.astype(BF16), ot_mat.astype(BF16), a_r, a_i, d_col


def _gelu_tanh(x):
    return 0.5 * x * (1.0 + jnp.tanh(math.sqrt(2.0 / math.pi) * (x + 0.044715 * (x * x * x))))


def _sigmoid(x):
    return 1.0 / (1.0 + jnp.exp(-x))


def _s5_kernel(ul_ref, uh_ref, tt_ref, s_ref, ot_ref, ar_ref, ai_ref, dcol_ref, wglu_ref, bglu_ref,
               yl_ref, yh_ref, d_scr, cr_scr, ci_scr, efr_scr, efi_scr, ebr_scr, ebi_scr, zt_scr):
    c, ng = S5_CHUNK, S5_GROUPS
    nch = ul_ref.shape[1] // c
    for j in range(c):
        tok = pl.ds(j, nch, stride=c)
        uj = jnp.concatenate([ul_ref[0, tok, :], uh_ref[0, tok, :]], axis=1)
        d_scr[:, j * S5_GROUP:(j + 1) * S5_GROUP, :] = uj.T.reshape(ng, S5_GROUP, nch)
    for g in range(ng):
        dg = d_scr[g].astype(BF16)
        contrib = lax.dot_general(dg, s_ref[g], _TN, preferred_element_type=F32)
        grp = pl.ds(g, nch, stride=ng)
        cr_scr[grp, :] = contrib[:, :128]
        ci_scr[grp, :] = contrib[:, 128:]

    ar, ai = ar_ref[...], ai_ref[...]
    fwd = lax.broadcasted_iota(jnp.int32, (ng, 128), 1) < S5_STATE

    def step(k, carry):
        sre, sim = carry
        up = pl.ds(pl.multiple_of(k * ng, ng), ng)
        dn = pl.ds(pl.multiple_of((nch - 1 - k) * ng, ng), ng)
        efr_scr[up, :] = sre
        efi_scr[up, :] = sim
        ebr_scr[dn, :] = sre
        ebi_scr[dn, :] = sim
        cre = jnp.where(fwd, cr_scr[up, :], cr_scr[dn, :])
        cim = jnp.where(fwd, ci_scr[up, :], ci_scr[dn, :])
        return ar * sre - ai * sim + cre, ar * sim + ai * sre + cim

    zero = jnp.zeros((ng, 128), F32)
    lax.fori_loop(0, nch, step, (zero, zero), unroll=8)

    fwd_all = lax.broadcasted_iota(jnp.int32, (nch, 128), 1) < S5_STATE
    for g in range(ng):
        dg32 = d_scr[g]
        grp = pl.ds(g, nch, stride=ng)
        enter = jnp.concatenate([jnp.where(fwd_all, efr_scr[grp, :], ebr_scr[grp, :]),
                                 jnp.where(fwd_all, efi_scr[grp, :], ebi_scr[grp, :])],
                                axis=1).astype(BF16)
        yt = (_mm(tt_ref[g], dg32.astype(BF16))
              + lax.dot_general(ot_ref[g], enter, _NT, preferred_element_type=F32)
              + dcol_ref[g] * dg32)
        zt_scr[:, g * S5_GROUP:(g + 1) * S5_GROUP, :] = _gelu_tanh(yt).reshape(c, S5_GROUP, nch)
    for i in range(c):
        zi = zt_scr[i].astype(BF16)
        glu = lax.dot_general(zi, wglu_ref[...], _TN, preferred_element_type=F32) + bglu_ref[...]
        ya = glu[:, :S5_WIDTH] * _sigmoid(glu[:, S5_WIDTH:])
        tok = pl.ds(i, nch, stride=c)
        yl_ref[0, tok, :] = ya[:, :128]
        yh_ref[0, tok, :] = ya[:, 128:]


def _s5(ul3, uh3, mats, wglu, bglu):
    b, seq, _ = ul3.shape
    nch = seq // S5_CHUNK
    tt_mat, s_mat, ot_mat, a_r, a_i, d_col = mats
    w = S5_CHUNK * S5_GROUP
    half = pl.BlockSpec((1, seq, 128), lambda i: (i, 0, 0))
    state_scr = pltpu.VMEM((S5_GROUPS * nch, 2 * S5_STATE), F32)
    return pl.pallas_call(
        _s5_kernel,
        grid=(b,),
        in_specs=[half, half,
                  _resident((S5_GROUPS, w, w)), _resident((S5_GROUPS, w, 4 * S5_STATE)),
                  _resident((S5_GROUPS, w, 4 * S5_STATE)),
                  _resident((S5_GROUPS, 128)), _resident((S5_GROUPS, 128)),
                  _resident((S5_GROUPS, w, nch)),
                  _resident((S5_WIDTH, 2 * S5_WIDTH)), _resident((1, 2 * S5_WIDTH))],
        out_specs=[half, half],
        out_shape=[jax.ShapeDtypeStruct((b, seq, 128), F32)] * 2,
        scratch_shapes=[pltpu.VMEM((S5_GROUPS, w, nch), F32)] + [state_scr] * 6
                       + [pltpu.VMEM((S5_CHUNK, w, nch), F32)],
        compiler_params=_params("parallel"),
        name="s5",
    )(ul3, uh3, tt_mat, s_mat, ot_mat, a_r, a_i, d_col, wglu, bglu)


def _gla_constants():
    r = np.arange(256)
    same = (r[:, None] // GLA_CHUNK) == (r[None, :] // GLA_CHUNK)
    prefix = same & (r[None, :] <= r[:, None])
    suffix = same & (r[None, :] >= r[:, None])
    avg = ((r[:, None] // GLA_DV) == (r[None, :] // GLA_DV)) / float(GLA_DV)
    as_bf = lambda m: jnp.asarray(m.astype(np.float32), BF16)
    return as_bf(prefix), as_bf(suffix), as_bf(same), as_bf(avg)


def _log_sigmoid(x):
    return jnp.minimum(x, 0.0) - jnp.log(1.0 + jnp.exp(-jnp.abs(x)))


def _split_mm(a, b):
    hi = a.astype(BF16)
    lo = (a - hi.astype(F32)).astype(BF16)
    return _mm(hi, b) + _mm(lo, b)


def _chunk_sums(mat, la):
    hi = la.astype(BF16)
    lo = (la - hi.astype(F32)).astype(BF16)
    out = []
    for i in range(la.shape[0] // 256):
        sl = slice(i * 256, (i + 1) * 256)
        out.append(_mm(mat, hi[sl]) + _mm(mat, lo[sl]))
    return jnp.concatenate(out, axis=0)


def _gla_kernel(g_ref, waf_ref, wab_ref, ba_ref, pre_ref, suf_ref, tot_ref, avg_ref, lng_ref, o_ref,
                q_s, kin_s, kst_s, v_s, totf_s, totb_s, o_s, kv_s, st_s):
    seq = g_ref.shape[1]
    nchunk = seq // GLA_CHUNK
    c = GLA_CHUNK
    q = g_ref[0, :, 0:128] * (GLA_DK ** -0.5)
    k = g_ref[0, :, 128:256]
    lfb = g_ref[0, :, 768:896].astype(BF16)
    la_f = _log_sigmoid(_mm(lfb, waf_ref[...]) + ba_ref[0:1, :]) * (1.0 / GLA_TAU)
    la_b = _log_sigmoid(_mm(lfb, wab_ref[...]) + ba_ref[1:2, :]) * (1.0 / GLA_TAU)

    bf = _chunk_sums(pre_ref[...], la_f)
    tf = _chunk_sums(tot_ref[...], la_f)
    q_s[:, 0:128] = (q * jnp.exp(bf)).astype(BF16)
    kin_s[:, 0:128] = (k * jnp.exp(-bf)).astype(BF16)
    kst_s[:, 0:128] = (k * jnp.exp(tf - bf)).astype(BF16)
    totf_s[...] = tf
    bs = _chunk_sums(suf_ref[...], la_b)
    tb = _chunk_sums(tot_ref[...], la_b)
    q_s[:, 128:256] = (q * jnp.exp(bs)).astype(BF16)
    kin_s[:, 128:256] = (k * jnp.exp(-bs)).astype(BF16)
    kst_s[:, 128:256] = (k * jnp.exp(tb - bs)).astype(BF16)
    totb_s[...] = tb
    v_s[...] = g_ref[0, :, 256:512].astype(BF16)

    r256 = lax.broadcasted_iota(jnp.int32, (4 * c, 256), 0)
    l256 = lax.broadcasted_iota(jnp.int32, (4 * c, 256), 1)
    head_k = (r256 // c) == ((l256 % 128) // GLA_DK)
    head_v = (r256 // c) == (l256 // GLA_DV)
    qi = lax.broadcasted_iota(jnp.int32, (c, 4 * c), 0)
    kj = lax.broadcasted_iota(jnp.int32, (c, 4 * c), 1) % c
    zero_bf = jnp.zeros((), BF16)

    def local(n, carry):
        r0 = pl.multiple_of(n * c, c)
        qq = q_s[pl.ds(r0, c), :]
        vv = v_s[pl.ds(r0, c), :]
        kexp = jnp.where(head_k, jnp.concatenate([kin_s[pl.ds(r0, c), :]] * 4, axis=0), zero_bf)
        vexp = jnp.where(head_v, jnp.concatenate([vv] * 4, axis=0), zero_bf)
        sc_f = lax.dot_general(qq[:, 0:128], kexp[:, 0:128], _NT, preferred_element_type=F32)
        sc_b = lax.dot_general(qq[:, 128:256], kexp[:, 128:256], _NT, preferred_element_type=F32)
        pm = jnp.where(kj <= qi, sc_f, sc_b).astype(BF16)
        o_s[pl.ds(r0, c), :] = _mm(pm, vexp)
        kv_t = lax.dot_general(vv, kst_s[pl.ds(r0, c), :], _TN, preferred_element_type=F32)
        kv_s[n] = jnp.where(head_k, kv_t, 0.0)
        return carry

    lax.fori_loop(0, nchunk, local, 0, unroll=4)

    def carry_states(n, carry):
        st_f, st_b = carry
        m = nchunk - 1 - n
        st_s[n, :, 0:128] = st_f.astype(BF16)
        st_s[m, :, 128:256] = st_b.astype(BF16)
        st_f = jnp.exp(totf_s[pl.ds(n * c, 1), :]) * st_f + kv_s[n, :, 0:128]
        st_b = jnp.exp(totb_s[pl.ds(m * c, 1), :]) * st_b + kv_s[m, :, 128:256]
        return st_f, st_b

    zero = jnp.zeros((GLA_WIDTH, GLA_QK), F32)
    lax.fori_loop(0, nchunk, carry_states, (zero, zero))

    def cross(n, carry):
        r0 = pl.multiple_of(n * c, c)
        o_s[pl.ds(r0, c), :] += lax.dot_general(q_s[pl.ds(r0, c), :], st_s[n], _NT, preferred_element_type=F32)
        return carry

    lax.fori_loop(0, nchunk, cross, 0, unroll=8)

    o = o_s[...]
    avg = avg_ref[...]
    mu = _split_mm(o, avg)
    dlt = o - mu
    var = _split_mm(dlt * dlt, avg)
    on = dlt * lax.rsqrt(var + LN_EPS)
    r = g_ref[0, :, 512:768]
    o_ref[0] = on * lng_ref[...] * (r * _sigmoid(r))


def _gla(g3, waf, wab, ba, consts, ln_g):
    b, seq, _ = g3.shape
    pre, suf, tot, avg = consts
    nchunk = seq // GLA_CHUNK
    both = pltpu.VMEM((seq, 2 * GLA_QK), BF16)
    return pl.pallas_call(
        _gla_kernel,
        grid=(b,),
        in_specs=[pl.BlockSpec((1, seq, G_COLS), lambda i: (i, 0, 0)),
                  _resident((128, 128)), _resident((128, 128)), _resident((2, 128)),
                  _resident((256, 256)), _resident((256, 256)), _resident((256, 256)),
                  _resident((256, 256)), _resident((1, 256))],
        out_specs=pl.BlockSpec((1, seq, GLA_WIDTH), lambda i: (i, 0, 0)),
        out_shape=jax.ShapeDtypeStruct((b, seq, GLA_WIDTH), F32),
        scratch_shapes=[both, both, both, pltpu.VMEM((seq, GLA_WIDTH), BF16),
                        pltpu.VMEM((seq, GLA_QK), F32), pltpu.VMEM((seq, GLA_QK), F32),
                        pltpu.VMEM((seq, GLA_WIDTH), F32),
                        pltpu.VMEM((nchunk, GLA_WIDTH, 2 * GLA_QK), F32),
                        pltpu.VMEM((nchunk, GLA_WIDTH, 2 * GLA_QK), BF16)],
        compiler_params=_params("parallel"),
        name="gla",
    )(g3, waf, wab, ba, pre, suf, tot, avg, ln_g)


def _rope_tables(seq):
    pos = jnp.arange(seq, dtype=F32)
    inv_freq = ROPE_THETA ** (-jnp.arange(0, ROT_DIM, 2, dtype=F32) / ROT_DIM)
    ang = pos[:, None] * inv_freq[None, :]
    cos, sin = jnp.cos(ang), jnp.sin(ang)
    half = ROT_DIM // 2
    pad = jnp.zeros((seq, SWA_HEAD_DIM - ROT_DIM), F32)
    c_head = jnp.concatenate([cos, cos, pad + 1.0], axis=1)
    s_head = jnp.concatenate([sin, sin, pad], axis=1)
    two = lambda t: jnp.concatenate([t, t], axis=1)
    rot = np.zeros((128, 128), np.float32)
    for base in (0, SWA_HEAD_DIM):
        for d in range(half):
            rot[base + d + half, base + d] = -1.0
            rot[base + d, base + d + half] = 1.0
    return two(c_head), two(s_head), jnp.asarray(rot, BF16)


def _swa_kernel(sink_ref, a_ref, cos_ref, sin_ref, rot_ref, o_ref,
                q_s, klo_s, khi_s, vd_s, sa_s, sb_s, p_s, bias_s, m_s):
    seq = a_ref.shape[1]
    blk = SWA_BLOCK
    hd = SWA_HEAD_DIM
    pairs = SWA_GROUP // 2
    cosv, sinv, rot = cos_ref[...], sin_ref[...], rot_ref[...]

    def rope(x):
        return x * cosv + _split_mm(x, rot) * sinv

    for slab in range(SWA_WIDTH // 128):
        q_s[slab] = (rope(a_ref[0, :, slab * 128:(slab + 1) * 128]) * (hd ** -0.5 * LOG2E)).astype(BF16)
    kr = rope(a_ref[0, :, SWA_WIDTH:SWA_WIDTH + SWA_KV])
    vr = a_ref[0, :, SWA_WIDTH + SWA_KV:A_COLS]
    ksw = pltpu.roll(kr, hd, 1)
    vsw = pltpu.roll(vr, hd, 1)
    low = lax.broadcasted_iota(jnp.int32, (seq, 128), 1) < hd
    body = slice(blk, blk + seq)
    zpad = jnp.zeros((blk, 128), BF16)
    for kh in range(SWA_KV_HEADS):
        for ref in (klo_s, khi_s):
            ref[kh, 0:blk, :] = zpad
            ref[kh, blk + seq:2 * blk + seq, :] = zpad
        vd_s[kh, 0:blk, 0:128] = zpad
        vd_s[kh, blk + seq:2 * blk + seq, 0:128] = zpad
        vd_s[kh, :, 128:256] = jnp.ones((seq + 2 * blk, 128), BF16)
    klo_s[0, body, :] = jnp.where(low, kr, 0.0).astype(BF16)
    khi_s[0, body, :] = jnp.where(low, 0.0, ksw).astype(BF16)
    klo_s[1, body, :] = jnp.where(low, ksw, 0.0).astype(BF16)
    khi_s[1, body, :] = jnp.where(low, 0.0, kr).astype(BF16)
    vd_s[0, body, 0:128] = jnp.where(low, vr, vsw).astype(BF16)
    vd_s[1, body, 0:128] = jnp.where(low, vsw, vr).astype(BF16)

    qi = lax.broadcasted_iota(jnp.int32, (blk, 3 * blk), 0)
    kk = lax.broadcasted_iota(jnp.int32, (blk, 3 * blk), 1)
    band = jnp.abs(qi + blk - kk) <= SWA_WINDOW
    low_o = lax.broadcasted_iota(jnp.int32, (blk, 128), 1) < hd
    rows = SWA_GROUP * blk
    chunks = [(kh, rb) for kh in range(SWA_KV_HEADS) for rb in range(rows // SWA_ROWS)]

    def scores(n, s_s):
        r0 = pl.multiple_of(n * blk, blk)
        for kh in range(SWA_KV_HEADS):
            kcat = jnp.concatenate([klo_s[kh, pl.ds(r0, 3 * blk), :], khi_s[kh, pl.ds(r0, 3 * blk), :]], axis=0)
            for pr in range(pairs):
                qq = q_s[kh * pairs + pr, pl.ds(r0, blk), :]
                s2 = lax.dot_general(qq, kcat, _NT, preferred_element_type=F32)
                s_s[kh, (2 * pr) * blk:(2 * pr + 1) * blk, :] = s2[:, :3 * blk]
                s_s[kh, (2 * pr + 1) * blk:(2 * pr + 2) * blk, :] = s2[:, 3 * blk:]

    nblk = seq // blk

    def attend(n, s_s, s_next):
        scores(jnp.minimum(n + 1, nblk - 1), s_next)
        r0 = pl.multiple_of(n * blk, blk)
        kpos = kk + (n - 1) * blk
        bias_s[...] = jnp.where(band & (kpos >= 0) & (kpos < seq), 0.0, NEG_BIG)
        for kh, rb in chunks:
            rs = slice(rb * SWA_ROWS, (rb + 1) * SWA_ROWS)
            bs = slice((rb * SWA_ROWS) % blk, (rb * SWA_ROWS) % blk + SWA_ROWS)
            sink2 = sink_ref[kh * SWA_GROUP + (rb * SWA_ROWS) // blk] * LOG2E
            lo = s_s[kh, rs, 0:blk] + bias_s[bs, 0:blk]
            hi = s_s[kh, rs, 2 * blk:3 * blk] + bias_s[bs, 2 * blk:3 * blk]
            s_s[kh, rs, 0:blk] = lo
            s_s[kh, rs, 2 * blk:3 * blk] = hi
            top = jnp.maximum(jnp.maximum(lo, s_s[kh, rs, blk:2 * blk]), hi)
            m_s[kh, rs, :] = jnp.broadcast_to(jnp.maximum(jnp.max(top, axis=1, keepdims=True), sink2), (SWA_ROWS, 128))
        for kh, rb in chunks:
            rs = slice(rb * SWA_ROWS, (rb + 1) * SWA_ROWS)
            p = jnp.exp2(s_s[kh, rs, :] - jnp.concatenate([m_s[kh, rs, :]] * 3, axis=1))
            p_s[kh, rs, :] = p.astype(BF16)
        for kh in range(SWA_KV_HEADS):
            o3 = _mm(p_s[kh], vd_s[kh, pl.ds(r0, 3 * blk), :])
            sink2 = jnp.concatenate([jnp.full((blk, 128), sink_ref[kh * SWA_GROUP + g] * LOG2E, F32)
                                     for g in range(SWA_GROUP)], axis=0)
            o2 = o3[:, 0:128] / (o3[:, 128:256] + jnp.exp2(sink2 - m_s[kh]))
            o_ref[0, pl.ds(r0, blk), kh * SWA_GROUP * hd:(kh + 1) * SWA_GROUP * hd] = jnp.concatenate(
                [jnp.where(low_o, o2[(2 * pr) * blk:(2 * pr + 1) * blk], o2[(2 * pr + 1) * blk:(2 * pr + 2) * blk])
                 for pr in range(pairs)], axis=1)

    def body_fn(t, carry):
        attend(2 * t, sa_s, sb_s)
        attend(2 * t + 1, sb_s, sa_s)
        return carry

    scores(0, sa_s)
    lax.fori_loop(0, nblk // 2, body_fn, 0)


def _swa(a3, sink, tables):
    b, seq, _ = a3.shape
    cos_t, sin_t, rot = tables
    tab = pl.BlockSpec((seq, 128), lambda i, s: (0, 0))
    rot_spec = pl.BlockSpec((128, 128), lambda i, s: (0, 0))
    banded = pltpu.VMEM((SWA_KV_HEADS, seq + 2 * SWA_BLOCK, 128), BF16)
    stat = pltpu.VMEM((SWA_KV_HEADS, SWA_GROUP * SWA_BLOCK, 128), F32)
    grid_spec = pltpu.PrefetchScalarGridSpec(
        num_scalar_prefetch=1,
        grid=(b,),
        in_specs=[pl.BlockSpec((1, seq, A_COLS), lambda i, s: (i, 0, 0)), tab, tab, rot_spec],
        out_specs=pl.BlockSpec((1, seq, SWA_WIDTH), lambda i, s: (i, 0, 0)),
        scratch_shapes=[pltpu.VMEM((SWA_WIDTH // 128, seq, 128), BF16), banded, banded,
                        pltpu.VMEM((SWA_KV_HEADS, seq + 2 * SWA_BLOCK, 256), BF16),
                        pltpu.VMEM((SWA_KV_HEADS, SWA_GROUP * SWA_BLOCK, 3 * SWA_BLOCK), F32),
                        pltpu.VMEM((SWA_KV_HEADS, SWA_GROUP * SWA_BLOCK, 3 * SWA_BLOCK), F32),
                        pltpu.VMEM((SWA_KV_HEADS, SWA_GROUP * SWA_BLOCK, 3 * SWA_BLOCK), BF16),
                        pltpu.VMEM((SWA_BLOCK, 3 * SWA_BLOCK), F32), stat],
    )
    return pl.pallas_call(
        _swa_kernel,
        grid_spec=grid_spec,
        out_shape=jax.ShapeDtypeStruct((b, seq, SWA_WIDTH), F32),
        compiler_params=_params("parallel"),
        name="swa",
    )(sink, a3, cos_t, sin_t, rot)


def _layer_norm(x, g, b):
    mu = jnp.mean(x, axis=-1, keepdims=True)
    d = x - mu
    var = jnp.mean(d * d, axis=-1, keepdims=True)
    return d * lax.rsqrt(var + LN_EPS) * g + b


def _out_ffn_kernel(x_ref, yal_ref, yah_ref, yb_ref, yc_ref, wo_ref,
                    g1_ref, b1_ref, w1_ref, w2_ref, g2_ref, b2_ref, o_ref, acc_ref, x1b_ref, *, alpha):
    subs = [slice(r, r + FFN_SUB) for r in range(0, x_ref.shape[0], FFN_SUB)]
    for rows in subs:
        ya = jnp.concatenate([yal_ref[rows, :], yah_ref[rows, :]], axis=1)
        mix = (_mm(ya.astype(BF16), wo_ref[0:S5_WIDTH, :])
               + _mm(yb_ref[rows, :].astype(BF16), wo_ref[S5_WIDTH:S5_WIDTH + GLA_WIDTH, :])
               + _mm(yc_ref[rows, :].astype(BF16), wo_ref[S5_WIDTH + GLA_WIDTH:D_MODEL, :]))
        x1 = _layer_norm(alpha * x_ref[rows, :] + mix, g1_ref[...], b1_ref[...])
        o_ref[rows, :] = x1
        x1b_ref[rows, :] = x1.astype(BF16)
    for rows in subs:
        for ci in range(D_FF // FF_CHUNK):
            sl = slice(ci * FF_CHUNK, (ci + 1) * FF_CHUNK)
            hid = jnp.maximum(_mm(x1b_ref[rows, :], w1_ref[:, sl]), 0.0)
            part = _mm((hid * hid).astype(BF16), w2_ref[sl, :])
            if ci == 0:
                acc_ref[rows, :] = part
            else:
                acc_ref[rows, :] += part
    for rows in subs:
        o_ref[rows, :] = _layer_norm(alpha * o_ref[rows, :] + acc_ref[rows, :], g2_ref[...], b2_ref[...])


def _out_ffn(x2d, yal2d, yah2d, yb2d, yc2d, wo, g1, b1, w1, w2, g2, b2, alpha):
    t = x2d.shape[0]
    tm = min(FFN_TILE, t)
    row = lambda n: pl.BlockSpec((tm, n), lambda i: (i, 0))
    once = lambda shape: pl.BlockSpec(shape, lambda i: (0, 0), pipeline_mode=pl.Buffered(1))
    return pl.pallas_call(
        functools.partial(_out_ffn_kernel, alpha=alpha),
        grid=(t // tm,),
        in_specs=[row(D_MODEL), row(128), row(128), row(GLA_WIDTH), row(SWA_WIDTH),
                  once((D_MODEL, D_MODEL)), once((1, D_MODEL)), once((1, D_MODEL)),
                  once((D_MODEL, D_FF)), once((D_FF, D_MODEL)),
                  once((1, D_MODEL)), once((1, D_MODEL))],
        out_specs=row(D_MODEL),
        out_shape=jax.ShapeDtypeStruct((t, D_MODEL), F32),
        scratch_shapes=[pltpu.VMEM((tm, D_MODEL), F32), pltpu.VMEM((tm, D_MODEL), BF16)],
        compiler_params=_params("parallel"),
        name="out_ffn",
    )(x2d, yal2d, yah2d, yb2d, yc2d, wo, g1, b1, w1, w2, g2, b2)


def _pack_w_in(w_in):
    points = np.cumsum([S5_WIDTH, GLA_QK, GLA_QK, GLA_WIDTH, GLA_WIDTH, GLA_RANK, GLA_RANK,
                        SWA_WIDTH, SWA_KV]).tolist()
    s5, gq, gk, gv, gr, glf, glb, aq, ak, av = jnp.split(w_in, points, axis=-1)
    pad = jnp.zeros(w_in.shape[:-1] + (128 - 2 * GLA_RANK,), w_in.dtype)
    return jnp.concatenate([s5, gq, gk, gv, gr, glf, glb, pad, aq, ak, av], axis=-1).astype(BF16)


def _pad_gate_w(w_a):
    z = jnp.zeros((128, GLA_QK), F32)
    waf = z.at[0:GLA_RANK].set(w_a[0].astype(F32))
    wab = z.at[GLA_RANK:2 * GLA_RANK].set(w_a[1].astype(F32))
    return waf.astype(BF16), wab.astype(BF16)


def _layer(x2d, bsz, seq, alpha, w_packed, a_re, a_im, log_step, b_re, b_im, c_re, c_im, d_skip,
           w_glu, b_glu, gla_w_a, gla_b_a, gla_ln_g, swa_sink, w_out, ln1_g, ln1_b, w_ff1, w_ff2,
           ln2_g, ln2_b, gla_consts, rope_tabs):
    t = bsz * seq
    row = lambda v: v.astype(F32)[None, :]
    ul, uh, g, a = _inproj(x2d, w_packed)

    mats = _s5_matrices(a_re, a_im, log_step, b_re, b_im, c_re, c_im, d_skip, seq // S5_CHUNK)
    yal, yah = _s5(ul.reshape(bsz, seq, 128), uh.reshape(bsz, seq, 128), mats, w_glu, row(b_glu))

    waf, wab = _pad_gate_w(gla_w_a)
    yb = _gla(g.reshape(bsz, seq, G_COLS), waf, wab, gla_b_a.astype(F32), gla_consts, row(gla_ln_g))
    yc = _swa(a.reshape(bsz, seq, A_COLS), swa_sink.astype(F32), rope_tabs)

    return _out_ffn(x2d, yal.reshape(t, 128), yah.reshape(t, 128), yb.reshape(t, GLA_WIDTH), yc.reshape(t, SWA_WIDTH),
                    w_out, row(ln1_g), row(ln1_b), w_ff1, w_ff2, row(ln2_g), row(ln2_b), alpha)


def kernel(x, w_in, s5_a_re, s5_a_im, s5_log_step, s5_b_re, s5_b_im, s5_c_re, s5_c_im, s5_d, s5_w_glu, s5_b_glu, gla_w_a, gla_b_a, gla_ln_g, swa_sink, w_out, ln1_g, ln1_b, w_ff1, w_ff2, ln2_g, ln2_b):
    bsz, seq, _ = x.shape
    depth = w_in.shape[0]
    alpha = (2 * depth) ** 0.25
    gla_consts = _gla_constants()
    rope_tabs = _rope_tables(seq)
    w_packed = _pack_w_in(w_in)
    s5_w_glu, w_out, w_ff1, w_ff2 = (w.astype(BF16) for w in (s5_w_glu, w_out, w_ff1, w_ff2))
    h = x.reshape(bsz * seq, D_MODEL)
    for l in range(depth):
        h = _layer(h, bsz, seq, alpha, w_packed[l], s5_a_re[l], s5_a_im[l], s5_log_step[l],
                   s5_b_re[l], s5_b_im[l], s5_c_re[l], s5_c_im[l], s5_d[l], s5_w_glu[l], s5_b_glu[l],
                   gla_w_a[l], gla_b_a[l], gla_ln_g[l], swa_sink[l], w_out[l], ln1_g[l], ln1_b[l],
                   w_ff1[l], w_ff2[l], ln2_g[l], ln2_b[l], gla_consts, rope_tabs)
    return h.reshape(bsz, seq, D_MODEL)
```

```python
import functools
import math

import jax
import jax.numpy as jnp
import numpy as np
from jax import lax
from jax.experimental import pallas as pl
from jax.experimental.pallas import tpu as pltpu

F32 = jnp.float32
BF16 = jnp.bfloat16

D_MODEL = 1024
S5_WIDTH = 256
S5_GROUP = 16
S5_GROUPS = 16
S5_STATE = 64
S5_CHUNK = 16
GLA_HEADS = 4
GLA_DK = 32
GLA_DV = 64
GLA_QK = 128
GLA_WIDTH = 256
GLA_RANK = 16
GLA_TAU = 16.0
GLA_CHUNK = 64
SWA_WIDTH = 512
SWA_HEAD_DIM = 64
SWA_HEADS = 8
SWA_KV_HEADS = 2
SWA_GROUP = SWA_HEADS // SWA_KV_HEADS
SWA_KV = 128
SWA_WINDOW = 128
SWA_BLOCK = 128
ROT_DIM = 16
ROPE_THETA = 500000.0
D_FF = 4096
LN_EPS = 1e-5
NEG_BIG = -1e30
LOG2E = 1.4426950408889634

U_COLS = S5_WIDTH
G_COLS = GLA_QK + GLA_QK + GLA_WIDTH + GLA_WIDTH + 128
A_COLS = SWA_WIDTH + SWA_KV + SWA_KV
IN_COLS = U_COLS + G_COLS + A_COLS

VMEM_LIMIT = 56 * 1024 * 1024
ROW_TILE = 1024
FF_CHUNK = 512
FFN_TILE = 1024
FFN_SUB = 512
SWA_ROWS = 64

_NT = (((1,), (1,)), ((), ()))
_TN = (((0,), (0,)), ((), ()))


def _mm(a, b):
    return jnp.dot(a, b, preferred_element_type=F32)


def _params(*sem):
    return pltpu.CompilerParams(dimension_semantics=sem, vmem_limit_bytes=VMEM_LIMIT)


def _resident(shape):
    nd = len(shape)
    return pl.BlockSpec(shape, lambda *_: (0,) * nd)


def _inproj_kernel(x_ref, w_ref, ul_ref, uh_ref, g_ref, a_ref):
    x = x_ref[...].astype(BF16)
    u = _mm(x, w_ref[:, 0:U_COLS])
    ul_ref[...] = u[:, :128]
    uh_ref[...] = u[:, 128:]
    g_ref[...] = _mm(x, w_ref[:, U_COLS:U_COLS + G_COLS])
    a_ref[...] = _mm(x, w_ref[:, U_COLS + G_COLS:IN_COLS])


def _inproj(x2d, w_packed):
    t = x2d.shape[0]
    tm = min(ROW_TILE, t)
    row = lambda n: pl.BlockSpec((tm, n), lambda i: (i, 0))
    return pl.pallas_call(
        _inproj_kernel,
        grid=(t // tm,),
        in_specs=[row(D_MODEL), _resident((D_MODEL, IN_COLS))],
        out_specs=[row(128), row(128), row(G_COLS), row(A_COLS)],
        out_shape=[jax.ShapeDtypeStruct((t, 128), F32), jax.ShapeDtypeStruct((t, 128), F32),
                   jax.ShapeDtypeStruct((t, G_COLS), F32), jax.ShapeDtypeStruct((t, A_COLS), F32)],
        compiler_params=_params("parallel"),
        name="inproj",
    )(x2d, w_packed)


def _s5_matrices(a_re, a_im, log_step, b_re, b_im, c_re, c_im, d_skip, nch):
    c, ng, w = S5_CHUNK, S5_GROUPS, S5_CHUNK * S5_GROUP
    ar, ai = a_re.astype(F32), a_im.astype(F32)
    step = jnp.exp(log_step.astype(F32))
    lr, li = ar * step, ai * step

    def lam_pow(z, k):
        kk = jnp.asarray(k, F32)[None, :, None, None]
        mag = jnp.exp(kk * lr[z][:, None, None, :])
        ang = kk * li[z][:, None, None, :]
        return mag * jnp.cos(ang), mag * jnp.sin(ang)

    def cmul(xr, xi, yr, yi):
        return xr * yr - xi * yi, xr * yi + xi * yr

    l1r, l1i = jnp.exp(lr) * jnp.cos(li), jnp.exp(lr) * jnp.sin(li)
    nr, ni = l1r - 1.0, l1i
    den = ar * ar + ai * ai
    qr, qi = (nr * ar + ni * ai) / den, (ni * ar - nr * ai) / den
    bt_r = jnp.swapaxes(b_re.astype(F32), -1, -2)
    bt_i = jnp.swapaxes(b_im.astype(F32), -1, -2)
    bbr, bbi = cmul(qr[:, :, None, :], qi[:, :, None, :], bt_r, bt_i)
    cr, ci = c_re.astype(F32), c_im.astype(F32)

    idx = np.arange(c)
    flat = lambda t: t.reshape(ng, w, S5_STATE)

    def with_c(z, k):
        pr, pi = lam_pow(z, k)
        xr, xi = cmul(pr, pi, cr[z][:, None], ci[z][:, None])
        return flat(xr), flat(xi)

    def with_b(z, k):
        pr, pi = lam_pow(z, k)
        yr, yi = cmul(pr, pi, bbr[z][:, None], bbi[z][:, None])
        return flat(yr), flat(yi)

    def re_prod(x, y):
        return jnp.einsum('gmk,gnk->gmn', jnp.concatenate([x[0], -x[1]], axis=-1),
                          jnp.concatenate([y[0], y[1]], axis=-1), precision=lax.Precision.HIGHEST)

    ri = np.arange(w)[:, None] // S5_GROUP
    cj = np.arange(w)[None, :] // S5_GROUP
    tt_mat = (jnp.where(ri >= cj, re_prod(with_c(0, idx), with_b(0, -idx)), 0.0)
              + jnp.where(cj >= ri, re_prod(with_c(1, -idx), with_b(1, idx)), 0.0))

    sf, sb = with_b(0, c - 1 - idx), with_b(1, idx)
    s_mat = jnp.concatenate([sf[0], sb[0], sf[1], sb[1]], axis=-1)
    of, ob = with_c(0, idx + 1), with_c(1, c - idx)
    ot_mat = jnp.concatenate([of[0], ob[0], -of[1], -ob[1]], axis=-1)

    pcr = [lam_pow(z, [c]) for z in range(2)]
    a_r = jnp.concatenate([pcr[0][0][:, 0, 0, :], pcr[1][0][:, 0, 0, :]], axis=-1)
    a_i = jnp.concatenate([pcr[0][1][:, 0, 0, :], pcr[1][1][:, 0, 0, :]], axis=-1)
    d_col = jnp.broadcast_to(jnp.tile(d_skip.astype(F32), (1, c))[:, :, None], (ng, w, nch))
    return tt_mat.astype(BF16), s_mat.astype(BF16), ot_mat.astype(BF16), a_r, a_i, d_col


def _gelu_tanh(x):
    return 0.5 * x * (1.0 + jnp.tanh(math.sqrt(2.0 / math.pi) * (x + 0.044715 * (x * x * x))))


def _sigmoid(x):
    return 1.0 / (1.0 + jnp.exp(-x))


def _s5_kernel(ul_ref, uh_ref, tt_ref, s_ref, ot_ref, ar_ref, ai_ref, dcol_ref, wglu_ref, bglu_ref,
               yl_ref, yh_ref, d_scr, cr_scr, ci_scr, efr_scr, efi_scr, ebr_scr, ebi_scr, zt_scr):
    c, ng = S5_CHUNK, S5_GROUPS
    nch = ul_ref.shape[1] // c
    for j in range(c):
        tok = pl.ds(j, nch, stride=c)
        uj = jnp.concatenate([ul_ref[0, tok, :], uh_ref[0, tok, :]], axis=1)
        d_scr[:, j * S5_GROUP:(j + 1) * S5_GROUP, :] = uj.T.reshape(ng, S5_GROUP, nch)
    for g in range(ng):
        dg = d_scr[g].astype(BF16)
        contrib = lax.dot_general(dg, s_ref[g], _TN, preferred_element_type=F32)
        grp = pl.ds(g, nch, stride=ng)
        cr_scr[grp, :] = contrib[:, :128]
        ci_scr[grp, :] = contrib[:, 128:]

    ar, ai = ar_ref[...], ai_ref[...]
    fwd = lax.broadcasted_iota(jnp.int32, (ng, 128), 1) < S5_STATE

    def step(k, carry):
        sre, sim = carry
        up = pl.ds(pl.multiple_of(k * ng, ng), ng)
        dn = pl.ds(pl.multiple_of((nch - 1 - k) * ng, ng), ng)
        efr_scr[up, :] = sre
        efi_scr[up, :] = sim
        ebr_scr[dn, :] = sre
        ebi_scr[dn, :] = sim
        cre = jnp.where(fwd, cr_scr[up, :], cr_scr[dn, :])
        cim = jnp.where(fwd, ci_scr[up, :], ci_scr[dn, :])
        return ar * sre - ai * sim + cre, ar * sim + ai * sre + cim

    zero = jnp.zeros((ng, 128), F32)
    lax.fori_loop(0, nch, step, (zero, zero), unroll=8)

    fwd_all = lax.broadcasted_iota(jnp.int32, (nch, 128), 1) < S5_STATE
    for g in range(ng):
        dg32 = d_scr[g]
        grp = pl.ds(g, nch, stride=ng)
        enter = jnp.concatenate([jnp.where(fwd_all, efr_scr[grp, :], ebr_scr[grp, :]),
                                 jnp.where(fwd_all, efi_scr[grp, :], ebi_scr[grp, :])],
                                axis=1).astype(BF16)
        yt = (_mm(tt_ref[g], dg32.astype(BF16))
              + lax.dot_general(ot_ref[g], enter, _NT, preferred_element_type=F32)
              + dcol_ref[g] * dg32)
        zt_scr[:, g * S5_GROUP:(g + 1) * S5_GROUP, :] = _gelu_tanh(yt).reshape(c, S5_GROUP, nch)
    for i in range(c):
        zi = zt_scr[i].astype(BF16)
        glu = lax.dot_general(zi, wglu_ref[...], _TN, preferred_element_type=F32) + bglu_ref[...]
        ya = glu[:, :S5_WIDTH] * _sigmoid(glu[:, S5_WIDTH:])
        tok = pl.ds(i, nch, stride=c)
        yl_ref[0, tok, :] = ya[:, :128]
        yh_ref[0, tok, :] = ya[:, 128:]


def _s5(ul3, uh3, mats, wglu, bglu):
    b, seq, _ = ul3.shape
    nch = seq // S5_CHUNK
    tt_mat, s_mat, ot_mat, a_r, a_i, d_col = mats
    w = S5_CHUNK * S5_GROUP
    half = pl.BlockSpec((1, seq, 128), lambda i: (i, 0, 0))
    state_scr = pltpu.VMEM((S5_GROUPS * nch, 2 * S5_STATE), F32)
    return pl.pallas_call(
        _s5_kernel,
        grid=(b,),
        in_specs=[half, half,
                  _resident((S5_GROUPS, w, w)), _resident((S5_GROUPS, w, 4 * S5_STATE)),
                  _resident((S5_GROUPS, w, 4 * S5_STATE)),
                  _resident((S5_GROUPS, 128)), _resident((S5_GROUPS, 128)),
                  _resident((S5_GROUPS, w, nch)),
                  _resident((S5_WIDTH, 2 * S5_WIDTH)), _resident((1, 2 * S5_WIDTH))],
        out_specs=[half, half],
        out_shape=[jax.ShapeDtypeStruct((b, seq, 128), F32)] * 2,
        scratch_shapes=[pltpu.VMEM((S5_GROUPS, w, nch), F32)] + [state_scr] * 6
                       + [pltpu.VMEM((S5_CHUNK, w, nch), F32)],
        compiler_params=_params("parallel"),
        name="s5",
    )(ul3, uh3, tt_mat, s_mat, ot_mat, a_r, a_i, d_col, wglu, bglu)


def _gla_constants():
    r = np.arange(256)
    same = (r[:, None] // GLA_CHUNK) == (r[None, :] // GLA_CHUNK)
    prefix = same & (r[None, :] <= r[:, None])
    suffix = same & (r[None, :] >= r[:, None])
    avg = ((r[:, None] // GLA_DV) == (r[None, :] // GLA_DV)) / float(GLA_DV)
    as_bf = lambda m: jnp.asarray(m.astype(np.float32), BF16)
    return as_bf(prefix), as_bf(suffix), as_bf(same), as_bf(avg)


def _log_sigmoid(x):
    return jnp.minimum(x, 0.0) - jnp.log(1.0 + jnp.exp(-jnp.abs(x)))


def _split_mm(a, b):
    hi = a.astype(BF16)
    lo = (a - hi.astype(F32)).astype(BF16)
    return _mm(hi, b) + _mm(lo, b)


def _chunk_sums(mat, la):
    hi = la.astype(BF16)
    lo = (la - hi.astype(F32)).astype(BF16)
    out = []
    for i in range(la.shape[0] // 256):
        sl = slice(i * 256, (i + 1) * 256)
        out.append(_mm(mat, hi[sl]) + _mm(mat, lo[sl]))
    return jnp.concatenate(out, axis=0)


def _gla_kernel(g_ref, waf_ref, wab_ref, ba_ref, pre_ref, suf_ref, tot_ref, avg_ref, lng_ref, o_ref,
                q_s, kin_s, kst_s, v_s, totf_s, totb_s, o_s, kv_s, st_s):
    seq = g_ref.shape[1]
    nchunk = seq // GLA_CHUNK
    c = GLA_CHUNK
    q = g_ref[0, :, 0:128] * (GLA_DK ** -0.5)
    k = g_ref[0, :, 128:256]
    lfb = g_ref[0, :, 768:896].astype(BF16)
    la_f = _log_sigmoid(_mm(lfb, waf_ref[...]) + ba_ref[0:1, :]) * (1.0 / GLA_TAU)
    la_b = _log_sigmoid(_mm(lfb, wab_ref[...]) + ba_ref[1:2, :]) * (1.0 / GLA_TAU)

    bf = _chunk_sums(pre_ref[...], la_f)
    tf = _chunk_sums(tot_ref[...], la_f)
    q_s[:, 0:128] = (q * jnp.exp(bf)).astype(BF16)
    kin_s[:, 0:128] = (k * jnp.exp(-bf)).astype(BF16)
    kst_s[:, 0:128] = (k * jnp.exp(tf - bf)).astype(BF16)
    totf_s[...] = tf
    bs = _chunk_sums(suf_ref[...], la_b)
    tb = _chunk_sums(tot_ref[...], la_b)
    q_s[:, 128:256] = (q * jnp.exp(bs)).astype(BF16)
    kin_s[:, 128:256] = (k * jnp.exp(-bs)).astype(BF16)
    kst_s[:, 128:256] = (k * jnp.exp(tb - bs)).astype(BF16)
    totb_s[...] = tb
    v_s[...] = g_ref[0, :, 256:512].astype(BF16)

    r256 = lax.broadcasted_iota(jnp.int32, (4 * c, 256), 0)
    l256 = lax.broadcasted_iota(jnp.int32, (4 * c, 256), 1)
    head_k = (r256 // c) == ((l256 % 128) // GLA_DK)
    head_v = (r256 // c) == (l256 // GLA_DV)
    qi = lax.broadcasted_iota(jnp.int32, (c, 4 * c), 0)
    kj = lax.broadcasted_iota(jnp.int32, (c, 4 * c), 1) % c
    zero_bf = jnp.zeros((), BF16)

    def local(n, carry):
        r0 = pl.multiple_of(n * c, c)
        qq = q_s[pl.ds(r0, c), :]
        vv = v_s[pl.ds(r0, c), :]
        kexp = jnp.where(head_k, jnp.concatenate([kin_s[pl.ds(r0, c), :]] * 4, axis=0), zero_bf)
        vexp = jnp.where(head_v, jnp.concatenate([vv] * 4, axis=0), zero_bf)
        sc_f = lax.dot_general(qq[:, 0:128], kexp[:, 0:128], _NT, preferred_element_type=F32)
        sc_b = lax.dot_general(qq[:, 128:256], kexp[:, 128:256], _NT, preferred_element_type=F32)
        pm = jnp.where(kj <= qi, sc_f, sc_b).astype(BF16)
        o_s[pl.ds(r0, c), :] = _mm(pm, vexp)
        kv_t = lax.dot_general(vv, kst_s[pl.ds(r0, c), :], _TN, preferred_element_type=F32)
        kv_s[n] = jnp.where(head_k, kv_t, 0.0)
        return carry

    lax.fori_loop(0, nchunk, local, 0, unroll=4)

    def carry_states(n, carry):
        st_f, st_b = carry
        m = nchunk - 1 - n
        st_s[n, :, 0:128] = st_f.astype(BF16)
        st_s[m, :, 128:256] = st_b.astype(BF16)
        st_f = jnp.exp(totf_s[pl.ds(n * c, 1), :]) * st_f + kv_s[n, :, 0:128]
        st_b = jnp.exp(totb_s[pl.ds(m * c, 1), :]) * st_b + kv_s[m, :, 128:256]
        return st_f, st_b

    zero = jnp.zeros((GLA_WIDTH, GLA_QK), F32)
    lax.fori_loop(0, nchunk, carry_states, (zero, zero))

    def cross(n, carry):
        r0 = pl.multiple_of(n * c, c)
        o_s[pl.ds(r0, c), :] += lax.dot_general(q_s[pl.ds(r0, c), :], st_s[n], _NT, preferred_element_type=F32)
        return carry

    lax.fori_loop(0, nchunk, cross, 0, unroll=8)

    o = o_s[...]
    avg = avg_ref[...]
    mu = _split_mm(o, avg)
    dlt = o - mu
    var = _split_mm(dlt * dlt, avg)
    on = dlt * lax.rsqrt(var + LN_EPS)
    r = g_ref[0, :, 512:768]
    o_ref[0] = on * lng_ref[...] * (r * _sigmoid(r))


def _gla(g3, waf, wab, ba, consts, ln_g):
    b, seq, _ = g3.shape
    pre, suf, tot, avg = consts
    nchunk = seq // GLA_CHUNK
    both = pltpu.VMEM((seq, 2 * GLA_QK), BF16)
    return pl.pallas_call(
        _gla_kernel,
        grid=(b,),
        in_specs=[pl.BlockSpec((1, seq, G_COLS), lambda i: (i, 0, 0)),
                  _resident((128, 128)), _resident((128, 128)), _resident((2, 128)),
                  _resident((256, 256)), _resident((256, 256)), _resident((256, 256)),
                  _resident((256, 256)), _resident((1, 256))],
        out_specs=pl.BlockSpec((1, seq, GLA_WIDTH), lambda i: (i, 0, 0)),
        out_shape=jax.ShapeDtypeStruct((b, seq, GLA_WIDTH), F32),
        scratch_shapes=[both, both, both, pltpu.VMEM((seq, GLA_WIDTH), BF16),
                        pltpu.VMEM((seq, GLA_QK), F32), pltpu.VMEM((seq, GLA_QK), F32),
                        pltpu.VMEM((seq, GLA_WIDTH), F32),
                        pltpu.VMEM((nchunk, GLA_WIDTH, 2 * GLA_QK), F32),
                        pltpu.VMEM((nchunk, GLA_WIDTH, 2 * GLA_QK), BF16)],
        compiler_params=_params("parallel"),
        name="gla",
    )(g3, waf, wab, ba, pre, suf, tot, avg, ln_g)


def _rope_tables(seq):
    pos = jnp.arange(seq, dtype=F32)
    inv_freq = ROPE_THETA ** (-jnp.arange(0, ROT_DIM, 2, dtype=F32) / ROT_DIM)
    ang = pos[:, None] * inv_freq[None, :]
    cos, sin = jnp.cos(ang), jnp.sin(ang)
    half = ROT_DIM // 2
    pad = jnp.zeros((seq, SWA_HEAD_DIM - ROT_DIM), F32)
    c_head = jnp.concatenate([cos, cos, pad + 1.0], axis=1)
    s_head = jnp.concatenate([sin, sin, pad], axis=1)
    two = lambda t: jnp.concatenate([t, t], axis=1)
    rot = np.zeros((128, 128), np.float32)
    for base in (0, SWA_HEAD_DIM):
        for d in range(half):
            rot[base + d + half, base + d] = -1.0
            rot[base + d, base + d + half] = 1.0
    return two(c_head), two(s_head), jnp.asarray(rot, BF16)


def _swa_kernel(sink_ref, a_ref, cos_ref, sin_ref, rot_ref, o_ref,
                q_s, klo_s, khi_s, vd_s, sa_s, sb_s, pa_s, pb_s, bias_s, ma_s, mb_s):
    seq = a_ref.shape[1]
    blk = SWA_BLOCK
    hd = SWA_HEAD_DIM
    pairs = SWA_GROUP // 2
    cosv, sinv, rot = cos_ref[...], sin_ref[...], rot_ref[...]

    def rope(x):
        return x * cosv + _split_mm(x, rot) * sinv

    for slab in range(SWA_WIDTH // 128):
        q_s[slab] = (rope(a_ref[0, :, slab * 128:(slab + 1) * 128]) * (hd ** -0.5 * LOG2E)).astype(BF16)
    kr = rope(a_ref[0, :, SWA_WIDTH:SWA_WIDTH + SWA_KV])
    vr = a_ref[0, :, SWA_WIDTH + SWA_KV:A_COLS]
    ksw = pltpu.roll(kr, hd, 1)
    vsw = pltpu.roll(vr, hd, 1)
    low = lax.broadcasted_iota(jnp.int32, (seq, 128), 1) < hd
    body = slice(blk, blk + seq)
    zpad = jnp.zeros((blk, 128), BF16)
    for kh in range(SWA_KV_HEADS):
        for ref in (klo_s, khi_s):
            ref[kh, 0:blk, :] = zpad
            ref[kh, blk + seq:2 * blk + seq, :] = zpad
        vd_s[kh, 0:blk, 0:128] = zpad
        vd_s[kh, blk + seq:2 * blk + seq, 0:128] = zpad
        vd_s[kh, :, 128:256] = jnp.ones((seq + 2 * blk, 128), BF16)
    klo_s[0, body, :] = jnp.where(low, kr, 0.0).astype(BF16)
    khi_s[0, body, :] = jnp.where(low, 0.0, ksw).astype(BF16)
    klo_s[1, body, :] = jnp.where(low, ksw, 0.0).astype(BF16)
    khi_s[1, body, :] = jnp.where(low, 0.0, kr).astype(BF16)
    vd_s[0, body, 0:128] = jnp.where(low, vr, vsw).astype(BF16)
    vd_s[1, body, 0:128] = jnp.where(low, vsw, vr).astype(BF16)

    qi = lax.broadcasted_iota(jnp.int32, (blk, 3 * blk), 0)
    kk = lax.broadcasted_iota(jnp.int32, (blk, 3 * blk), 1)
    band = jnp.abs(qi + blk - kk) <= SWA_WINDOW
    low_o = lax.broadcasted_iota(jnp.int32, (blk, 128), 1) < hd
    rows = SWA_GROUP * blk
    chunks = [(kh, rb) for kh in range(SWA_KV_HEADS) for rb in range(rows // SWA_ROWS)]

    def scores(n, s_s):
        r0 = pl.multiple_of(n * blk, blk)
        for kh in range(SWA_KV_HEADS):
            kcat = jnp.concatenate([klo_s[kh, pl.ds(r0, 3 * blk), :], khi_s[kh, pl.ds(r0, 3 * blk), :]], axis=0)
            for pr in range(pairs):
                qq = q_s[kh * pairs + pr, pl.ds(r0, blk), :]
                s2 = lax.dot_general(qq, kcat, _NT, preferred_element_type=F32)
                s_s[kh, (2 * pr) * blk:(2 * pr + 1) * blk, :] = s2[:, :3 * blk]
                s_s[kh, (2 * pr + 1) * blk:(2 * pr + 2) * blk, :] = s2[:, 3 * blk:]

    nblk = seq // blk

    def softmax_block(n, s_s, p_s, m_s):
        kpos = kk + (n - 1) * blk
        bias_s[...] = jnp.where(band & (kpos >= 0) & (kpos < seq), 0.0, NEG_BIG)
        for kh, rb in chunks:
            rs = slice(rb * SWA_ROWS, (rb + 1) * SWA_ROWS)
            bs = slice((rb * SWA_ROWS) % blk, (rb * SWA_ROWS) % blk + SWA_ROWS)
            sink2 = sink_ref[kh * SWA_GROUP + (rb * SWA_ROWS) // blk] * LOG2E
            lo = s_s[kh, rs, 0:blk] + bias_s[bs, 0:blk]
            hi = s_s[kh, rs, 2 * blk:3 * blk] + bias_s[bs, 2 * blk:3 * blk]
            s_s[kh, rs, 0:blk] = lo
            s_s[kh, rs, 2 * blk:3 * blk] = hi
            top = jnp.maximum(jnp.maximum(lo, s_s[kh, rs, blk:2 * blk]), hi)
            m_s[kh, rs, :] = jnp.broadcast_to(jnp.maximum(jnp.max(top, axis=1, keepdims=True), sink2), (SWA_ROWS, 128))
        for kh, rb in chunks:
            rs = slice(rb * SWA_ROWS, (rb + 1) * SWA_ROWS)
            p = jnp.exp2(s_s[kh, rs, :] - jnp.concatenate([m_s[kh, rs, :]] * 3, axis=1))
            p_s[kh, rs, :] = p.astype(BF16)

    def output_block(n, p_s, m_s):
        r0 = pl.multiple_of(n * blk, blk)
        for kh in range(SWA_KV_HEADS):
            o3 = _mm(p_s[kh], vd_s[kh, pl.ds(r0, 3 * blk), :])
            sink2 = jnp.concatenate([jnp.full((blk, 128), sink_ref[kh * SWA_GROUP + g] * LOG2E, F32)
                                     for g in range(SWA_GROUP)], axis=0)
            o2 = o3[:, 0:128] / (o3[:, 128:256] + jnp.exp2(sink2 - m_s[kh]))
            o_ref[0, pl.ds(r0, blk), kh * SWA_GROUP * hd:(kh + 1) * SWA_GROUP * hd] = jnp.concatenate(
                [jnp.where(low_o, o2[(2 * pr) * blk:(2 * pr + 1) * blk], o2[(2 * pr + 1) * blk:(2 * pr + 2) * blk])
                 for pr in range(pairs)], axis=1)

    def body_fn(t, carry):
        n = 2 * t + 1
        output_block(n - 1, pa_s, ma_s)
        scores(n + 1, sa_s)
        softmax_block(n, sb_s, pb_s, mb_s)
        output_block(n, pb_s, mb_s)
        scores(jnp.minimum(n + 2, nblk - 1), sb_s)
        softmax_block(n + 1, sa_s, pa_s, ma_s)
        return carry

    scores(0, sa_s)
    scores(1, sb_s)
    softmax_block(0, sa_s, pa_s, ma_s)
    lax.fori_loop(0, (nblk - 2) // 2, body_fn, 0)
    softmax_block(nblk - 1, sb_s, pb_s, mb_s)
    output_block(nblk - 2, pa_s, ma_s)
    output_block(nblk - 1, pb_s, mb_s)


def _swa(a3, sink, tables):
    b, seq, _ = a3.shape
    cos_t, sin_t, rot = tables
    tab = pl.BlockSpec((seq, 128), lambda i, s: (0, 0))
    rot_spec = pl.BlockSpec((128, 128), lambda i, s: (0, 0))
    banded = pltpu.VMEM((SWA_KV_HEADS, seq + 2 * SWA_BLOCK, 128), BF16)
    stat = pltpu.VMEM((SWA_KV_HEADS, SWA_GROUP * SWA_BLOCK, 128), F32)
    grid_spec = pltpu.PrefetchScalarGridSpec(
        num_scalar_prefetch=1,
        grid=(b,),
        in_specs=[pl.BlockSpec((1, seq, A_COLS), lambda i, s: (i, 0, 0)), tab, tab, rot_spec],
        out_specs=pl.BlockSpec((1, seq, SWA_WIDTH), lambda i, s: (i, 0, 0)),
        scratch_shapes=[pltpu.VMEM((SWA_WIDTH // 128, seq, 128), BF16), banded, banded,
                        pltpu.VMEM((SWA_KV_HEADS, seq + 2 * SWA_BLOCK, 256), BF16),
                        pltpu.VMEM((SWA_KV_HEADS, SWA_GROUP * SWA_BLOCK, 3 * SWA_BLOCK), F32),
                        pltpu.VMEM((SWA_KV_HEADS, SWA_GROUP * SWA_BLOCK, 3 * SWA_BLOCK), F32),
                        pltpu.VMEM((SWA_KV_HEADS, SWA_GROUP * SWA_BLOCK, 3 * SWA_BLOCK), BF16),
                        pltpu.VMEM((SWA_KV_HEADS, SWA_GROUP * SWA_BLOCK, 3 * SWA_BLOCK), BF16),
                        pltpu.VMEM((SWA_BLOCK, 3 * SWA_BLOCK), F32), stat, stat],
    )
    return pl.pallas_call(
        _swa_kernel,
        grid_spec=grid_spec,
        out_shape=jax.ShapeDtypeStruct((b, seq, SWA_WIDTH), F32),
        compiler_params=_params("parallel"),
        name="swa",
    )(sink, a3, cos_t, sin_t, rot)


def _layer_norm(x, g, b):
    mu = jnp.mean(x, axis=-1, keepdims=True)
    d = x - mu
    var = jnp.mean(d * d, axis=-1, keepdims=True)
    return d * lax.rsqrt(var + LN_EPS) * g + b


def _out_ffn_kernel(x_ref, yal_ref, yah_ref, yb_ref, yc_ref, wo_ref,
                    g1_ref, b1_ref, w1_ref, w2_ref, g2_ref, b2_ref, o_ref, acc_ref, x1b_ref, *, alpha):
    subs = [slice(r, r + FFN_SUB) for r in range(0, x_ref.shape[0], FFN_SUB)]
    for rows in subs:
        ya = jnp.concatenate([yal_ref[rows, :], yah_ref[rows, :]], axis=1)
        mix = (_mm(ya.astype(BF16), wo_ref[0:S5_WIDTH, :])
               + _mm(yb_ref[rows, :].astype(BF16), wo_ref[S5_WIDTH:S5_WIDTH + GLA_WIDTH, :])
               + _mm(yc_ref[rows, :].astype(BF16), wo_ref[S5_WIDTH + GLA_WIDTH:D_MODEL, :]))
        x1 = _layer_norm(alpha * x_ref[rows, :] + mix, g1_ref[...], b1_ref[...])
        o_ref[rows, :] = x1
        x1b_ref[rows, :] = x1.astype(BF16)
    for rows in subs:
        for ci in range(D_FF // FF_CHUNK):
            sl = slice(ci * FF_CHUNK, (ci + 1) * FF_CHUNK)
            hid = jnp.maximum(_mm(x1b_ref[rows, :], w1_ref[:, sl]), 0.0)
            part = _mm((hid * hid).astype(BF16), w2_ref[sl, :])
            if ci == 0:
                acc_ref[rows, :] = part
            else:
                acc_ref[rows, :] += part
    for rows in subs:
        o_ref[rows, :] = _layer_norm(alpha * o_ref[rows, :] + acc_ref[rows, :], g2_ref[...], b2_ref[...])


def _out_ffn(x2d, yal2d, yah2d, yb2d, yc2d, wo, g1, b1, w1, w2, g2, b2, alpha):
    t = x2d.shape[0]
    tm = min(FFN_TILE, t)
    row = lambda n: pl.BlockSpec((tm, n), lambda i: (i, 0))
    once = lambda shape: pl.BlockSpec(shape, lambda i: (0, 0), pipeline_mode=pl.Buffered(1))
    return pl.pallas_call(
        functools.partial(_out_ffn_kernel, alpha=alpha),
        grid=(t // tm,),
        in_specs=[row(D_MODEL), row(128), row(128), row(GLA_WIDTH), row(SWA_WIDTH),
                  once((D_MODEL, D_MODEL)), once((1, D_MODEL)), once((1, D_MODEL)),
                  once((D_MODEL, D_FF)), once((D_FF, D_MODEL)),
                  once((1, D_MODEL)), once((1, D_MODEL))],
        out_specs=row(D_MODEL),
        out_shape=jax.ShapeDtypeStruct((t, D_MODEL), F32),
        scratch_shapes=[pltpu.VMEM((tm, D_MODEL), F32), pltpu.VMEM((tm, D_MODEL), BF16)],
        compiler_params=_params("parallel"),
        name="out_ffn",
    )(x2d, yal2d, yah2d, yb2d, yc2d, wo, g1, b1, w1, w2, g2, b2)


def _pack_w_in(w_in):
    points = np.cumsum([S5_WIDTH, GLA_QK, GLA_QK, GLA_WIDTH, GLA_WIDTH, GLA_RANK, GLA_RANK,
                        SWA_WIDTH, SWA_KV]).tolist()
    s5, gq, gk, gv, gr, glf, glb, aq, ak, av = jnp.split(w_in, points, axis=-1)
    pad = jnp.zeros(w_in.shape[:-1] + (128 - 2 * GLA_RANK,), w_in.dtype)
    return jnp.concatenate([s5, gq, gk, gv, gr, glf, glb, pad, aq, ak, av], axis=-1).astype(BF16)


def _pad_gate_w(w_a):
    z = jnp.zeros((128, GLA_QK), F32)
    waf = z.at[0:GLA_RANK].set(w_a[0].astype(F32))
    wab = z.at[GLA_RANK:2 * GLA_RANK].set(w_a[1].astype(F32))
    return waf.astype(BF16), wab.astype(BF16)


def _layer(x2d, bsz, seq, alpha, w_packed, a_re, a_im, log_step, b_re, b_im, c_re, c_im, d_skip,
           w_glu, b_glu, gla_w_a, gla_b_a, gla_ln_g, swa_sink, w_out, ln1_g, ln1_b, w_ff1, w_ff2,
           ln2_g, ln2_b, gla_consts, rope_tabs):
    t = bsz * seq
    row = lambda v: v.astype(F32)[None, :]
    ul, uh, g, a = _inproj(x2d, w_packed)

    mats = _s5_matrices(a_re, a_im, log_step, b_re, b_im, c_re, c_im, d_skip, seq // S5_CHUNK)
    yal, yah = _s5(ul.reshape(bsz, seq, 128), uh.reshape(bsz, seq, 128), mats, w_glu, row(b_glu))

    waf, wab = _pad_gate_w(gla_w_a)
    yb = _gla(g.reshape(bsz, seq, G_COLS), waf, wab, gla_b_a.astype(F32), gla_consts, row(gla_ln_g))
    yc = _swa(a.reshape(bsz, seq, A_COLS), swa_sink.astype(F32), rope_tabs)

    return _out_ffn(x2d, yal.reshape(t, 128), yah.reshape(t, 128), yb.reshape(t, GLA_WIDTH), yc.reshape(t, SWA_WIDTH),
                    w_out, row(ln1_g), row(ln1_b), w_ff1, w_ff2, row(ln2_g), row(ln2_b), alpha)


def kernel(x, w_in, s5_a_re, s5_a_im, s5_log_step, s5_b_re, s5_b_im, s5_c_re, s5_c_im, s5_d, s5_w_glu, s5_b_glu, gla_w_a, gla_b_a, gla_ln_g, swa_sink, w_out, ln1_g, ln1_b, w_ff1, w_ff2, ln2_g, ln2_b):
    bsz, seq, _ = x.shape
    depth = w_in.shape[0]
    alpha = (2 * depth) ** 0.25
    gla_consts = _gla_constants()
    rope_tabs = _rope_tables(seq)
    w_packed = _pack_w_in(w_in)
    s5_w_glu, w_out, w_ff1, w_ff2 = (w.astype(BF16) for w in (s5_w_glu, w_out, w_ff1, w_ff2))
    h = x.reshape(bsz * seq, D_MODEL)
    for l in range(depth):
        h = _layer(h, bsz, seq, alpha, w_packed[l], s5_a_re[l], s5_a_im[l], s5_log_step[l],
                   s5_b_re[l], s5_b_im[l], s5_c_re[l], s5_c_im[l], s5_d[l], s5_w_glu[l], s5_b_glu[l],
                   gla_w_a[l], gla_b_a[l], gla_ln_g[l], swa_sink[l], w_out[l], ln1_g[l], ln1_b[l],
                   w_ff1[l], w_ff2[l], ln2_g[l], ln2_b[l], gla_consts, rope_tabs)
    return h.reshape(bsz, seq, D_MODEL)
```

```python
import functools
import math

import jax
import jax.numpy as jnp
import numpy as np
from jax import lax
from jax.experimental import pallas as pl
from jax.experimental.pallas import tpu as pltpu

F32 = jnp.float32
BF16 = jnp.bfloat16

D_MODEL = 1024
S5_WIDTH = 256
S5_GROUP = 16
S5_GROUPS = 16
S5_STATE = 64
S5_CHUNK = 16
GLA_HEADS = 4
GLA_DK = 32
GLA_DV = 64
GLA_QK = 128
GLA_WIDTH = 256
GLA_RANK = 16
GLA_TAU = 16.0
GLA_CHUNK = 64
GLA_GROUP = 16
SWA_WIDTH = 512
SWA_HEAD_DIM = 64
SWA_HEADS = 8
SWA_KV_HEADS = 2
SWA_GROUP = SWA_HEADS // SWA_KV_HEADS
SWA_KV = 128
SWA_WINDOW = 128
SWA_BLOCK = 128
ROT_DIM = 16
ROPE_THETA = 500000.0
D_FF = 4096
LN_EPS = 1e-5
NEG_BIG = -1e30
LOG2E = 1.4426950408889634

U_COLS = S5_WIDTH
G_COLS = GLA_QK + GLA_QK + GLA_WIDTH + GLA_WIDTH + 128
A_COLS = SWA_WIDTH + SWA_KV + SWA_KV
IN_COLS = U_COLS + G_COLS + A_COLS

VMEM_LIMIT = 56 * 1024 * 1024
ROW_TILE = 1024
FF_CHUNK = 512
FFN_TILE = 1024
FFN_SUB = 512
SWA_ROWS = 64

_NT = (((1,), (1,)), ((), ()))
_TN = (((0,), (0,)), ((), ()))


def _mm(a, b):
    return jnp.dot(a, b, preferred_element_type=F32)


def _params(*sem):
    return pltpu.CompilerParams(dimension_semantics=sem, vmem_limit_bytes=VMEM_LIMIT)


def _resident(shape):
    nd = len(shape)
    return pl.BlockSpec(shape, lambda *_: (0,) * nd)


def _inproj_kernel(x_ref, w_ref, ul_ref, uh_ref, g_ref, a_ref):
    x = x_ref[...].astype(BF16)
    u = _mm(x, w_ref[:, 0:U_COLS])
    ul_ref[...] = u[:, :128]
    uh_ref[...] = u[:, 128:]
    g_ref[...] = _mm(x, w_ref[:, U_COLS:U_COLS + G_COLS])
    a_ref[...] = _mm(x, w_ref[:, U_COLS + G_COLS:IN_COLS])


def _inproj(x2d, w_packed):
    t = x2d.shape[0]
    tm = min(ROW_TILE, t)
    row = lambda n: pl.BlockSpec((tm, n), lambda i: (i, 0))
    return pl.pallas_call(
        _inproj_kernel,
        grid=(t // tm,),
        in_specs=[row(D_MODEL), _resident((D_MODEL, IN_COLS))],
        out_specs=[row(128), row(128), row(G_COLS), row(A_COLS)],
        out_shape=[jax.ShapeDtypeStruct((t, 128), F32), jax.ShapeDtypeStruct((t, 128), F32),
                   jax.ShapeDtypeStruct((t, G_COLS), F32), jax.ShapeDtypeStruct((t, A_COLS), F32)],
        compiler_params=_params("parallel"),
        name="inproj",
    )(x2d, w_packed)


def _s5_matrices(a_re, a_im, log_step, b_re, b_im, c_re, c_im, d_skip, nch):
    c, ng, w = S5_CHUNK, S5_GROUPS, S5_CHUNK * S5_GROUP
    ar, ai = a_re.astype(F32), a_im.astype(F32)
    step = jnp.exp(log_step.astype(F32))
    lr, li = ar * step, ai * step

    def lam_pow(z, k):
        kk = jnp.asarray(k, F32)[None, :, None, None]
        mag = jnp.exp(kk * lr[z][:, None, None, :])
        ang = kk * li[z][:, None, None, :]
        return mag * jnp.cos(ang), mag * jnp.sin(ang)

    def cmul(xr, xi, yr, yi):
        return xr * yr - xi * yi, xr * yi + xi * yr

    l1r, l1i = jnp.exp(lr) * jnp.cos(li), jnp.exp(lr) * jnp.sin(li)
    nr, ni = l1r - 1.0, l1i
    den = ar * ar + ai * ai
    qr, qi = (nr * ar + ni * ai) / den, (ni * ar - nr * ai) / den
    bt_r = jnp.swapaxes(b_re.astype(F32), -1, -2)
    bt_i = jnp.swapaxes(b_im.astype(F32), -1, -2)
    bbr, bbi = cmul(qr[:, :, None, :], qi[:, :, None, :], bt_r, bt_i)
    cr, ci = c_re.astype(F32), c_im.astype(F32)

    idx = np.arange(c)
    flat = lambda t: t.reshape(ng, w, S5_STATE)

    def with_c(z, k):
        pr, pi = lam_pow(z, k)
        xr, xi = cmul(pr, pi, cr[z][:, None], ci[z][:, None])
        return flat(xr), flat(xi)

    def with_b(z, k):
        pr, pi = lam_pow(z, k)
        yr, yi = cmul(pr, pi, bbr[z][:, None], bbi[z][:, None])
        return flat(yr), flat(yi)

    def re_prod(x, y):
        return jnp.einsum('gmk,gnk->gmn', jnp.concatenate([x[0], -x[1]], axis=-1),
                          jnp.concatenate([y[0], y[1]], axis=-1), precision=lax.Precision.HIGHEST)

    ri = np.arange(w)[:, None] // S5_GROUP
    cj = np.arange(w)[None, :] // S5_GROUP
    tt_mat = (jnp.where(ri >= cj, re_prod(with_c(0, idx), with_b(0, -idx)), 0.0)
              + jnp.where(cj >= ri, re_prod(with_c(1, -idx), with_b(1, idx)), 0.0))

    sf, sb = with_b(0, c - 1 - idx), with_b(1, idx)
    s_mat = jnp.concatenate([sf[0], sb[0], sf[1], sb[1]], axis=-1)
    of, ob = with_c(0, idx + 1), with_c(1, c - idx)
    ot_mat = jnp.concatenate([of[0], ob[0], -of[1], -ob[1]], axis=-1)

    pcr = [lam_pow(z, [c]) for z in range(2)]
    a_r = jnp.concatenate([pcr[0][0][:, 0, 0, :], pcr[1][0][:, 0, 0, :]], axis=-1)
    a_i = jnp.concatenate([pcr[0][1][:, 0, 0, :], pcr[1][1][:, 0, 0, :]], axis=-1)
    d_col = jnp.broadcast_to(jnp.tile(d_skip.astype(F32), (1, c))[:, :, None], (ng, w, nch))
    return tt_mat.astype(BF16), s_mat.astype(BF16), ot_mat.astype(BF16), a_r, a_i, d_col


def _gelu_tanh(x):
    return 0.5 * x * (1.0 + jnp.tanh(math.sqrt(2.0 / math.pi) * (x + 0.044715 * (x * x * x))))


def _sigmoid(x):
    return 1.0 / (1.0 + jnp.exp(-x))


def _s5_kernel(ul_ref, uh_ref, tt_ref, s_ref, ot_ref, ar_ref, ai_ref, dcol_ref, wglu_ref, bglu_ref,
               yl_ref, yh_ref, d_scr, cr_scr, ci_scr, efr_scr, efi_scr, ebr_scr, ebi_scr, zt_scr):
    c, ng = S5_CHUNK, S5_GROUPS
    nch = ul_ref.shape[1] // c
    for j in range(c):
        tok = pl.ds(j, nch, stride=c)
        uj = jnp.concatenate([ul_ref[0, tok, :], uh_ref[0, tok, :]], axis=1)
        d_scr[:, j * S5_GROUP:(j + 1) * S5_GROUP, :] = uj.T.reshape(ng, S5_GROUP, nch)
    contrib = [lax.dot_general(d_scr[g].astype(BF16), s_ref[g], _TN, preferred_element_type=F32)
               for g in range(ng)]
    for g in range(ng):
        grp = pl.ds(g, nch, stride=ng)
        cr_scr[grp, :] = contrib[g][:, :128]
        ci_scr[grp, :] = contrib[g][:, 128:]

    ar, ai = ar_ref[...], ai_ref[...]
    fwd = lax.broadcasted_iota(jnp.int32, (ng, 128), 1) < S5_STATE

    def step(k, carry):
        sre, sim = carry
        up = pl.ds(pl.multiple_of(k * ng, ng), ng)
        dn = pl.ds(pl.multiple_of((nch - 1 - k) * ng, ng), ng)
        efr_scr[up, :] = sre
        efi_scr[up, :] = sim
        ebr_scr[dn, :] = sre
        ebi_scr[dn, :] = sim
        cre = jnp.where(fwd, cr_scr[up, :], cr_scr[dn, :])
        cim = jnp.where(fwd, ci_scr[up, :], ci_scr[dn, :])
        return ar * sre - ai * sim + cre, ar * sim + ai * sre + cim

    zero = jnp.zeros((ng, 128), F32)
    lax.fori_loop(0, nch, step, (zero, zero), unroll=8)

    fwd_all = lax.broadcasted_iota(jnp.int32, (nch, 128), 1) < S5_STATE
    enter = []
    for g in range(ng):
        grp = pl.ds(g, nch, stride=ng)
        enter.append(jnp.concatenate([jnp.where(fwd_all, efr_scr[grp, :], ebr_scr[grp, :]),
                                      jnp.where(fwd_all, efi_scr[grp, :], ebi_scr[grp, :])],
                                     axis=1).astype(BF16))
    yt = [_mm(tt_ref[g], d_scr[g].astype(BF16)) + lax.dot_general(ot_ref[g], enter[g], _NT, preferred_element_type=F32)
          for g in range(ng)]
    for g in range(ng):
        zt_scr[:, g * S5_GROUP:(g + 1) * S5_GROUP, :] = _gelu_tanh(yt[g] + dcol_ref[g] * d_scr[g]).reshape(c, S5_GROUP, nch)
    glu = [lax.dot_general(zt_scr[i].astype(BF16), wglu_ref[...], _TN, preferred_element_type=F32)
           for i in range(c)]
    for i in range(c):
        gl = glu[i] + bglu_ref[...]
        ya = gl[:, :S5_WIDTH] * _sigmoid(gl[:, S5_WIDTH:])
        tok = pl.ds(i, nch, stride=c)
        yl_ref[0, tok, :] = ya[:, :128]
        yh_ref[0, tok, :] = ya[:, 128:]


def _s5(ul3, uh3, mats, wglu, bglu):
    b, seq, _ = ul3.shape
    nch = seq // S5_CHUNK
    tt_mat, s_mat, ot_mat, a_r, a_i, d_col = mats
    w = S5_CHUNK * S5_GROUP
    half = pl.BlockSpec((1, seq, 128), lambda i: (i, 0, 0))
    state_scr = pltpu.VMEM((S5_GROUPS * nch, 2 * S5_STATE), F32)
    return pl.pallas_call(
        _s5_kernel,
        grid=(b,),
        in_specs=[half, half,
                  _resident((S5_GROUPS, w, w)), _resident((S5_GROUPS, w, 4 * S5_STATE)),
                  _resident((S5_GROUPS, w, 4 * S5_STATE)),
                  _resident((S5_GROUPS, 128)), _resident((S5_GROUPS, 128)),
                  _resident((S5_GROUPS, w, nch)),
                  _resident((S5_WIDTH, 2 * S5_WIDTH)), _resident((1, 2 * S5_WIDTH))],
        out_specs=[half, half],
        out_shape=[jax.ShapeDtypeStruct((b, seq, 128), F32)] * 2,
        scratch_shapes=[pltpu.VMEM((S5_GROUPS, w, nch), F32)] + [state_scr] * 6
                       + [pltpu.VMEM((S5_CHUNK, w, nch), F32)],
        compiler_params=_params("parallel"),
        name="s5",
    )(ul3, uh3, tt_mat, s_mat, ot_mat, a_r, a_i, d_col, wglu, bglu)


def _gla_constants():
    r = np.arange(256)
    same = (r[:, None] // GLA_CHUNK) == (r[None, :] // GLA_CHUNK)
    prefix = same & (r[None, :] <= r[:, None])
    suffix = same & (r[None, :] >= r[:, None])
    avg = ((r[:, None] // GLA_DV) == (r[None, :] // GLA_DV)) / float(GLA_DV)
    as_bf = lambda m: jnp.asarray(m.astype(np.float32), BF16)
    return as_bf(prefix), as_bf(suffix), as_bf(same), as_bf(avg)


def _log_sigmoid(x):
    return jnp.minimum(x, 0.0) - jnp.log(1.0 + jnp.exp(-jnp.abs(x)))


def _split_mm(a, b):
    hi = a.astype(BF16)
    lo = (a - hi.astype(F32)).astype(BF16)
    return _mm(hi, b) + _mm(lo, b)


def _chunk_sums(mat, la):
    hi = la.astype(BF16)
    lo = (la - hi.astype(F32)).astype(BF16)
    out = []
    for i in range(la.shape[0] // 256):
        sl = slice(i * 256, (i + 1) * 256)
        out.append(_mm(mat, hi[sl]) + _mm(mat, lo[sl]))
    return jnp.concatenate(out, axis=0)


def _gla_kernel(g_ref, waf_ref, wab_ref, ba_ref, pre_ref, suf_ref, tot_ref, avg_ref, lng_ref, o_ref,
                q_s, kin_s, kst_s, v_s, totf_s, totb_s, o_s, kv_s, st_s):
    seq = g_ref.shape[1]
    nchunk = seq // GLA_CHUNK
    c = GLA_CHUNK
    grp = math.gcd(GLA_GROUP, nchunk)
    q = g_ref[0, :, 0:128] * (GLA_DK ** -0.5)
    k = g_ref[0, :, 128:256]
    lfb = g_ref[0, :, 768:896].astype(BF16)
    la_f = _log_sigmoid(_mm(lfb, waf_ref[...]) + ba_ref[0:1, :]) * (1.0 / GLA_TAU)
    la_b = _log_sigmoid(_mm(lfb, wab_ref[...]) + ba_ref[1:2, :]) * (1.0 / GLA_TAU)

    bf = _chunk_sums(pre_ref[...], la_f)
    tf = _chunk_sums(tot_ref[...], la_f)
    q_s[:, 0:128] = (q * jnp.exp(bf)).astype(BF16)
    kin_s[:, 0:128] = (k * jnp.exp(-bf)).astype(BF16)
    kst_s[:, 0:128] = (k * jnp.exp(tf - bf)).astype(BF16)
    totf_s[...] = tf
    bs = _chunk_sums(suf_ref[...], la_b)
    tb = _chunk_sums(tot_ref[...], la_b)
    q_s[:, 128:256] = (q * jnp.exp(bs)).astype(BF16)
    kin_s[:, 128:256] = (k * jnp.exp(-bs)).astype(BF16)
    kst_s[:, 128:256] = (k * jnp.exp(tb - bs)).astype(BF16)
    totb_s[...] = tb
    v_s[...] = g_ref[0, :, 256:512].astype(BF16)

    r256 = lax.broadcasted_iota(jnp.int32, (4 * c, 256), 0)
    l256 = lax.broadcasted_iota(jnp.int32, (4 * c, 256), 1)
    head_k = (r256 // c) == ((l256 % 128) // GLA_DK)
    head_v = (r256 // c) == (l256 // GLA_DV)
    qi = lax.broadcasted_iota(jnp.int32, (c, 4 * c), 0)
    kj = lax.broadcasted_iota(jnp.int32, (c, 4 * c), 1) % c
    zero_bf = jnp.zeros((), BF16)

    def local_group(it, carry):
        starts = [pl.multiple_of((it * grp + u) * c, c) for u in range(grp)]
        qq = [q_s[pl.ds(r0, c), :] for r0 in starts]
        vv = [v_s[pl.ds(r0, c), :] for r0 in starts]
        kexp = [jnp.where(head_k, jnp.concatenate([kin_s[pl.ds(r0, c), :]] * 4, axis=0), zero_bf) for r0 in starts]
        sc_f = [lax.dot_general(q[:, 0:128], k[:, 0:128], _NT, preferred_element_type=F32) for q, k in zip(qq, kexp)]
        sc_b = [lax.dot_general(q[:, 128:256], k[:, 128:256], _NT, preferred_element_type=F32) for q, k in zip(qq, kexp)]
        kv_t = [lax.dot_general(v, kst_s[pl.ds(r0, c), :], _TN, preferred_element_type=F32) for v, r0 in zip(vv, starts)]
        for u, r0 in enumerate(starts):
            vexp = jnp.where(head_v, jnp.concatenate([vv[u]] * 4, axis=0), zero_bf)
            pm = jnp.where(kj <= qi, sc_f[u], sc_b[u]).astype(BF16)
            o_s[pl.ds(r0, c), :] = _mm(pm, vexp)
            kv_s[it * grp + u] = jnp.where(head_k, kv_t[u], 0.0)
        return carry

    lax.fori_loop(0, nchunk // grp, local_group, 0)

    def carry_states(n, carry):
        st_f, st_b = carry
        m = nchunk - 1 - n
        st_s[n, :, 0:128] = st_f.astype(BF16)
        st_s[m, :, 128:256] = st_b.astype(BF16)
        st_f = jnp.exp(totf_s[pl.ds(n * c, 1), :]) * st_f + kv_s[n, :, 0:128]
        st_b = jnp.exp(totb_s[pl.ds(m * c, 1), :]) * st_b + kv_s[m, :, 128:256]
        return st_f, st_b

    zero = jnp.zeros((GLA_WIDTH, GLA_QK), F32)
    lax.fori_loop(0, nchunk, carry_states, (zero, zero))

    def cross_group(it, carry):
        starts = [pl.multiple_of((it * grp + u) * c, c) for u in range(grp)]
        upd = [lax.dot_general(q_s[pl.ds(r0, c), :], st_s[it * grp + u], _NT, preferred_element_type=F32)
               for u, r0 in enumerate(starts)]
        for r0, d in zip(starts, upd):
            o_s[pl.ds(r0, c), :] += d
        return carry

    lax.fori_loop(0, nchunk // grp, cross_group, 0)

    o = o_s[...]
    avg = avg_ref[...]
    mu = _split_mm(o, avg)
    dlt = o - mu
    var = _split_mm(dlt * dlt, avg)
    on = dlt * lax.rsqrt(var + LN_EPS)
    r = g_ref[0, :, 512:768]
    o_ref[0] = on * lng_ref[...] * (r * _sigmoid(r))


def _gla(g3, waf, wab, ba, consts, ln_g):
    b, seq, _ = g3.shape
    pre, suf, tot, avg = consts
    nchunk = seq // GLA_CHUNK
    both = pltpu.VMEM((seq, 2 * GLA_QK), BF16)
    return pl.pallas_call(
        _gla_kernel,
        grid=(b,),
        in_specs=[pl.BlockSpec((1, seq, G_COLS), lambda i: (i, 0, 0)),
                  _resident((128, 128)), _resident((128, 128)), _resident((2, 128)),
                  _resident((256, 256)), _resident((256, 256)), _resident((256, 256)),
                  _resident((256, 256)), _resident((1, 256))],
        out_specs=pl.BlockSpec((1, seq, GLA_WIDTH), lambda i: (i, 0, 0)),
        out_shape=jax.ShapeDtypeStruct((b, seq, GLA_WIDTH), F32),
        scratch_shapes=[both, both, both, pltpu.VMEM((seq, GLA_WIDTH), BF16),
                        pltpu.VMEM((seq, GLA_QK), F32), pltpu.VMEM((seq, GLA_QK), F32),
                        pltpu.VMEM((seq, GLA_WIDTH), F32),
                        pltpu.VMEM((nchunk, GLA_WIDTH, 2 * GLA_QK), F32),
                        pltpu.VMEM((nchunk, GLA_WIDTH, 2 * GLA_QK), BF16)],
        compiler_params=_params("parallel"),
        name="gla",
    )(g3, waf, wab, ba, pre, suf, tot, avg, ln_g)


def _rope_tables(seq):
    pos = jnp.arange(seq, dtype=F32)
    inv_freq = ROPE_THETA ** (-jnp.arange(0, ROT_DIM, 2, dtype=F32) / ROT_DIM)
    ang = pos[:, None] * inv_freq[None, :]
    cos, sin = jnp.cos(ang), jnp.sin(ang)
    half = ROT_DIM // 2
    pad = jnp.zeros((seq, SWA_HEAD_DIM - ROT_DIM), F32)
    c_head = jnp.concatenate([cos, cos, pad + 1.0], axis=1)
    s_head = jnp.concatenate([sin, sin, pad], axis=1)
    two = lambda t: jnp.concatenate([t, t], axis=1)
    rot = np.zeros((128, 128), np.float32)
    for base in (0, SWA_HEAD_DIM):
        for d in range(half):
            rot[base + d + half, base + d] = -1.0
            rot[base + d, base + d + half] = 1.0
    return two(c_head), two(s_head), jnp.asarray(rot, BF16)


def _swa_kernel(sink_ref, a_ref, cos_ref, sin_ref, rot_ref, o_ref,
                q_s, klo_s, khi_s, vd_s, sa_s, sb_s, pa_s, pb_s, bias_s, ma_s, mb_s):
    seq = a_ref.shape[1]
    blk = SWA_BLOCK
    hd = SWA_HEAD_DIM
    pairs = SWA_GROUP // 2
    cosv, sinv, rot = cos_ref[...], sin_ref[...], rot_ref[...]

    def rope(x):
        return x * cosv + _split_mm(x, rot) * sinv

    for slab in range(SWA_WIDTH // 128):
        q_s[slab] = (rope(a_ref[0, :, slab * 128:(slab + 1) * 128]) * (hd ** -0.5 * LOG2E)).astype(BF16)
    kr = rope(a_ref[0, :, SWA_WIDTH:SWA_WIDTH + SWA_KV])
    vr = a_ref[0, :, SWA_WIDTH + SWA_KV:A_COLS]
    ksw = pltpu.roll(kr, hd, 1)
    vsw = pltpu.roll(vr, hd, 1)
    low = lax.broadcasted_iota(jnp.int32, (seq, 128), 1) < hd
    body = slice(blk, blk + seq)
    zpad = jnp.zeros((blk, 128), BF16)
    for kh in range(SWA_KV_HEADS):
        for ref in (klo_s, khi_s):
            ref[kh, 0:blk, :] = zpad
            ref[kh, blk + seq:2 * blk + seq, :] = zpad
        vd_s[kh, 0:blk, 0:128] = zpad
        vd_s[kh, blk + seq:2 * blk + seq, 0:128] = zpad
        vd_s[kh, :, 128:256] = jnp.ones((seq + 2 * blk, 128), BF16)
    klo_s[0, body, :] = jnp.where(low, kr, 0.0).astype(BF16)
    khi_s[0, body, :] = jnp.where(low, 0.0, ksw).astype(BF16)
    klo_s[1, body, :] = jnp.where(low, ksw, 0.0).astype(BF16)
    khi_s[1, body, :] = jnp.where(low, 0.0, kr).astype(BF16)
    vd_s[0, body, 0:128] = jnp.where(low, vr, vsw).astype(BF16)
    vd_s[1, body, 0:128] = jnp.where(low, vsw, vr).astype(BF16)

    qi = lax.broadcasted_iota(jnp.int32, (blk, 3 * blk), 0)
    kk = lax.broadcasted_iota(jnp.int32, (blk, 3 * blk), 1)
    band = jnp.abs(qi + blk - kk) <= SWA_WINDOW
    low_o = lax.broadcasted_iota(jnp.int32, (blk, 128), 1) < hd
    rows = SWA_GROUP * blk
    chunks = [(kh, rb) for kh in range(SWA_KV_HEADS) for rb in range(rows // SWA_ROWS)]

    def scores(n, s_s):
        r0 = pl.multiple_of(n * blk, blk)
        for kh in range(SWA_KV_HEADS):
            kcat = jnp.concatenate([klo_s[kh, pl.ds(r0, 3 * blk), :], khi_s[kh, pl.ds(r0, 3 * blk), :]], axis=0)
            for pr in range(pairs):
                qq = q_s[kh * pairs + pr, pl.ds(r0, blk), :]
                s2 = lax.dot_general(qq, kcat, _NT, preferred_element_type=F32)
                s_s[kh, (2 * pr) * blk:(2 * pr + 1) * blk, :] = s2[:, :3 * blk]
                s_s[kh, (2 * pr + 1) * blk:(2 * pr + 2) * blk, :] = s2[:, 3 * blk:]

    nblk = seq // blk

    def softmax_block(n, s_s, p_s, m_s):
        kpos = kk + (n - 1) * blk
        bias_s[...] = jnp.where(band & (kpos >= 0) & (kpos < seq), 0.0, NEG_BIG)
        for kh, rb in chunks:
            rs = slice(rb * SWA_ROWS, (rb + 1) * SWA_ROWS)
            bs = slice((rb * SWA_ROWS) % blk, (rb * SWA_ROWS) % blk + SWA_ROWS)
            sink2 = sink_ref[kh * SWA_GROUP + (rb * SWA_ROWS) // blk] * LOG2E
            lo = s_s[kh, rs, 0:blk] + bias_s[bs, 0:blk]
            hi = s_s[kh, rs, 2 * blk:3 * blk] + bias_s[bs, 2 * blk:3 * blk]
            s_s[kh, rs, 0:blk] = lo
            s_s[kh, rs, 2 * blk:3 * blk] = hi
            top = jnp.maximum(jnp.maximum(lo, s_s[kh, rs, blk:2 * blk]), hi)
            m_s[kh, rs, :] = jnp.broadcast_to(jnp.maximum(jnp.max(top, axis=1, keepdims=True), sink2), (SWA_ROWS, 128))
        for kh, rb in chunks:
            rs = slice(rb * SWA_ROWS, (rb + 1) * SWA_ROWS)
            p = jnp.exp2(s_s[kh, rs, :] - jnp.concatenate([m_s[kh, rs, :]] * 3, axis=1))
            p_s[kh, rs, :] = p.astype(BF16)

    def output_block(n, p_s, m_s):
        r0 = pl.multiple_of(n * blk, blk)
        for kh in range(SWA_KV_HEADS):
            o3 = _mm(p_s[kh], vd_s[kh, pl.ds(r0, 3 * blk), :])
            sink2 = jnp.concatenate([jnp.full((blk, 128), sink_ref[kh * SWA_GROUP + g] * LOG2E, F32)
                                     for g in range(SWA_GROUP)], axis=0)
            o2 = o3[:, 0:128] / (o3[:, 128:256] + jnp.exp2(sink2 - m_s[kh]))
            o_ref[0, pl.ds(r0, blk), kh * SWA_GROUP * hd:(kh + 1) * SWA_GROUP * hd] = jnp.concatenate(
                [jnp.where(low_o, o2[(2 * pr) * blk:(2 * pr + 1) * blk], o2[(2 * pr + 1) * blk:(2 * pr + 2) * blk])
                 for pr in range(pairs)], axis=1)

    def body_fn(t, carry):
        n = 2 * t + 1
        output_block(n - 1, pa_s, ma_s)
        scores(n + 1, sa_s)
        softmax_block(n, sb_s, pb_s, mb_s)
        output_block(n, pb_s, mb_s)
        scores(jnp.minimum(n + 2, nblk - 1), sb_s)
        softmax_block(n + 1, sa_s, pa_s, ma_s)
        return carry

    scores(0, sa_s)
    scores(1, sb_s)
    softmax_block(0, sa_s, pa_s, ma_s)
    lax.fori_loop(0, (nblk - 2) // 2, body_fn, 0)
    softmax_block(nblk - 1, sb_s, pb_s, mb_s)
    output_block(nblk - 2, pa_s, ma_s)
    output_block(nblk - 1, pb_s, mb_s)


def _swa(a3, sink, tables):
    b, seq, _ = a3.shape
    cos_t, sin_t, rot = tables
    tab = pl.BlockSpec((seq, 128), lambda i, s: (0, 0))
    rot_spec = pl.BlockSpec((128, 128), lambda i, s: (0, 0))
    banded = pltpu.VMEM((SWA_KV_HEADS, seq + 2 * SWA_BLOCK, 128), BF16)
    stat = pltpu.VMEM((SWA_KV_HEADS, SWA_GROUP * SWA_BLOCK, 128), F32)
    grid_spec = pltpu.PrefetchScalarGridSpec(
        num_scalar_prefetch=1,
        grid=(b,),
        in_specs=[pl.BlockSpec((1, seq, A_COLS), lambda i, s: (i, 0, 0)), tab, tab, rot_spec],
        out_specs=pl.BlockSpec((1, seq, SWA_WIDTH), lambda i, s: (i, 0, 0)),
        scratch_shapes=[pltpu.VMEM((SWA_WIDTH // 128, seq, 128), BF16), banded, banded,
                        pltpu.VMEM((SWA_KV_HEADS, seq + 2 * SWA_BLOCK, 256), BF16),
                        pltpu.VMEM((SWA_KV_HEADS, SWA_GROUP * SWA_BLOCK, 3 * SWA_BLOCK), F32),
                        pltpu.VMEM((SWA_KV_HEADS, SWA_GROUP * SWA_BLOCK, 3 * SWA_BLOCK), F32),
                        pltpu.VMEM((SWA_KV_HEADS, SWA_GROUP * SWA_BLOCK, 3 * SWA_BLOCK), BF16),
                        pltpu.VMEM((SWA_KV_HEADS, SWA_GROUP * SWA_BLOCK, 3 * SWA_BLOCK), BF16),
                        pltpu.VMEM((SWA_BLOCK, 3 * SWA_BLOCK), F32), stat, stat],
    )
    return pl.pallas_call(
        _swa_kernel,
        grid_spec=grid_spec,
        out_shape=jax.ShapeDtypeStruct((b, seq, SWA_WIDTH), F32),
        compiler_params=_params("parallel"),
        name="swa",
    )(sink, a3, cos_t, sin_t, rot)


def _layer_norm(x, g, b):
    mu = jnp.mean(x, axis=-1, keepdims=True)
    d = x - mu
    var = jnp.mean(d * d, axis=-1, keepdims=True)
    return d * lax.rsqrt(var + LN_EPS) * g + b


def _out_ffn_kernel(x_ref, yal_ref, yah_ref, yb_ref, yc_ref, wo_ref,
                    g1_ref, b1_ref, w1_ref, w2_ref, g2_ref, b2_ref, o_ref, acc_ref, x1b_ref, *, alpha):
    subs = [slice(r, r + FFN_SUB) for r in range(0, x_ref.shape[0], FFN_SUB)]
    for rows in subs:
        ya = jnp.concatenate([yal_ref[rows, :], yah_ref[rows, :]], axis=1)
        mix = (_mm(ya.astype(BF16), wo_ref[0:S5_WIDTH, :])
               + _mm(yb_ref[rows, :].astype(BF16), wo_ref[S5_WIDTH:S5_WIDTH + GLA_WIDTH, :])
               + _mm(yc_ref[rows, :].astype(BF16), wo_ref[S5_WIDTH + GLA_WIDTH:D_MODEL, :]))
        x1 = _layer_norm(alpha * x_ref[rows, :] + mix, g1_ref[...], b1_ref[...])
        o_ref[rows, :] = x1
        x1b_ref[rows, :] = x1.astype(BF16)
    for rows in subs:
        for ci in range(D_FF // FF_CHUNK):
            sl = slice(ci * FF_CHUNK, (ci + 1) * FF_CHUNK)
            hid = jnp.maximum(_mm(x1b_ref[rows, :], w1_ref[:, sl]), 0.0)
            part = _mm((hid * hid).astype(BF16), w2_ref[sl, :])
            if ci == 0:
                acc_ref[rows, :] = part
            else:
                acc_ref[rows, :] += part
    for rows in subs:
        o_ref[rows, :] = _layer_norm(alpha * o_ref[rows, :] + acc_ref[rows, :], g2_ref[...], b2_ref[...])


def _out_ffn(x2d, yal2d, yah2d, yb2d, yc2d, wo, g1, b1, w1, w2, g2, b2, alpha):
    t = x2d.shape[0]
    tm = min(FFN_TILE, t)
    row = lambda n: pl.BlockSpec((tm, n), lambda i: (i, 0))
    once = lambda shape: pl.BlockSpec(shape, lambda i: (0, 0), pipeline_mode=pl.Buffered(1))
    return pl.pallas_call(
        functools.partial(_out_ffn_kernel, alpha=alpha),
        grid=(t // tm,),
        in_specs=[row(D_MODEL), row(128), row(128), row(GLA_WIDTH), row(SWA_WIDTH),
                  once((D_MODEL, D_MODEL)), once((1, D_MODEL)), once((1, D_MODEL)),
                  once((D_MODEL, D_FF)), once((D_FF, D_MODEL)),
                  once((1, D_MODEL)), once((1, D_MODEL))],
        out_specs=row(D_MODEL),
        out_shape=jax.ShapeDtypeStruct((t, D_MODEL), F32),
        scratch_shapes=[pltpu.VMEM((tm, D_MODEL), F32), pltpu.VMEM((tm, D_MODEL), BF16)],
        compiler_params=_params("parallel"),
        name="out_ffn",
    )(x2d, yal2d, yah2d, yb2d, yc2d, wo, g1, b1, w1, w2, g2, b2)


def _pack_w_in(w_in):
    points = np.cumsum([S5_WIDTH, GLA_QK, GLA_QK, GLA_WIDTH, GLA_WIDTH, GLA_RANK, GLA_RANK,
                        SWA_WIDTH, SWA_KV]).tolist()
    s5, gq, gk, gv, gr, glf, glb, aq, ak, av = jnp.split(w_in, points, axis=-1)
    pad = jnp.zeros(w_in.shape[:-1] + (128 - 2 * GLA_RANK,), w_in.dtype)
    return jnp.concatenate([s5, gq, gk, gv, gr, glf, glb, pad, aq, ak, av], axis=-1).astype(BF16)


def _pad_gate_w(w_a):
    z = jnp.zeros((128, GLA_QK), F32)
    waf = z.at[0:GLA_RANK].set(w_a[0].astype(F32))
    wab = z.at[GLA_RANK:2 * GLA_RANK].set(w_a[1].astype(F32))
    return waf.astype(BF16), wab.astype(BF16)


def _layer(x2d, bsz, seq, alpha, w_packed, a_re, a_im, log_step, b_re, b_im, c_re, c_im, d_skip,
           w_glu, b_glu, gla_w_a, gla_b_a, gla_ln_g, swa_sink, w_out, ln1_g, ln1_b, w_ff1, w_ff2,
           ln2_g, ln2_b, gla_consts, rope_tabs):
    t = bsz * seq
    row = lambda v: v.astype(F32)[None, :]
    ul, uh, g, a = _inproj(x2d, w_packed)

    mats = _s5_matrices(a_re, a_im, log_step, b_re, b_im, c_re, c_im, d_skip, seq // S5_CHUNK)
    yal, yah = _s5(ul.reshape(bsz, seq, 128), uh.reshape(bsz, seq, 128), mats, w_glu, row(b_glu))

    waf, wab = _pad_gate_w(gla_w_a)
    yb = _gla(g.reshape(bsz, seq, G_COLS), waf, wab, gla_b_a.astype(F32), gla_consts, row(gla_ln_g))
    yc = _swa(a.reshape(bsz, seq, A_COLS), swa_sink.astype(F32), rope_tabs)

    return _out_ffn(x2d, yal.reshape(t, 128), yah.reshape(t, 128), yb.reshape(t, GLA_WIDTH), yc.reshape(t, SWA_WIDTH),
                    w_out, row(ln1_g), row(ln1_b), w_ff1, w_ff2, row(ln2_g), row(ln2_b), alpha)


def kernel(x, w_in, s5_a_re, s5_a_im, s5_log_step, s5_b_re, s5_b_im, s5_c_re, s5_c_im, s5_d, s5_w_glu, s5_b_glu, gla_w_a, gla_b_a, gla_ln_g, swa_sink, w_out, ln1_g, ln1_b, w_ff1, w_ff2, ln2_g, ln2_b):
    bsz, seq, _ = x.shape
    depth = w_in.shape[0]
    alpha = (2 * depth) ** 0.25
    gla_consts = _gla_constants()
    rope_tabs = _rope_tables(seq)
    w_packed = _pack_w_in(w_in)
    s5_w_glu, w_out, w_ff1, w_ff2 = (w.astype(BF16) for w in (s5_w_glu, w_out, w_ff1, w_ff2))
    h = x.reshape(bsz * seq, D_MODEL)
    for l in range(depth):
        h = _layer(h, bsz, seq, alpha, w_packed[l], s5_a_re[l], s5_a_im[l], s5_log_step[l],
                   s5_b_re[l], s5_b_im[l], s5_c_re[l], s5_c_im[l], s5_d[l], s5_w_glu[l], s5_b_glu[l],
                   gla_w_a[l], gla_b_a[l], gla_ln_g[l], swa_sink[l], w_out[l], ln1_g[l], ln1_b[l],
                   w_ff1[l], w_ff2[l], ln2_g[l], ln2_b[l], gla_consts, rope_tabs)
    return h.reshape(bsz, seq, D_MODEL)
```

```python
import functools
import math

import jax
import jax.numpy as jnp
import numpy as np
from jax import lax
from jax.experimental import pallas as pl
from jax.experimental.pallas import tpu as pltpu

F32 = jnp.float32
BF16 = jnp.bfloat16

D_MODEL = 1024
S5_WIDTH = 256
S5_GROUP = 16
S5_GROUPS = 16
S5_STATE = 64
S5_CHUNK = 16
GLA_HEADS = 4
GLA_DK = 32
GLA_DV = 64
GLA_QK = 128
GLA_WIDTH = 256
GLA_RANK = 16
GLA_TAU = 16.0
GLA_CHUNK = 64
GLA_GROUP = 16
SWA_WIDTH = 512
SWA_HEAD_DIM = 64
SWA_HEADS = 8
SWA_KV_HEADS = 2
SWA_GROUP = SWA_HEADS // SWA_KV_HEADS
SWA_KV = 128
SWA_WINDOW = 128
SWA_BLOCK = 128
ROT_DIM = 16
ROPE_THETA = 500000.0
D_FF = 4096
LN_EPS = 1e-5
NEG_BIG = -1e30
LOG2E = 1.4426950408889634

U_COLS = S5_WIDTH
G_COLS = GLA_QK + GLA_QK + GLA_WIDTH + GLA_WIDTH + 128
A_COLS = SWA_WIDTH + SWA_KV + SWA_KV
IN_COLS = U_COLS + G_COLS + A_COLS

VMEM_LIMIT = 56 * 1024 * 1024
ROW_TILE = 1024
FF_CHUNK = 512
FFN_TILE = 1024
FFN_SUB = 512
SWA_ROWS = 64

_NT = (((1,), (1,)), ((), ()))
_TN = (((0,), (0,)), ((), ()))


def _mm(a, b):
    return jnp.dot(a, b, preferred_element_type=F32)


def _params(*sem):
    return pltpu.CompilerParams(dimension_semantics=sem, vmem_limit_bytes=VMEM_LIMIT)


def _resident(shape):
    nd = len(shape)
    return pl.BlockSpec(shape, lambda *_: (0,) * nd)


def _inproj_kernel(x_ref, w_ref, ul_ref, uh_ref, g_ref, a_ref):
    x = x_ref[...].astype(BF16)
    u = _mm(x, w_ref[:, 0:U_COLS])
    ul_ref[...] = u[:, :128]
    uh_ref[...] = u[:, 128:]
    g_ref[...] = _mm(x, w_ref[:, U_COLS:U_COLS + G_COLS])
    a_ref[...] = _mm(x, w_ref[:, U_COLS + G_COLS:IN_COLS])


def _inproj(x2d, w_packed):
    t = x2d.shape[0]
    tm = min(ROW_TILE, t)
    row = lambda n: pl.BlockSpec((tm, n), lambda i: (i, 0))
    return pl.pallas_call(
        _inproj_kernel,
        grid=(t // tm,),
        in_specs=[row(D_MODEL), _resident((D_MODEL, IN_COLS))],
        out_specs=[row(128), row(128), row(G_COLS), row(A_COLS)],
        out_shape=[jax.ShapeDtypeStruct((t, 128), F32), jax.ShapeDtypeStruct((t, 128), F32),
                   jax.ShapeDtypeStruct((t, G_COLS), F32), jax.ShapeDtypeStruct((t, A_COLS), F32)],
        compiler_params=_params("parallel"),
        name="inproj",
    )(x2d, w_packed)


def _s5_matrices(a_re, a_im, log_step, b_re, b_im, c_re, c_im, d_skip, nch):
    c, ng, w = S5_CHUNK, S5_GROUPS, S5_CHUNK * S5_GROUP
    ar, ai = a_re.astype(F32), a_im.astype(F32)
    step = jnp.exp(log_step.astype(F32))
    lr, li = ar * step, ai * step

    def lam_pow(z, k):
        kk = jnp.asarray(k, F32)[None, :, None, None]
        mag = jnp.exp(kk * lr[z][:, None, None, :])
        ang = kk * li[z][:, None, None, :]
        return mag * jnp.cos(ang), mag * jnp.sin(ang)

    def cmul(xr, xi, yr, yi):
        return xr * yr - xi * yi, xr * yi + xi * yr

    l1r, l1i = jnp.exp(lr) * jnp.cos(li), jnp.exp(lr) * jnp.sin(li)
    nr, ni = l1r - 1.0, l1i
    den = ar * ar + ai * ai
    qr, qi = (nr * ar + ni * ai) / den, (ni * ar - nr * ai) / den
    bt_r = jnp.swapaxes(b_re.astype(F32), -1, -2)
    bt_i = jnp.swapaxes(b_im.astype(F32), -1, -2)
    bbr, bbi = cmul(qr[:, :, None, :], qi[:, :, None, :], bt_r, bt_i)
    cr, ci = c_re.astype(F32), c_im.astype(F32)

    idx = np.arange(c)
    flat = lambda t: t.reshape(ng, w, S5_STATE)

    def with_c(z, k):
        pr, pi = lam_pow(z, k)
        xr, xi = cmul(pr, pi, cr[z][:, None], ci[z][:, None])
        return flat(xr), flat(xi)

    def with_b(z, k):
        pr, pi = lam_pow(z, k)
        yr, yi = cmul(pr, pi, bbr[z][:, None], bbi[z][:, None])
        return flat(yr), flat(yi)

    def re_prod(x, y):
        return jnp.einsum('gmk,gnk->gmn', jnp.concatenate([x[0], -x[1]], axis=-1),
                          jnp.concatenate([y[0], y[1]], axis=-1), precision=lax.Precision.HIGHEST)

    ri = np.arange(w)[:, None] // S5_GROUP
    cj = np.arange(w)[None, :] // S5_GROUP
    tt_mat = (jnp.where(ri >= cj, re_prod(with_c(0, idx), with_b(0, -idx)), 0.0)
              + jnp.where(cj >= ri, re_prod(with_c(1, -idx), with_b(1, idx)), 0.0))

    sf, sb = with_b(0, c - 1 - idx), with_b(1, idx)
    s_mat = jnp.concatenate([sf[0], sb[0], sf[1], sb[1]], axis=-1)
    of, ob = with_c(0, idx + 1), with_c(1, c - idx)
    ot_mat = jnp.concatenate([of[0], ob[0], -of[1], -ob[1]], axis=-1)

    pcr = [lam_pow(z, [c]) for z in range(2)]
    a_r = jnp.concatenate([pcr[0][0][:, 0, 0, :], pcr[1][0][:, 0, 0, :]], axis=-1)
    a_i = jnp.concatenate([pcr[0][1][:, 0, 0, :], pcr[1][1][:, 0, 0, :]], axis=-1)
    d_col = jnp.broadcast_to(jnp.tile(d_skip.astype(F32), (1, c))[:, :, None], (ng, w, nch))
    return tt_mat.astype(BF16), s_mat.astype(BF16), ot_mat.astype(BF16), a_r, a_i, d_col


def _gelu_tanh(x):
    return 0.5 * x * (1.0 + jnp.tanh(math.sqrt(2.0 / math.pi) * (x + 0.044715 * (x * x * x))))


def _sigmoid(x):
    return 1.0 / (1.0 + jnp.exp(-x))


def _s5_kernel(ul_ref, uh_ref, tt_ref, s_ref, ot_ref, ar_ref, ai_ref, dcol_ref, wglu_ref, bglu_ref,
               yl_ref, yh_ref, d_scr, cr_scr, ci_scr, efr_scr, efi_scr, ebr_scr, ebi_scr, zt_scr):
    c, ng = S5_CHUNK, S5_GROUPS
    nch = ul_ref.shape[1] // c
    for j in range(c):
        tok = pl.ds(j, nch, stride=c)
        uj = jnp.concatenate([ul_ref[0, tok, :], uh_ref[0, tok, :]], axis=1)
        d_scr[:, j * S5_GROUP:(j + 1) * S5_GROUP, :] = uj.T.reshape(ng, S5_GROUP, nch)
    contrib = [lax.dot_general(d_scr[g].astype(BF16), s_ref[g], _TN, preferred_element_type=F32)
               for g in range(ng)]
    for g in range(ng):
        grp = pl.ds(g, nch, stride=ng)
        cr_scr[grp, :] = contrib[g][:, :128]
        ci_scr[grp, :] = contrib[g][:, 128:]

    ar, ai = ar_ref[...], ai_ref[...]
    fwd = lax.broadcasted_iota(jnp.int32, (ng, 128), 1) < S5_STATE

    def step(k, carry):
        sre, sim = carry
        up = pl.ds(pl.multiple_of(k * ng, ng), ng)
        dn = pl.ds(pl.multiple_of((nch - 1 - k) * ng, ng), ng)
        efr_scr[up, :] = sre
        efi_scr[up, :] = sim
        ebr_scr[dn, :] = sre
        ebi_scr[dn, :] = sim
        cre = jnp.where(fwd, cr_scr[up, :], cr_scr[dn, :])
        cim = jnp.where(fwd, ci_scr[up, :], ci_scr[dn, :])
        return ar * sre - ai * sim + cre, ar * sim + ai * sre + cim

    zero = jnp.zeros((ng, 128), F32)
    lax.fori_loop(0, nch, step, (zero, zero), unroll=8)

    fwd_all = lax.broadcasted_iota(jnp.int32, (nch, 128), 1) < S5_STATE
    enter = []
    for g in range(ng):
        grp = pl.ds(g, nch, stride=ng)
        enter.append(jnp.concatenate([jnp.where(fwd_all, efr_scr[grp, :], ebr_scr[grp, :]),
                                      jnp.where(fwd_all, efi_scr[grp, :], ebi_scr[grp, :])],
                                     axis=1).astype(BF16))
    yt = [_mm(tt_ref[g], d_scr[g].astype(BF16)) + lax.dot_general(ot_ref[g], enter[g], _NT, preferred_element_type=F32)
          for g in range(ng)]
    for g in range(ng):
        zt_scr[:, g * S5_GROUP:(g + 1) * S5_GROUP, :] = _gelu_tanh(yt[g] + dcol_ref[g] * d_scr[g]).reshape(c, S5_GROUP, nch)
    glu = [lax.dot_general(zt_scr[i].astype(BF16), wglu_ref[...], _TN, preferred_element_type=F32)
           for i in range(c)]
    for i in range(c):
        gl = glu[i] + bglu_ref[...]
        ya = gl[:, :S5_WIDTH] * _sigmoid(gl[:, S5_WIDTH:])
        tok = pl.ds(i, nch, stride=c)
        yl_ref[0, tok, :] = ya[:, :128]
        yh_ref[0, tok, :] = ya[:, 128:]


def _s5(ul3, uh3, mats, wglu, bglu):
    b, seq, _ = ul3.shape
    nch = seq // S5_CHUNK
    tt_mat, s_mat, ot_mat, a_r, a_i, d_col = mats
    w = S5_CHUNK * S5_GROUP
    half = pl.BlockSpec((1, seq, 128), lambda i: (i, 0, 0))
    state_scr = pltpu.VMEM((S5_GROUPS * nch, 2 * S5_STATE), F32)
    return pl.pallas_call(
        _s5_kernel,
        grid=(b,),
        in_specs=[half, half,
                  _resident((S5_GROUPS, w, w)), _resident((S5_GROUPS, w, 4 * S5_STATE)),
                  _resident((S5_GROUPS, w, 4 * S5_STATE)),
                  _resident((S5_GROUPS, 128)), _resident((S5_GROUPS, 128)),
                  _resident((S5_GROUPS, w, nch)),
                  _resident((S5_WIDTH, 2 * S5_WIDTH)), _resident((1, 2 * S5_WIDTH))],
        out_specs=[half, half],
        out_shape=[jax.ShapeDtypeStruct((b, seq, 128), F32)] * 2,
        scratch_shapes=[pltpu.VMEM((S5_GROUPS, w, nch), F32)] + [state_scr] * 6
                       + [pltpu.VMEM((S5_CHUNK, w, nch), F32)],
        compiler_params=_params("parallel"),
        name="s5",
    )(ul3, uh3, tt_mat, s_mat, ot_mat, a_r, a_i, d_col, wglu, bglu)


def _gla_constants():
    r = np.arange(256)
    same = (r[:, None] // GLA_CHUNK) == (r[None, :] // GLA_CHUNK)
    prefix = same & (r[None, :] <= r[:, None])
    suffix = same & (r[None, :] >= r[:, None])
    avg = ((r[:, None] // GLA_DV) == (r[None, :] // GLA_DV)) / float(GLA_DV)
    as_bf = lambda m: jnp.asarray(m.astype(np.float32), BF16)
    return as_bf(prefix), as_bf(suffix), as_bf(avg)


def _log_sigmoid(x):
    return jnp.minimum(x, 0.0) - jnp.log(1.0 + jnp.exp(-jnp.abs(x)))


def _split_mm(a, b):
    hi = a.astype(BF16)
    lo = (a - hi.astype(F32)).astype(BF16)
    return _mm(hi, b) + _mm(lo, b)


def _chunk_sums(mats, la):
    hi = la.astype(BF16)
    lo = (la - hi.astype(F32)).astype(BF16)
    outs = []
    for mat in mats:
        blocks = []
        for i in range(la.shape[0] // 256):
            sl = slice(i * 256, (i + 1) * 256)
            blocks.append(_mm(mat, hi[sl]) + _mm(mat, lo[sl]))
        outs.append(jnp.concatenate(blocks, axis=0))
    return outs


def _chunk_row(x, row):
    x3 = x.reshape(x.shape[0] // GLA_CHUNK, GLA_CHUNK, x.shape[1])
    return jnp.broadcast_to(x3[:, row:row + 1, :], x3.shape).reshape(x.shape)


def _gla_kernel(g_ref, waf_ref, wab_ref, ba_ref, pre_ref, suf_ref, avg_ref, lng_ref, o_ref,
                q_s, kin_s, kst_s, v_s, totf_s, totb_s, o_s, kv_s, st_s):
    seq = g_ref.shape[1]
    nchunk = seq // GLA_CHUNK
    c = GLA_CHUNK
    grp = math.gcd(GLA_GROUP, nchunk)
    q = g_ref[0, :, 0:128] * (GLA_DK ** -0.5)
    k = g_ref[0, :, 128:256]
    lfb = g_ref[0, :, 768:896].astype(BF16)
    la_f = _log_sigmoid(_mm(lfb, waf_ref[...]) + ba_ref[0:1, :]) * (LOG2E / GLA_TAU)
    la_b = _log_sigmoid(_mm(lfb, wab_ref[...]) + ba_ref[1:2, :]) * (LOG2E / GLA_TAU)

    (bf,) = _chunk_sums((pre_ref[...],), la_f)
    tf = _chunk_row(bf, c - 1)
    q_s[:, 0:128] = (q * jnp.exp2(bf)).astype(BF16)
    kin_s[:, 0:128] = (k * jnp.exp2(-bf)).astype(BF16)
    kst_s[:, 0:128] = (k * jnp.exp2(tf - bf)).astype(BF16)
    totf_s[...] = tf
    (bs,) = _chunk_sums((suf_ref[...],), la_b)
    tb = _chunk_row(bs, 0)
    q_s[:, 128:256] = (q * jnp.exp2(bs)).astype(BF16)
    kin_s[:, 128:256] = (k * jnp.exp2(-bs)).astype(BF16)
    kst_s[:, 128:256] = (k * jnp.exp2(tb - bs)).astype(BF16)
    totb_s[...] = tb
    v_s[...] = g_ref[0, :, 256:512].astype(BF16)

    r256 = lax.broadcasted_iota(jnp.int32, (4 * c, 256), 0)
    l256 = lax.broadcasted_iota(jnp.int32, (4 * c, 256), 1)
    head_k = (r256 // c) == ((l256 % 128) // GLA_DK)
    head_v = (r256 // c) == (l256 // GLA_DV)
    qi = lax.broadcasted_iota(jnp.int32, (c, 4 * c), 0)
    kj = lax.broadcasted_iota(jnp.int32, (c, 4 * c), 1) % c
    zero_bf = jnp.zeros((), BF16)

    def local_group(it, carry):
        starts = [pl.multiple_of((it * grp + u) * c, c) for u in range(grp)]
        qq = [q_s[pl.ds(r0, c), :] for r0 in starts]
        vv = [v_s[pl.ds(r0, c), :] for r0 in starts]
        kexp = [jnp.where(head_k, jnp.concatenate([kin_s[pl.ds(r0, c), :]] * 4, axis=0), zero_bf) for r0 in starts]
        sc_f = [lax.dot_general(q[:, 0:128], k[:, 0:128], _NT, preferred_element_type=F32) for q, k in zip(qq, kexp)]
        sc_b = [lax.dot_general(q[:, 128:256], k[:, 128:256], _NT, preferred_element_type=F32) for q, k in zip(qq, kexp)]
        kv_t = [lax.dot_general(v, kst_s[pl.ds(r0, c), :], _TN, preferred_element_type=F32) for v, r0 in zip(vv, starts)]
        for u, r0 in enumerate(starts):
            vexp = jnp.where(head_v, jnp.concatenate([vv[u]] * 4, axis=0), zero_bf)
            pm = jnp.where(kj <= qi, sc_f[u], sc_b[u]).astype(BF16)
            o_s[pl.ds(r0, c), :] = _mm(pm, vexp)
            kv_s[it * grp + u] = jnp.where(head_k, kv_t[u], 0.0)
        return carry

    lax.fori_loop(0, nchunk // grp, local_group, 0)

    def carry_states(n, carry):
        st_f, st_b = carry
        m = nchunk - 1 - n
        st_s[n, :, 0:128] = st_f.astype(BF16)
        st_s[m, :, 128:256] = st_b.astype(BF16)
        st_f = jnp.exp2(totf_s[pl.ds(n * c, 1), :]) * st_f + kv_s[n, :, 0:128]
        st_b = jnp.exp2(totb_s[pl.ds(m * c, 1), :]) * st_b + kv_s[m, :, 128:256]
        return st_f, st_b

    zero = jnp.zeros((GLA_WIDTH, GLA_QK), F32)
    lax.fori_loop(0, nchunk, carry_states, (zero, zero))

    def cross_group(it, carry):
        starts = [pl.multiple_of((it * grp + u) * c, c) for u in range(grp)]
        upd = [lax.dot_general(q_s[pl.ds(r0, c), :], st_s[it * grp + u], _NT, preferred_element_type=F32)
               for u, r0 in enumerate(starts)]
        for r0, d in zip(starts, upd):
            o_s[pl.ds(r0, c), :] += d
        return carry

    lax.fori_loop(0, nchunk // grp, cross_group, 0)

    o = o_s[...]
    avg = avg_ref[...]
    mu = _split_mm(o, avg)
    dlt = o - mu
    var = _split_mm(dlt * dlt, avg)
    on = dlt * lax.rsqrt(var + LN_EPS)
    r = g_ref[0, :, 512:768]
    o_ref[0] = on * lng_ref[...] * (r * _sigmoid(r))


def _gla(g3, waf, wab, ba, consts, ln_g):
    b, seq, _ = g3.shape
    pre, suf, avg = consts
    nchunk = seq // GLA_CHUNK
    both = pltpu.VMEM((seq, 2 * GLA_QK), BF16)
    return pl.pallas_call(
        _gla_kernel,
        grid=(b,),
        in_specs=[pl.BlockSpec((1, seq, G_COLS), lambda i: (i, 0, 0)),
                  _resident((128, 128)), _resident((128, 128)), _resident((2, 128)),
                  _resident((256, 256)), _resident((256, 256)), _resident((256, 256)), _resident((1, 256))],
        out_specs=pl.BlockSpec((1, seq, GLA_WIDTH), lambda i: (i, 0, 0)),
        out_shape=jax.ShapeDtypeStruct((b, seq, GLA_WIDTH), F32),
        scratch_shapes=[both, both, both, pltpu.VMEM((seq, GLA_WIDTH), BF16),
                        pltpu.VMEM((seq, GLA_QK), F32), pltpu.VMEM((seq, GLA_QK), F32),
                        pltpu.VMEM((seq, GLA_WIDTH), F32),
                        pltpu.VMEM((nchunk, GLA_WIDTH, 2 * GLA_QK), F32),
                        pltpu.VMEM((nchunk, GLA_WIDTH, 2 * GLA_QK), BF16)],
        compiler_params=_params("parallel"),
        name="gla",
    )(g3, waf, wab, ba, pre, suf, avg, ln_g)


def _rope_tables(seq):
    pos = jnp.arange(seq, dtype=F32)
    inv_freq = ROPE_THETA ** (-jnp.arange(0, ROT_DIM, 2, dtype=F32) / ROT_DIM)
    ang = pos[:, None] * inv_freq[None, :]
    cos, sin = jnp.cos(ang), jnp.sin(ang)
    half = ROT_DIM // 2
    pad = jnp.zeros((seq, SWA_HEAD_DIM - ROT_DIM), F32)
    c_head = jnp.concatenate([cos, cos, pad + 1.0], axis=1)
    s_head = jnp.concatenate([sin, sin, pad], axis=1)
    two = lambda t: jnp.concatenate([t, t], axis=1)
    rot = np.zeros((128, 128), np.float32)
    for base in (0, SWA_HEAD_DIM):
        for d in range(half):
            rot[base + d + half, base + d] = -1.0
            rot[base + d, base + d + half] = 1.0
    return two(c_head), two(s_head), jnp.asarray(rot, BF16)


def _swa_kernel(sink_ref, a_ref, cos_ref, sin_ref, rot_ref, o_ref,
                q_s, klo_s, khi_s, vd_s, sa_s, sb_s, pa_s, pb_s, bias_s, ma_s, mb_s):
    seq = a_ref.shape[1]
    blk = SWA_BLOCK
    hd = SWA_HEAD_DIM
    pairs = SWA_GROUP // 2
    cosv, sinv, rot = cos_ref[...], sin_ref[...], rot_ref[...]

    def rope(x):
        return x * cosv + _split_mm(x, rot) * sinv

    for slab in range(SWA_WIDTH // 128):
        q_s[slab] = (rope(a_ref[0, :, slab * 128:(slab + 1) * 128]) * (hd ** -0.5 * LOG2E)).astype(BF16)
    kr = rope(a_ref[0, :, SWA_WIDTH:SWA_WIDTH + SWA_KV])
    vr = a_ref[0, :, SWA_WIDTH + SWA_KV:A_COLS]
    ksw = pltpu.roll(kr, hd, 1)
    vsw = pltpu.roll(vr, hd, 1)
    low = lax.broadcasted_iota(jnp.int32, (seq, 128), 1) < hd
    body = slice(blk, blk + seq)
    zpad = jnp.zeros((blk, 128), BF16)
    for kh in range(SWA_KV_HEADS):
        for ref in (klo_s, khi_s):
            ref[kh, 0:blk, :] = zpad
            ref[kh, blk + seq:2 * blk + seq, :] = zpad
        vd_s[kh, 0:blk, 0:128] = zpad
        vd_s[kh, blk + seq:2 * blk + seq, 0:128] = zpad
        vd_s[kh, :, 128:256] = jnp.ones((seq + 2 * blk, 128), BF16)
    klo_s[0, body, :] = jnp.where(low, kr, 0.0).astype(BF16)
    khi_s[0, body, :] = jnp.where(low, 0.0, ksw).astype(BF16)
    klo_s[1, body, :] = jnp.where(low, ksw, 0.0).astype(BF16)
    khi_s[1, body, :] = jnp.where(low, 0.0, kr).astype(BF16)
    vd_s[0, body, 0:128] = jnp.where(low, vr, vsw).astype(BF16)
    vd_s[1, body, 0:128] = jnp.where(low, vsw, vr).astype(BF16)

    qi = lax.broadcasted_iota(jnp.int32, (blk, 3 * blk), 0)
    kk = lax.broadcasted_iota(jnp.int32, (blk, 3 * blk), 1)
    band = jnp.abs(qi + blk - kk) <= SWA_WINDOW
    low_o = lax.broadcasted_iota(jnp.int32, (blk, 128), 1) < hd
    rows = SWA_GROUP * blk
    chunks = [(kh, rb) for kh in range(SWA_KV_HEADS) for rb in range(rows // SWA_ROWS)]

    def scores(n, s_s):
        r0 = pl.multiple_of(n * blk, blk)
        for kh in range(SWA_KV_HEADS):
            kcat = jnp.concatenate([klo_s[kh, pl.ds(r0, 3 * blk), :], khi_s[kh, pl.ds(r0, 3 * blk), :]], axis=0)
            for pr in range(pairs):
                qq = q_s[kh * pairs + pr, pl.ds(r0, blk), :]
                s2 = lax.dot_general(qq, kcat, _NT, preferred_element_type=F32)
                s_s[kh, (2 * pr) * blk:(2 * pr + 1) * blk, :] = s2[:, :3 * blk]
                s_s[kh, (2 * pr + 1) * blk:(2 * pr + 2) * blk, :] = s2[:, 3 * blk:]

    nblk = seq // blk

    def softmax_block(n, s_s, p_s, m_s):
        kpos = kk + (n - 1) * blk
        bias_s[...] = jnp.where(band & (kpos >= 0) & (kpos < seq), 0.0, NEG_BIG)
        for kh, rb in chunks:
            rs = slice(rb * SWA_ROWS, (rb + 1) * SWA_ROWS)
            bs = slice((rb * SWA_ROWS) % blk, (rb * SWA_ROWS) % blk + SWA_ROWS)
            sink2 = sink_ref[kh * SWA_GROUP + (rb * SWA_ROWS) // blk] * LOG2E
            lo = s_s[kh, rs, 0:blk] + bias_s[bs, 0:blk]
            hi = s_s[kh, rs, 2 * blk:3 * blk] + bias_s[bs, 2 * blk:3 * blk]
            s_s[kh, rs, 0:blk] = lo
            s_s[kh, rs, 2 * blk:3 * blk] = hi
            top = jnp.maximum(jnp.maximum(lo, s_s[kh, rs, blk:2 * blk]), hi)
            m_s[kh, rs, :] = jnp.broadcast_to(jnp.maximum(jnp.max(top, axis=1, keepdims=True), sink2), (SWA_ROWS, 128))
        for kh, rb in chunks:
            rs = slice(rb * SWA_ROWS, (rb + 1) * SWA_ROWS)
            p = jnp.exp2(s_s[kh, rs, :] - jnp.concatenate([m_s[kh, rs, :]] * 3, axis=1))
            p_s[kh, rs, :] = p.astype(BF16)

    def output_block(n, p_s, m_s):
        r0 = pl.multiple_of(n * blk, blk)
        for kh in range(SWA_KV_HEADS):
            o3 = _mm(p_s[kh], vd_s[kh, pl.ds(r0, 3 * blk), :])
            sink2 = jnp.concatenate([jnp.full((blk, 128), sink_ref[kh * SWA_GROUP + g] * LOG2E, F32)
                                     for g in range(SWA_GROUP)], axis=0)
            o2 = o3[:, 0:128] / (o3[:, 128:256] + jnp.exp2(sink2 - m_s[kh]))
            o_ref[0, pl.ds(r0, blk), kh * SWA_GROUP * hd:(kh + 1) * SWA_GROUP * hd] = jnp.concatenate(
                [jnp.where(low_o, o2[(2 * pr) * blk:(2 * pr + 1) * blk], o2[(2 * pr + 1) * blk:(2 * pr + 2) * blk])
                 for pr in range(pairs)], axis=1)

    def body_fn(t, carry):
        n = 2 * t + 1
        output_block(n - 1, pa_s, ma_s)
        scores(n + 1, sa_s)
        softmax_block(n, sb_s, pb_s, mb_s)
        output_block(n, pb_s, mb_s)
        scores(jnp.minimum(n + 2, nblk - 1), sb_s)
        softmax_block(n + 1, sa_s, pa_s, ma_s)
        return carry

    scores(0, sa_s)
    scores(1, sb_s)
    softmax_block(0, sa_s, pa_s, ma_s)
    lax.fori_loop(0, (nblk - 2) // 2, body_fn, 0)
    softmax_block(nblk - 1, sb_s, pb_s, mb_s)
    output_block(nblk - 2, pa_s, ma_s)
    output_block(nblk - 1, pb_s, mb_s)


def _swa(a3, sink, tables):
    b, seq, _ = a3.shape
    cos_t, sin_t, rot = tables
    tab = pl.BlockSpec((seq, 128), lambda i, s: (0, 0))
    rot_spec = pl.BlockSpec((128, 128), lambda i, s: (0, 0))
    banded = pltpu.VMEM((SWA_KV_HEADS, seq + 2 * SWA_BLOCK, 128), BF16)
    stat = pltpu.VMEM((SWA_KV_HEADS, SWA_GROUP * SWA_BLOCK, 128), F32)
    grid_spec = pltpu.PrefetchScalarGridSpec(
        num_scalar_prefetch=1,
        grid=(b,),
        in_specs=[pl.BlockSpec((1, seq, A_COLS), lambda i, s: (i, 0, 0)), tab, tab, rot_spec],
        out_specs=pl.BlockSpec((1, seq, SWA_WIDTH), lambda i, s: (i, 0, 0)),
        scratch_shapes=[pltpu.VMEM((SWA_WIDTH // 128, seq, 128), BF16), banded, banded,
                        pltpu.VMEM((SWA_KV_HEADS, seq + 2 * SWA_BLOCK, 256), BF16),
                        pltpu.VMEM((SWA_KV_HEADS, SWA_GROUP * SWA_BLOCK, 3 * SWA_BLOCK), F32),
                        pltpu.VMEM((SWA_KV_HEADS, SWA_GROUP * SWA_BLOCK, 3 * SWA_BLOCK), F32),
                        pltpu.VMEM((SWA_KV_HEADS, SWA_GROUP * SWA_BLOCK, 3 * SWA_BLOCK), BF16),
                        pltpu.VMEM((SWA_KV_HEADS, SWA_GROUP * SWA_BLOCK, 3 * SWA_BLOCK), BF16),
                        pltpu.VMEM((SWA_BLOCK, 3 * SWA_BLOCK), F32), stat, stat],
    )
    return pl.pallas_call(
        _swa_kernel,
        grid_spec=grid_spec,
        out_shape=jax.ShapeDtypeStruct((b, seq, SWA_WIDTH), F32),
        compiler_params=_params("parallel"),
        name="swa",
    )(sink, a3, cos_t, sin_t, rot)


def _layer_norm(x, g, b):
    mu = jnp.mean(x, axis=-1, keepdims=True)
    d = x - mu
    var = jnp.mean(d * d, axis=-1, keepdims=True)
    return d * lax.rsqrt(var + LN_EPS) * g + b


def _out_ffn_kernel(x_ref, yal_ref, yah_ref, yb_ref, yc_ref, wo_ref,
                    g1_ref, b1_ref, w1_ref, w2_ref, g2_ref, b2_ref, o_ref, acc_ref, x1b_ref, *, alpha):
    subs = [slice(r, r + FFN_SUB) for r in range(0, x_ref.shape[0], FFN_SUB)]
    for rows in subs:
        ya = jnp.concatenate([yal_ref[rows, :], yah_ref[rows, :]], axis=1)
        mix = (_mm(ya.astype(BF16), wo_ref[0:S5_WIDTH, :])
               + _mm(yb_ref[rows, :].astype(BF16), wo_ref[S5_WIDTH:S5_WIDTH + GLA_WIDTH, :])
               + _mm(yc_ref[rows, :].astype(BF16), wo_ref[S5_WIDTH + GLA_WIDTH:D_MODEL, :]))
        x1 = _layer_norm(alpha * x_ref[rows, :] + mix, g1_ref[...], b1_ref[...])
        o_ref[rows, :] = x1
        x1b_ref[rows, :] = x1.astype(BF16)
    for rows in subs:
        for ci in range(D_FF // FF_CHUNK):
            sl = slice(ci * FF_CHUNK, (ci + 1) * FF_CHUNK)
            hid = jnp.maximum(_mm(x1b_ref[rows, :], w1_ref[:, sl]), 0.0)
            part = _mm((hid * hid).astype(BF16), w2_ref[sl, :])
            if ci == 0:
                acc_ref[rows, :] = part
            else:
                acc_ref[rows, :] += part
    for rows in subs:
        o_ref[rows, :] = _layer_norm(alpha * o_ref[rows, :] + acc_ref[rows, :], g2_ref[...], b2_ref[...])


def _out_ffn(x2d, yal2d, yah2d, yb2d, yc2d, wo, g1, b1, w1, w2, g2, b2, alpha):
    t = x2d.shape[0]
    tm = min(FFN_TILE, t)
    row = lambda n: pl.BlockSpec((tm, n), lambda i: (i, 0))
    once = lambda shape: pl.BlockSpec(shape, lambda i: (0, 0), pipeline_mode=pl.Buffered(1))
    return pl.pallas_call(
        functools.partial(_out_ffn_kernel, alpha=alpha),
        grid=(t // tm,),
        in_specs=[row(D_MODEL), row(128), row(128), row(GLA_WIDTH), row(SWA_WIDTH),
                  once((D_MODEL, D_MODEL)), once((1, D_MODEL)), once((1, D_MODEL)),
                  once((D_MODEL, D_FF)), once((D_FF, D_MODEL)),
                  once((1, D_MODEL)), once((1, D_MODEL))],
        out_specs=row(D_MODEL),
        out_shape=jax.ShapeDtypeStruct((t, D_MODEL), F32),
        scratch_shapes=[pltpu.VMEM((tm, D_MODEL), F32), pltpu.VMEM((tm, D_MODEL), BF16)],
        compiler_params=_params("parallel"),
        name="out_ffn",
    )(x2d, yal2d, yah2d, yb2d, yc2d, wo, g1, b1, w1, w2, g2, b2)


def _pack_w_in(w_in):
    points = np.cumsum([S5_WIDTH, GLA_QK, GLA_QK, GLA_WIDTH, GLA_WIDTH, GLA_RANK, GLA_RANK,
                        SWA_WIDTH, SWA_KV]).tolist()
    s5, gq, gk, gv, gr, glf, glb, aq, ak, av = jnp.split(w_in, points, axis=-1)
    pad = jnp.zeros(w_in.shape[:-1] + (128 - 2 * GLA_RANK,), w_in.dtype)
    return jnp.concatenate([s5, gq, gk, gv, gr, glf, glb, pad, aq, ak, av], axis=-1).astype(BF16)


def _pad_gate_w(w_a):
    z = jnp.zeros((128, GLA_QK), F32)
    waf = z.at[0:GLA_RANK].set(w_a[0].astype(F32))
    wab = z.at[GLA_RANK:2 * GLA_RANK].set(w_a[1].astype(F32))
    return waf.astype(BF16), wab.astype(BF16)


def _layer(x2d, bsz, seq, alpha, w_packed, a_re, a_im, log_step, b_re, b_im, c_re, c_im, d_skip,
           w_glu, b_glu, gla_w_a, gla_b_a, gla_ln_g, swa_sink, w_out, ln1_g, ln1_b, w_ff1, w_ff2,
           ln2_g, ln2_b, gla_consts, rope_tabs):
    t = bsz * seq
    row = lambda v: v.astype(F32)[None, :]
    ul, uh, g, a = _inproj(x2d, w_packed)

    mats = _s5_matrices(a_re, a_im, log_step, b_re, b_im, c_re, c_im, d_skip, seq // S5_CHUNK)
    yal, yah = _s5(ul.reshape(bsz, seq, 128), uh.reshape(bsz, seq, 128), mats, w_glu, row(b_glu))

    waf, wab = _pad_gate_w(gla_w_a)
    yb = _gla(g.reshape(bsz, seq, G_COLS), waf, wab, gla_b_a.astype(F32), gla_consts, row(gla_ln_g))
    yc = _swa(a.reshape(bsz, seq, A_COLS), swa_sink.astype(F32), rope_tabs)

    return _out_ffn(x2d, yal.reshape(t, 128), yah.reshape(t, 128), yb.reshape(t, GLA_WIDTH), yc.reshape(t, SWA_WIDTH),
                    w_out, row(ln1_g), row(ln1_b), w_ff1, w_ff2, row(ln2_g), row(ln2_b), alpha)


def kernel(x, w_in, s5_a_re, s5_a_im, s5_log_step, s5_b_re, s5_b_im, s5_c_re, s5_c_im, s5_d, s5_w_glu, s5_b_glu, gla_w_a, gla_b_a, gla_ln_g, swa_sink, w_out, ln1_g, ln1_b, w_ff1, w_ff2, ln2_g, ln2_b):
    bsz, seq, _ = x.shape
    depth = w_in.shape[0]
    alpha = (2 * depth) ** 0.25
    gla_consts = _gla_constants()
    rope_tabs = _rope_tables(seq)
    w_packed = _pack_w_in(w_in)
    s5_w_glu, w_out, w_ff1, w_ff2 = (w.astype(BF16) for w in (s5_w_glu, w_out, w_ff1, w_ff2))
    h = x.reshape(bsz * seq, D_MODEL)
    for l in range(depth):
        h = _layer(h, bsz, seq, alpha, w_packed[l], s5_a_re[l], s5_a_im[l], s5_log_step[l],
                   s5_b_re[l], s5_b_im[l], s5_c_re[l], s5_c_im[l], s5_d[l], s5_w_glu[l], s5_b_glu[l],
                   gla_w_a[l], gla_b_a[l], gla_ln_g[l], swa_sink[l], w_out[l], ln1_g[l], ln1_b[l],
                   w_ff1[l], w_ff2[l], ln2_g[l], ln2_b[l], gla_consts, rope_tabs)
    return h.reshape(bsz, seq, D_MODEL)
```

```python
import functools
import math

import jax
import jax.numpy as jnp
import numpy as np
from jax import lax
from jax.experimental import pallas as pl
from jax.experimental.pallas import tpu as pltpu

F32 = jnp.float32
BF16 = jnp.bfloat16

D_MODEL = 1024
S5_WIDTH = 256
S5_GROUP = 16
S5_GROUPS = 16
S5_STATE = 64
S5_CHUNK = 16
GLA_HEADS = 4
GLA_DK = 32
GLA_DV = 64
GLA_QK = 128
GLA_WIDTH = 256
GLA_RANK = 16
GLA_TAU = 16.0
GLA_CHUNK = 64
GLA_GROUP = 16
SWA_WIDTH = 512
SWA_HEAD_DIM = 64
SWA_HEADS = 8
SWA_KV_HEADS = 2
SWA_GROUP = SWA_HEADS // SWA_KV_HEADS
SWA_KV = 128
SWA_WINDOW = 128
SWA_BLOCK = 128
ROT_DIM = 16
ROPE_THETA = 500000.0
D_FF = 4096
LN_EPS = 1e-5
NEG_BIG = -1e30
LOG2E = 1.4426950408889634

U_COLS = S5_WIDTH
G_COLS = GLA_QK + GLA_QK + GLA_WIDTH + GLA_WIDTH + 128
A_COLS = SWA_WIDTH + SWA_KV + SWA_KV
IN_COLS = U_COLS + G_COLS + A_COLS

VMEM_LIMIT = 56 * 1024 * 1024
ROW_TILE = 1024
FF_CHUNK = 512
FFN_TILE = 1024
FFN_SUB = 512
SWA_ROWS = 64

_NT = (((1,), (1,)), ((), ()))
_TN = (((0,), (0,)), ((), ()))


def _mm(a, b):
    return jnp.dot(a, b, preferred_element_type=F32)


def _params(*sem):
    return pltpu.CompilerParams(dimension_semantics=sem, vmem_limit_bytes=VMEM_LIMIT)


def _resident(shape):
    nd = len(shape)
    return pl.BlockSpec(shape, lambda *_: (0,) * nd)


def _inproj_kernel(x_ref, w_ref, ul_ref, uh_ref, g_ref, a_ref):
    x = x_ref[...].astype(BF16)
    u = _mm(x, w_ref[:, 0:U_COLS])
    ul_ref[...] = u[:, :128]
    uh_ref[...] = u[:, 128:]
    g_ref[...] = _mm(x, w_ref[:, U_COLS:U_COLS + G_COLS])
    a_ref[...] = _mm(x, w_ref[:, U_COLS + G_COLS:IN_COLS])


def _inproj(x2d, w_packed):
    t = x2d.shape[0]
    tm = min(ROW_TILE, t)
    row = lambda n: pl.BlockSpec((tm, n), lambda i: (i, 0))
    return pl.pallas_call(
        _inproj_kernel,
        grid=(t // tm,),
        in_specs=[row(D_MODEL), _resident((D_MODEL, IN_COLS))],
        out_specs=[row(128), row(128), row(G_COLS), row(A_COLS)],
        out_shape=[jax.ShapeDtypeStruct((t, 128), F32), jax.ShapeDtypeStruct((t, 128), F32),
                   jax.ShapeDtypeStruct((t, G_COLS), F32), jax.ShapeDtypeStruct((t, A_COLS), F32)],
        compiler_params=_params("parallel"),
        name="inproj",
    )(x2d, w_packed)


def _s5_matrices(a_re, a_im, log_step, b_re, b_im, c_re, c_im, d_skip, nch):
    c, ng, w = S5_CHUNK, S5_GROUPS, S5_CHUNK * S5_GROUP
    ar, ai = a_re.astype(F32), a_im.astype(F32)
    step = jnp.exp(log_step.astype(F32))
    lr, li = ar * step, ai * step

    def lam_pow(z, k):
        kk = jnp.asarray(k, F32)[None, :, None, None]
        mag = jnp.exp(kk * lr[z][:, None, None, :])
        ang = kk * li[z][:, None, None, :]
        return mag * jnp.cos(ang), mag * jnp.sin(ang)

    def cmul(xr, xi, yr, yi):
        return xr * yr - xi * yi, xr * yi + xi * yr

    l1r, l1i = jnp.exp(lr) * jnp.cos(li), jnp.exp(lr) * jnp.sin(li)
    nr, ni = l1r - 1.0, l1i
    den = ar * ar + ai * ai
    qr, qi = (nr * ar + ni * ai) / den, (ni * ar - nr * ai) / den
    bt_r = jnp.swapaxes(b_re.astype(F32), -1, -2)
    bt_i = jnp.swapaxes(b_im.astype(F32), -1, -2)
    bbr, bbi = cmul(qr[:, :, None, :], qi[:, :, None, :], bt_r, bt_i)
    cr, ci = c_re.astype(F32), c_im.astype(F32)

    idx = np.arange(c)
    flat = lambda t: t.reshape(ng, w, S5_STATE)

    def with_c(z, k):
        pr, pi = lam_pow(z, k)
        xr, xi = cmul(pr, pi, cr[z][:, None], ci[z][:, None])
        return flat(xr), flat(xi)

    def with_b(z, k):
        pr, pi = lam_pow(z, k)
        yr, yi = cmul(pr, pi, bbr[z][:, None], bbi[z][:, None])
        return flat(yr), flat(yi)

    def re_prod(x, y):
        return jnp.einsum('gmk,gnk->gmn', jnp.concatenate([x[0], -x[1]], axis=-1),
                          jnp.concatenate([y[0], y[1]], axis=-1), precision=lax.Precision.HIGHEST)

    ri = np.arange(w)[:, None] // S5_GROUP
    cj = np.arange(w)[None, :] // S5_GROUP
    tt_mat = (jnp.where(ri >= cj, re_prod(with_c(0, idx), with_b(0, -idx)), 0.0)
              + jnp.where(cj >= ri, re_prod(with_c(1, -idx), with_b(1, idx)), 0.0))

    sf, sb = with_b(0, c - 1 - idx), with_b(1, idx)
    s_mat = jnp.concatenate([sf[0], sb[0], sf[1], sb[1]], axis=-1)
    of, ob = with_c(0, idx + 1), with_c(1, c - idx)
    ot_mat = jnp.concatenate([of[0], ob[0], -of[1], -ob[1]], axis=-1)

    pcr = [lam_pow(z, [c]) for z in range(2)]
    a_r = jnp.concatenate([pcr[0][0][:, 0, 0, :], pcr[1][0][:, 0, 0, :]], axis=-1)
    a_i = jnp.concatenate([pcr[0][1][:, 0, 0, :], pcr[1][1][:, 0, 0, :]], axis=-1)
    d_col = jnp.broadcast_to(jnp.tile(d_skip.astype(F32), (1, c))[:, :, None], (ng, w, nch))
    return tt_mat.astype(BF16), s_mat.astype(BF16), ot_mat.astype(BF16), a_r, a_i, d_col


def _gelu_tanh(x):
    return 0.5 * x * (1.0 + jnp.tanh(math.sqrt(2.0 / math.pi) * (x + 0.044715 * (x * x * x))))


def _sigmoid(x):
    return 1.0 / (1.0 + jnp.exp(-x))


def _s5_kernel(ul_ref, uh_ref, tt_ref, s_ref, ot_ref, ar_ref, ai_ref, dcol_ref, wglu_ref, bglu_ref,
               yl_ref, yh_ref, d_scr, cr_scr, ci_scr, efr_scr, efi_scr, ebr_scr, ebi_scr, zt_scr):
    c, ng = S5_CHUNK, S5_GROUPS
    nch = ul_ref.shape[1] // c
    for j in range(c):
        tok = pl.ds(j, nch, stride=c)
        uj = jnp.concatenate([ul_ref[0, tok, :], uh_ref[0, tok, :]], axis=1)
        d_scr[:, j * S5_GROUP:(j + 1) * S5_GROUP, :] = uj.T.reshape(ng, S5_GROUP, nch)
    contrib = [lax.dot_general(d_scr[g].astype(BF16), s_ref[g], _TN, preferred_element_type=F32)
               for g in range(ng)]
    for g in range(ng):
        grp = pl.ds(g, nch, stride=ng)
        cr_scr[grp, :] = contrib[g][:, :128]
        ci_scr[grp, :] = contrib[g][:, 128:]

    ar, ai = ar_ref[...], ai_ref[...]
    fwd = lax.broadcasted_iota(jnp.int32, (ng, 128), 1) < S5_STATE

    def step(k, carry):
        sre, sim = carry
        up = pl.ds(pl.multiple_of(k * ng, ng), ng)
        dn = pl.ds(pl.multiple_of((nch - 1 - k) * ng, ng), ng)
        efr_scr[up, :] = sre
        efi_scr[up, :] = sim
        ebr_scr[dn, :] = sre
        ebi_scr[dn, :] = sim
        cre = jnp.where(fwd, cr_scr[up, :], cr_scr[dn, :])
        cim = jnp.where(fwd, ci_scr[up, :], ci_scr[dn, :])
        return ar * sre - ai * sim + cre, ar * sim + ai * sre + cim

    zero = jnp.zeros((ng, 128), F32)
    lax.fori_loop(0, nch, step, (zero, zero), unroll=8)

    fwd_all = lax.broadcasted_iota(jnp.int32, (nch, 128), 1) < S5_STATE
    enter = []
    for g in range(ng):
        grp = pl.ds(g, nch, stride=ng)
        enter.append(jnp.concatenate([jnp.where(fwd_all, efr_scr[grp, :], ebr_scr[grp, :]),
                                      jnp.where(fwd_all, efi_scr[grp, :], ebi_scr[grp, :])],
                                     axis=1).astype(BF16))
    yt = [_mm(tt_ref[g], d_scr[g].astype(BF16)) + lax.dot_general(ot_ref[g], enter[g], _NT, preferred_element_type=F32)
          for g in range(ng)]
    for g in range(ng):
        zt_scr[:, g * S5_GROUP:(g + 1) * S5_GROUP, :] = _gelu_tanh(yt[g] + dcol_ref[g] * d_scr[g]).reshape(c, S5_GROUP, nch)
    glu = [lax.dot_general(zt_scr[i].astype(BF16), wglu_ref[...], _TN, preferred_element_type=F32)
           for i in range(c)]
    for i in range(c):
        gl = glu[i] + bglu_ref[...]
        ya = gl[:, :S5_WIDTH] * _sigmoid(gl[:, S5_WIDTH:])
        tok = pl.ds(i, nch, stride=c)
        yl_ref[0, tok, :] = ya[:, :128]
        yh_ref[0, tok, :] = ya[:, 128:]


def _s5(ul3, uh3, mats, wglu, bglu):
    b, seq, _ = ul3.shape
    nch = seq // S5_CHUNK
    tt_mat, s_mat, ot_mat, a_r, a_i, d_col = mats
    w = S5_CHUNK * S5_GROUP
    half = pl.BlockSpec((1, seq, 128), lambda i: (i, 0, 0))
    state_scr = pltpu.VMEM((S5_GROUPS * nch, 2 * S5_STATE), F32)
    return pl.pallas_call(
        _s5_kernel,
        grid=(b,),
        in_specs=[half, half,
                  _resident((S5_GROUPS, w, w)), _resident((S5_GROUPS, w, 4 * S5_STATE)),
                  _resident((S5_GROUPS, w, 4 * S5_STATE)),
                  _resident((S5_GROUPS, 128)), _resident((S5_GROUPS, 128)),
                  _resident((S5_GROUPS, w, nch)),
                  _resident((S5_WIDTH, 2 * S5_WIDTH)), _resident((1, 2 * S5_WIDTH))],
        out_specs=[half, half],
        out_shape=[jax.ShapeDtypeStruct((b, seq, 128), F32)] * 2,
        scratch_shapes=[pltpu.VMEM((S5_GROUPS, w, nch), F32)] + [state_scr] * 6
                       + [pltpu.VMEM((S5_CHUNK, w, nch), F32)],
        compiler_params=_params("parallel"),
        name="s5",
    )(ul3, uh3, tt_mat, s_mat, ot_mat, a_r, a_i, d_col, wglu, bglu)


def _gla_constants():
    r = np.arange(256)
    same = (r[:, None] // GLA_CHUNK) == (r[None, :] // GLA_CHUNK)
    prefix = same & (r[None, :] <= r[:, None])
    avg = ((r[:, None] // GLA_DV) == (r[None, :] // GLA_DV)) / float(GLA_DV)
    as_bf = lambda m: jnp.asarray(m.astype(np.float32), BF16)
    return as_bf(prefix), as_bf(avg)


def _log_sigmoid(x):
    return jnp.minimum(x, 0.0) - jnp.log(1.0 + jnp.exp(-jnp.abs(x)))


def _split_mm(a, b):
    hi = a.astype(BF16)
    lo = (a - hi.astype(F32)).astype(BF16)
    return _mm(hi, b) + _mm(lo, b)


def _chunk_sums(mats, la):
    hi = la.astype(BF16)
    lo = (la - hi.astype(F32)).astype(BF16)
    outs = []
    for mat in mats:
        blocks = []
        for i in range(la.shape[0] // 256):
            sl = slice(i * 256, (i + 1) * 256)
            blocks.append(_mm(mat, hi[sl]) + _mm(mat, lo[sl]))
        outs.append(jnp.concatenate(blocks, axis=0))
    return outs


def _chunk_row(x, row):
    x3 = x.reshape(x.shape[0] // GLA_CHUNK, GLA_CHUNK, x.shape[1])
    return jnp.broadcast_to(x3[:, row:row + 1, :], x3.shape).reshape(x.shape)


def _gla_kernel(g_ref, wa_ref, ba_ref, pre_ref, avg_ref, lng_ref, o_ref,
                q_s, kin_s, kst_s, v_s, totf_s, totb_s, o_s, kv_s, st_s):
    seq = g_ref.shape[1]
    nchunk = seq // GLA_CHUNK
    c = GLA_CHUNK
    grp = math.gcd(GLA_GROUP, nchunk)
    q = g_ref[0, :, 0:128] * (GLA_DK ** -0.5)
    k = g_ref[0, :, 128:256]
    lfb = g_ref[0, :, 768:896].astype(BF16)
    la = _log_sigmoid(_mm(lfb, wa_ref[...]) + ba_ref[...]) * (LOG2E / GLA_TAU)
    (pre,) = _chunk_sums((pre_ref[...],), la)
    tot = _chunk_row(pre, c - 1)
    bf, tf = pre[:, 0:128], tot[:, 0:128]
    q_s[:, 0:128] = (q * jnp.exp2(bf)).astype(BF16)
    kin_s[:, 0:128] = (k * jnp.exp2(-bf)).astype(BF16)
    kst_s[:, 0:128] = (k * jnp.exp2(tf - bf)).astype(BF16)
    totf_s[...] = tf
    tb = tot[:, 128:256]
    bs = tb - pre[:, 128:256] + la[:, 128:256]
    q_s[:, 128:256] = (q * jnp.exp2(bs)).astype(BF16)
    kin_s[:, 128:256] = (k * jnp.exp2(-bs)).astype(BF16)
    kst_s[:, 128:256] = (k * jnp.exp2(tb - bs)).astype(BF16)
    totb_s[...] = tb
    v_s[...] = g_ref[0, :, 256:512].astype(BF16)

    r256 = lax.broadcasted_iota(jnp.int32, (4 * c, 256), 0)
    l256 = lax.broadcasted_iota(jnp.int32, (4 * c, 256), 1)
    head_k = (r256 // c) == ((l256 % 128) // GLA_DK)
    head_v = (r256 // c) == (l256 // GLA_DV)
    qi = lax.broadcasted_iota(jnp.int32, (c, 4 * c), 0)
    kj = lax.broadcasted_iota(jnp.int32, (c, 4 * c), 1) % c
    zero_bf = jnp.zeros((), BF16)

    def local_group(it, carry):
        starts = [pl.multiple_of((it * grp + u) * c, c) for u in range(grp)]
        qq = [q_s[pl.ds(r0, c), :] for r0 in starts]
        vv = [v_s[pl.ds(r0, c), :] for r0 in starts]
        kexp = [jnp.where(head_k, jnp.concatenate([kin_s[pl.ds(r0, c), :]] * 4, axis=0), zero_bf) for r0 in starts]
        sc_f = [lax.dot_general(q[:, 0:128], k[:, 0:128], _NT, preferred_element_type=F32) for q, k in zip(qq, kexp)]
        sc_b = [lax.dot_general(q[:, 128:256], k[:, 128:256], _NT, preferred_element_type=F32) for q, k in zip(qq, kexp)]
        kv_t = [lax.dot_general(v, kst_s[pl.ds(r0, c), :], _TN, preferred_element_type=F32) for v, r0 in zip(vv, starts)]
        for u, r0 in enumerate(starts):
            vexp = jnp.where(head_v, jnp.concatenate([vv[u]] * 4, axis=0), zero_bf)
            pm = jnp.where(kj <= qi, sc_f[u], sc_b[u]).astype(BF16)
            o_s[pl.ds(r0, c), :] = _mm(pm, vexp)
            kv_s[it * grp + u] = jnp.where(head_k, kv_t[u], 0.0)
        return carry

    lax.fori_loop(0, nchunk // grp, local_group, 0)

    def carry_states(n, carry):
        st_f, st_b = carry
        m = nchunk - 1 - n
        st_s[n, :, 0:128] = st_f.astype(BF16)
        st_s[m, :, 128:256] = st_b.astype(BF16)
        st_f = jnp.exp2(totf_s[pl.ds(n * c, 1), :]) * st_f + kv_s[n, :, 0:128]
        st_b = jnp.exp2(totb_s[pl.ds(m * c, 1), :]) * st_b + kv_s[m, :, 128:256]
        return st_f, st_b

    zero = jnp.zeros((GLA_WIDTH, GLA_QK), F32)
    lax.fori_loop(0, nchunk, carry_states, (zero, zero))

    def cross_group(it, carry):
        starts = [pl.multiple_of((it * grp + u) * c, c) for u in range(grp)]
        upd = [lax.dot_general(q_s[pl.ds(r0, c), :], st_s[it * grp + u], _NT, preferred_element_type=F32)
               for u, r0 in enumerate(starts)]
        for r0, d in zip(starts, upd):
            o_s[pl.ds(r0, c), :] += d
        return carry

    lax.fori_loop(0, nchunk // grp, cross_group, 0)

    o = o_s[...]
    avg = avg_ref[...]
    mu = _split_mm(o, avg)
    dlt = o - mu
    var = _split_mm(dlt * dlt, avg)
    on = dlt * lax.rsqrt(var + LN_EPS)
    r = g_ref[0, :, 512:768]
    o_ref[0] = on * lng_ref[...] * (r * _sigmoid(r))


def _gla(g3, wa, ba, consts, ln_g):
    b, seq, _ = g3.shape
    pre, avg = consts
    nchunk = seq // GLA_CHUNK
    both = pltpu.VMEM((seq, 2 * GLA_QK), BF16)
    return pl.pallas_call(
        _gla_kernel,
        grid=(b,),
        in_specs=[pl.BlockSpec((1, seq, G_COLS), lambda i: (i, 0, 0)),
                  _resident((128, 2 * GLA_QK)), _resident((1, 2 * GLA_QK)),
                  _resident((256, 256)), _resident((256, 256)), _resident((1, 256))],
        out_specs=pl.BlockSpec((1, seq, GLA_WIDTH), lambda i: (i, 0, 0)),
        out_shape=jax.ShapeDtypeStruct((b, seq, GLA_WIDTH), F32),
        scratch_shapes=[both, both, both, pltpu.VMEM((seq, GLA_WIDTH), BF16),
                        pltpu.VMEM((seq, GLA_QK), F32), pltpu.VMEM((seq, GLA_QK), F32),
                        pltpu.VMEM((seq, GLA_WIDTH), F32),
                        pltpu.VMEM((nchunk, GLA_WIDTH, 2 * GLA_QK), F32),
                        pltpu.VMEM((nchunk, GLA_WIDTH, 2 * GLA_QK), BF16)],
        compiler_params=_params("parallel"),
        name="gla",
    )(g3, wa, ba, pre, avg, ln_g)


def _rope_tables(seq):
    pos = jnp.arange(seq, dtype=F32)
    inv_freq = ROPE_THETA ** (-jnp.arange(0, ROT_DIM, 2, dtype=F32) / ROT_DIM)
    ang = pos[:, None] * inv_freq[None, :]
    cos, sin = jnp.cos(ang), jnp.sin(ang)
    half = ROT_DIM // 2
    pad = jnp.zeros((seq, SWA_HEAD_DIM - ROT_DIM), F32)
    c_head = jnp.concatenate([cos, cos, pad + 1.0], axis=1)
    s_head = jnp.concatenate([sin, sin, pad], axis=1)
    two = lambda t: jnp.concatenate([t, t], axis=1)
    rot = np.zeros((128, 128), np.float32)
    for base in (0, SWA_HEAD_DIM):
        for d in range(half):
            rot[base + d + half, base + d] = -1.0
            rot[base + d, base + d + half] = 1.0
    return two(c_head), two(s_head), jnp.asarray(rot, BF16)


def _swa_kernel(sink_ref, a_ref, cos_ref, sin_ref, rot_ref, o_ref,
                q_s, klo_s, khi_s, vd_s, sa_s, sb_s, pa_s, pb_s, bias_s, ma_s, mb_s):
    seq = a_ref.shape[1]
    blk = SWA_BLOCK
    hd = SWA_HEAD_DIM
    pairs = SWA_GROUP // 2
    cosv, sinv, rot = cos_ref[...], sin_ref[...], rot_ref[...]

    def rope(x):
        return x * cosv + _split_mm(x, rot) * sinv

    for slab in range(SWA_WIDTH // 128):
        q_s[slab] = (rope(a_ref[0, :, slab * 128:(slab + 1) * 128]) * (hd ** -0.5 * LOG2E)).astype(BF16)
    kr = rope(a_ref[0, :, SWA_WIDTH:SWA_WIDTH + SWA_KV])
    vr = a_ref[0, :, SWA_WIDTH + SWA_KV:A_COLS]
    ksw = pltpu.roll(kr, hd, 1)
    vsw = pltpu.roll(vr, hd, 1)
    low = lax.broadcasted_iota(jnp.int32, (seq, 128), 1) < hd
    body = slice(blk, blk + seq)
    zpad = jnp.zeros((blk, 128), BF16)
    for kh in range(SWA_KV_HEADS):
        for ref in (klo_s, khi_s):
            ref[kh, 0:blk, :] = zpad
            ref[kh, blk + seq:2 * blk + seq, :] = zpad
        vd_s[kh, 0:blk, 0:128] = zpad
        vd_s[kh, blk + seq:2 * blk + seq, 0:128] = zpad
        vd_s[kh, :, 128:256] = jnp.ones((seq + 2 * blk, 128), BF16)
    klo_s[0, body, :] = jnp.where(low, kr, 0.0).astype(BF16)
    khi_s[0, body, :] = jnp.where(low, 0.0, ksw).astype(BF16)
    klo_s[1, body, :] = jnp.where(low, ksw, 0.0).astype(BF16)
    khi_s[1, body, :] = jnp.where(low, 0.0, kr).astype(BF16)
    vd_s[0, body, 0:128] = jnp.where(low, vr, vsw).astype(BF16)
    vd_s[1, body, 0:128] = jnp.where(low, vsw, vr).astype(BF16)

    qi = lax.broadcasted_iota(jnp.int32, (blk, 3 * blk), 0)
    kk = lax.broadcasted_iota(jnp.int32, (blk, 3 * blk), 1)
    band = jnp.abs(qi + blk - kk) <= SWA_WINDOW
    low_o = lax.broadcasted_iota(jnp.int32, (blk, 128), 1) < hd
    rows = SWA_GROUP * blk
    chunks = [(kh, rb) for kh in range(SWA_KV_HEADS) for rb in range(rows // SWA_ROWS)]

    def scores(n, s_s):
        r0 = pl.multiple_of(n * blk, blk)
        for kh in range(SWA_KV_HEADS):
            kcat = jnp.concatenate([klo_s[kh, pl.ds(r0, 3 * blk), :], khi_s[kh, pl.ds(r0, 3 * blk), :]], axis=0)
            for pr in range(pairs):
                qq = q_s[kh * pairs + pr, pl.ds(r0, blk), :]
                s2 = lax.dot_general(qq, kcat, _NT, preferred_element_type=F32)
                s_s[kh, (2 * pr) * blk:(2 * pr + 1) * blk, :] = s2[:, :3 * blk]
                s_s[kh, (2 * pr + 1) * blk:(2 * pr + 2) * blk, :] = s2[:, 3 * blk:]

    nblk = seq // blk

    def softmax_block(n, s_s, p_s, m_s):
        kpos = kk + (n - 1) * blk
        bias_s[...] = jnp.where(band & (kpos >= 0) & (kpos < seq), 0.0, NEG_BIG)
        for kh, rb in chunks:
            rs = slice(rb * SWA_ROWS, (rb + 1) * SWA_ROWS)
            bs = slice((rb * SWA_ROWS) % blk, (rb * SWA_ROWS) % blk + SWA_ROWS)
            sink2 = sink_ref[kh * SWA_GROUP + (rb * SWA_ROWS) // blk] * LOG2E
            lo = s_s[kh, rs, 0:blk] + bias_s[bs, 0:blk]
            hi = s_s[kh, rs, 2 * blk:3 * blk] + bias_s[bs, 2 * blk:3 * blk]
            s_s[kh, rs, 0:blk] = lo
            s_s[kh, rs, 2 * blk:3 * blk] = hi
            top = jnp.maximum(jnp.maximum(lo, s_s[kh, rs, blk:2 * blk]), hi)
            m_s[kh, rs, :] = jnp.broadcast_to(jnp.maximum(jnp.max(top, axis=1, keepdims=True), sink2), (SWA_ROWS, 128))
        for kh, rb in chunks:
            rs = slice(rb * SWA_ROWS, (rb + 1) * SWA_ROWS)
            p = jnp.exp2(s_s[kh, rs, :] - jnp.concatenate([m_s[kh, rs, :]] * 3, axis=1))
            p_s[kh, rs, :] = p.astype(BF16)

    def output_block(n, p_s, m_s):
        r0 = pl.multiple_of(n * blk, blk)
        for kh in range(SWA_KV_HEADS):
            o3 = _mm(p_s[kh], vd_s[kh, pl.ds(r0, 3 * blk), :])
            sink2 = jnp.concatenate([jnp.full((blk, 128), sink_ref[kh * SWA_GROUP + g] * LOG2E, F32)
                                     for g in range(SWA_GROUP)], axis=0)
            o2 = o3[:, 0:128] / (o3[:, 128:256] + jnp.exp2(sink2 - m_s[kh]))
            o_ref[0, pl.ds(r0, blk), kh * SWA_GROUP * hd:(kh + 1) * SWA_GROUP * hd] = jnp.concatenate(
                [jnp.where(low_o, o2[(2 * pr) * blk:(2 * pr + 1) * blk], o2[(2 * pr + 1) * blk:(2 * pr + 2) * blk])
                 for pr in range(pairs)], axis=1)

    def body_fn(t, carry):
        n = 2 * t + 1
        output_block(n - 1, pa_s, ma_s)
        scores(n + 1, sa_s)
        softmax_block(n, sb_s, pb_s, mb_s)
        output_block(n, pb_s, mb_s)
        scores(jnp.minimum(n + 2, nblk - 1), sb_s)
        softmax_block(n + 1, sa_s, pa_s, ma_s)
        return carry

    scores(0, sa_s)
    scores(1, sb_s)
    softmax_block(0, sa_s, pa_s, ma_s)
    lax.fori_loop(0, (nblk - 2) // 2, body_fn, 0)
    softmax_block(nblk - 1, sb_s, pb_s, mb_s)
    output_block(nblk - 2, pa_s, ma_s)
    output_block(nblk - 1, pb_s, mb_s)


def _swa(a3, sink, tables):
    b, seq, _ = a3.shape
    cos_t, sin_t, rot = tables
    tab = pl.BlockSpec((seq, 128), lambda i, s: (0, 0))
    rot_spec = pl.BlockSpec((128, 128), lambda i, s: (0, 0))
    banded = pltpu.VMEM((SWA_KV_HEADS, seq + 2 * SWA_BLOCK, 128), BF16)
    stat = pltpu.VMEM((SWA_KV_HEADS, SWA_GROUP * SWA_BLOCK, 128), F32)
    grid_spec = pltpu.PrefetchScalarGridSpec(
        num_scalar_prefetch=1,
        grid=(b,),
        in_specs=[pl.BlockSpec((1, seq, A_COLS), lambda i, s: (i, 0, 0)), tab, tab, rot_spec],
        out_specs=pl.BlockSpec((1, seq, SWA_WIDTH), lambda i, s: (i, 0, 0)),
        scratch_shapes=[pltpu.VMEM((SWA_WIDTH // 128, seq, 128), BF16), banded, banded,
                        pltpu.VMEM((SWA_KV_HEADS, seq + 2 * SWA_BLOCK, 256), BF16),
                        pltpu.VMEM((SWA_KV_HEADS, SWA_GROUP * SWA_BLOCK, 3 * SWA_BLOCK), F32),
                        pltpu.VMEM((SWA_KV_HEADS, SWA_GROUP * SWA_BLOCK, 3 * SWA_BLOCK), F32),
                        pltpu.VMEM((SWA_KV_HEADS, SWA_GROUP * SWA_BLOCK, 3 * SWA_BLOCK), BF16),
                        pltpu.VMEM((SWA_KV_HEADS, SWA_GROUP * SWA_BLOCK, 3 * SWA_BLOCK), BF16),
                        pltpu.VMEM((SWA_BLOCK, 3 * SWA_BLOCK), F32), stat, stat],
    )
    return pl.pallas_call(
        _swa_kernel,
        grid_spec=grid_spec,
        out_shape=jax.ShapeDtypeStruct((b, seq, SWA_WIDTH), F32),
        compiler_params=_params("parallel"),
        name="swa",
    )(sink, a3, cos_t, sin_t, rot)


def _layer_norm(x, g, b):
    mu = jnp.mean(x, axis=-1, keepdims=True)
    d = x - mu
    var = jnp.mean(d * d, axis=-1, keepdims=True)
    return d * lax.rsqrt(var + LN_EPS) * g + b


def _out_ffn_kernel(x_ref, yal_ref, yah_ref, yb_ref, yc_ref, wo_ref,
                    g1_ref, b1_ref, w1_ref, w2_ref, g2_ref, b2_ref, o_ref, acc_ref, x1b_ref, *, alpha):
    subs = [slice(r, r + FFN_SUB) for r in range(0, x_ref.shape[0], FFN_SUB)]
    for rows in subs:
        ya = jnp.concatenate([yal_ref[rows, :], yah_ref[rows, :]], axis=1)
        mix = (_mm(ya.astype(BF16), wo_ref[0:S5_WIDTH, :])
               + _mm(yb_ref[rows, :].astype(BF16), wo_ref[S5_WIDTH:S5_WIDTH + GLA_WIDTH, :])
               + _mm(yc_ref[rows, :].astype(BF16), wo_ref[S5_WIDTH + GLA_WIDTH:D_MODEL, :]))
        x1 = _layer_norm(alpha * x_ref[rows, :] + mix, g1_ref[...], b1_ref[...])
        o_ref[rows, :] = x1
        x1b_ref[rows, :] = x1.astype(BF16)
    for rows in subs:
        for ci in range(D_FF // FF_CHUNK):
            sl = slice(ci * FF_CHUNK, (ci + 1) * FF_CHUNK)
            hid = jnp.maximum(_mm(x1b_ref[rows, :], w1_ref[:, sl]), 0.0)
            part = _mm((hid * hid).astype(BF16), w2_ref[sl, :])
            if ci == 0:
                acc_ref[rows, :] = part
            else:
                acc_ref[rows, :] += part
    for rows in subs:
        o_ref[rows, :] = _layer_norm(alpha * o_ref[rows, :] + acc_ref[rows, :], g2_ref[...], b2_ref[...])


def _out_ffn(x2d, yal2d, yah2d, yb2d, yc2d, wo, g1, b1, w1, w2, g2, b2, alpha):
    t = x2d.shape[0]
    tm = min(FFN_TILE, t)
    row = lambda n: pl.BlockSpec((tm, n), lambda i: (i, 0))
    once = lambda shape: pl.BlockSpec(shape, lambda i: (0, 0), pipeline_mode=pl.Buffered(1))
    return pl.pallas_call(
        functools.partial(_out_ffn_kernel, alpha=alpha),
        grid=(t // tm,),
        in_specs=[row(D_MODEL), row(128), row(128), row(GLA_WIDTH), row(SWA_WIDTH),
                  once((D_MODEL, D_MODEL)), once((1, D_MODEL)), once((1, D_MODEL)),
                  once((D_MODEL, D_FF)), once((D_FF, D_MODEL)),
                  once((1, D_MODEL)), once((1, D_MODEL))],
        out_specs=row(D_MODEL),
        out_shape=jax.ShapeDtypeStruct((t, D_MODEL), F32),
        scratch_shapes=[pltpu.VMEM((tm, D_MODEL), F32), pltpu.VMEM((tm, D_MODEL), BF16)],
        compiler_params=_params("parallel"),
        name="out_ffn",
    )(x2d, yal2d, yah2d, yb2d, yc2d, wo, g1, b1, w1, w2, g2, b2)


def _pack_w_in(w_in):
    points = np.cumsum([S5_WIDTH, GLA_QK, GLA_QK, GLA_WIDTH, GLA_WIDTH, GLA_RANK, GLA_RANK,
                        SWA_WIDTH, SWA_KV]).tolist()
    s5, gq, gk, gv, gr, glf, glb, aq, ak, av = jnp.split(w_in, points, axis=-1)
    pad = jnp.zeros(w_in.shape[:-1] + (128 - 2 * GLA_RANK,), w_in.dtype)
    return jnp.concatenate([s5, gq, gk, gv, gr, glf, glb, pad, aq, ak, av], axis=-1).astype(BF16)


def _pad_gate_w(w_a):
    z = jnp.zeros((128, 2 * GLA_QK), F32)
    z = z.at[0:GLA_RANK, 0:GLA_QK].set(w_a[0].astype(F32))
    z = z.at[GLA_RANK:2 * GLA_RANK, GLA_QK:2 * GLA_QK].set(w_a[1].astype(F32))
    return z.astype(BF16)


def _layer(x2d, bsz, seq, alpha, w_packed, a_re, a_im, log_step, b_re, b_im, c_re, c_im, d_skip,
           w_glu, b_glu, gla_w_a, gla_b_a, gla_ln_g, swa_sink, w_out, ln1_g, ln1_b, w_ff1, w_ff2,
           ln2_g, ln2_b, gla_consts, rope_tabs):
    t = bsz * seq
    row = lambda v: v.astype(F32)[None, :]
    ul, uh, g, a = _inproj(x2d, w_packed)

    mats = _s5_matrices(a_re, a_im, log_step, b_re, b_im, c_re, c_im, d_skip, seq // S5_CHUNK)
    yal, yah = _s5(ul.reshape(bsz, seq, 128), uh.reshape(bsz, seq, 128), mats, w_glu, row(b_glu))

    yb = _gla(g.reshape(bsz, seq, G_COLS), _pad_gate_w(gla_w_a), gla_b_a.astype(F32).reshape(1, 2 * GLA_QK),
              gla_consts, row(gla_ln_g))
    yc = _swa(a.reshape(bsz, seq, A_COLS), swa_sink.astype(F32), rope_tabs)

    return _out_ffn(x2d, yal.reshape(t, 128), yah.reshape(t, 128), yb.reshape(t, GLA_WIDTH), yc.reshape(t, SWA_WIDTH),
                    w_out, row(ln1_g), row(ln1_b), w_ff1, w_ff2, row(ln2_g), row(ln2_b), alpha)


def kernel(x, w_in, s5_a_re, s5_a_im, s5_log_step, s5_b_re, s5_b_im, s5_c_re, s5_c_im, s5_d, s5_w_glu, s5_b_glu, gla_w_a, gla_b_a, gla_ln_g, swa_sink, w_out, ln1_g, ln1_b, w_ff1, w_ff2, ln2_g, ln2_b):
    bsz, seq, _ = x.shape
    depth = w_in.shape[0]
    alpha = (2 * depth) ** 0.25
    gla_consts = _gla_constants()
    rope_tabs = _rope_tables(seq)
    w_packed = _pack_w_in(w_in)
    s5_w_glu, w_out, w_ff1, w_ff2 = (w.astype(BF16) for w in (s5_w_glu, w_out, w_ff1, w_ff2))
    h = x.reshape(bsz * seq, D_MODEL)
    for l in range(depth):
        h = _layer(h, bsz, seq, alpha, w_packed[l], s5_a_re[l], s5_a_im[l], s5_log_step[l],
                   s5_b_re[l], s5_b_im[l], s5_c_re[l], s5_c_im[l], s5_d[l], s5_w_glu[l], s5_b_glu[l],
                   gla_w_a[l], gla_b_a[l], gla_ln_g[l], swa_sink[l], w_out[l], ln1_g[l], ln1_b[l],
                   w_ff1[l], w_ff2[l], ln2_g[l], ln2_b[l], gla_consts, rope_tabs)
    return h.reshape(bsz, seq, D_MODEL)
```

```python
import functools
import math

import jax
import jax.numpy as jnp
import numpy as np
from jax import lax
from jax.experimental import pallas as pl
from jax.experimental.pallas import tpu as pltpu

F32 = jnp.float32
BF16 = jnp.bfloat16

D_MODEL = 1024
S5_WIDTH = 256
S5_GROUP = 16
S5_GROUPS = 16
S5_STATE = 64
S5_CHUNK = 16
GLA_HEADS = 4
GLA_DK = 32
GLA_DV = 64
GLA_QK = 128
GLA_WIDTH = 256
GLA_RANK = 16
GLA_TAU = 16.0
GLA_CHUNK = 64
GLA_GROUP = 16
SWA_WIDTH = 512
SWA_HEAD_DIM = 64
SWA_HEADS = 8
SWA_KV_HEADS = 2
SWA_GROUP = SWA_HEADS // SWA_KV_HEADS
SWA_KV = 128
SWA_WINDOW = 128
SWA_BLOCK = 128
ROT_DIM = 16
ROPE_THETA = 500000.0
D_FF = 4096
LN_EPS = 1e-5
NEG_BIG = -1e30
LOG2E = 1.4426950408889634

U_COLS = S5_WIDTH
G_COLS = GLA_QK + GLA_QK + GLA_WIDTH + GLA_WIDTH + 128
A_COLS = SWA_WIDTH + SWA_KV + SWA_KV
IN_COLS = U_COLS + G_COLS + A_COLS

VMEM_LIMIT = 56 * 1024 * 1024
ROW_TILE = 1024
FF_CHUNK = 512
FFN_TILE = 1024
FFN_SUB = 512
SWA_ROWS = 64

_NT = (((1,), (1,)), ((), ()))
_TN = (((0,), (0,)), ((), ()))


def _mm(a, b):
    return jnp.dot(a, b, preferred_element_type=F32)


def _params(*sem):
    return pltpu.CompilerParams(dimension_semantics=sem, vmem_limit_bytes=VMEM_LIMIT)


def _resident(shape):
    nd = len(shape)
    return pl.BlockSpec(shape, lambda *_: (0,) * nd)


def _inproj_kernel(x_ref, w_ref, ul_ref, uh_ref, g_ref, a_ref):
    x = x_ref[...].astype(BF16)
    u = _mm(x, w_ref[:, 0:U_COLS])
    ul_ref[...] = u[:, :128]
    uh_ref[...] = u[:, 128:]
    g_ref[...] = _mm(x, w_ref[:, U_COLS:U_COLS + G_COLS])
    a_ref[...] = _mm(x, w_ref[:, U_COLS + G_COLS:IN_COLS])


def _inproj(x2d, w_packed):
    t = x2d.shape[0]
    tm = min(ROW_TILE, t)
    row = lambda n: pl.BlockSpec((tm, n), lambda i: (i, 0))
    return pl.pallas_call(
        _inproj_kernel,
        grid=(t // tm,),
        in_specs=[row(D_MODEL), _resident((D_MODEL, IN_COLS))],
        out_specs=[row(128), row(128), row(G_COLS), row(A_COLS)],
        out_shape=[jax.ShapeDtypeStruct((t, 128), F32), jax.ShapeDtypeStruct((t, 128), F32),
                   jax.ShapeDtypeStruct((t, G_COLS), F32), jax.ShapeDtypeStruct((t, A_COLS), F32)],
        compiler_params=_params("parallel"),
        name="inproj",
    )(x2d, w_packed)


def _s5_matrices(a_re, a_im, log_step, b_re, b_im, c_re, c_im, d_skip, nch):
    c, ng, w = S5_CHUNK, S5_GROUPS, S5_CHUNK * S5_GROUP
    ar, ai = a_re.astype(F32), a_im.astype(F32)
    step = jnp.exp(log_step.astype(F32))
    lr, li = ar * step, ai * step

    def lam_pow(z, k):
        kk = jnp.asarray(k, F32)[None, :, None, None]
        mag = jnp.exp(kk * lr[z][:, None, None, :])
        ang = kk * li[z][:, None, None, :]
        return mag * jnp.cos(ang), mag * jnp.sin(ang)

    def cmul(xr, xi, yr, yi):
        return xr * yr - xi * yi, xr * yi + xi * yr

    l1r, l1i = jnp.exp(lr) * jnp.cos(li), jnp.exp(lr) * jnp.sin(li)
    nr, ni = l1r - 1.0, l1i
    den = ar * ar + ai * ai
    qr, qi = (nr * ar + ni * ai) / den, (ni * ar - nr * ai) / den
    bt_r = jnp.swapaxes(b_re.astype(F32), -1, -2)
    bt_i = jnp.swapaxes(b_im.astype(F32), -1, -2)
    bbr, bbi = cmul(qr[:, :, None, :], qi[:, :, None, :], bt_r, bt_i)
    cr, ci = c_re.astype(F32), c_im.astype(F32)

    idx = np.arange(c)
    flat = lambda t: t.reshape(ng, w, S5_STATE)

    def with_c(z, k):
        pr, pi = lam_pow(z, k)
        xr, xi = cmul(pr, pi, cr[z][:, None], ci[z][:, None])
        return flat(xr), flat(xi)

    def with_b(z, k):
        pr, pi = lam_pow(z, k)
        yr, yi = cmul(pr, pi, bbr[z][:, None], bbi[z][:, None])
        return flat(yr), flat(yi)

    def re_prod(x, y):
        return jnp.einsum('gmk,gnk->gmn', jnp.concatenate([x[0], -x[1]], axis=-1),
                          jnp.concatenate([y[0], y[1]], axis=-1), precision=lax.Precision.HIGHEST)

    ri = np.arange(w)[:, None] // S5_GROUP
    cj = np.arange(w)[None, :] // S5_GROUP
    tt_mat = (jnp.where(ri >= cj, re_prod(with_c(0, idx), with_b(0, -idx)), 0.0)
              + jnp.where(cj >= ri, re_prod(with_c(1, -idx), with_b(1, idx)), 0.0))

    sf, sb = with_b(0, c - 1 - idx), with_b(1, idx)
    s_mat = jnp.concatenate([sf[0], sb[0], sf[1], sb[1]], axis=-1)
    of, ob = with_c(0, idx + 1), with_c(1, c - idx)
    ot_mat = jnp.concatenate([of[0], ob[0], -of[1], -ob[1]], axis=-1)

    pcr = [lam_pow(z, [c]) for z in range(2)]
    a_r = jnp.concatenate([pcr[0][0][:, 0, 0, :], pcr[1][0][:, 0, 0, :]], axis=-1)
    a_i = jnp.concatenate([pcr[0][1][:, 0, 0, :], pcr[1][1][:, 0, 0, :]], axis=-1)
    d_col = jnp.broadcast_to(jnp.tile(d_skip.astype(F32), (1, c))[:, :, None], (ng, w, nch))
    return tt_mat.astype(BF16), s_mat.astype(BF16), ot_mat.astype(BF16), a_r, a_i, d_col


def _gelu_tanh(x):
    return 0.5 * x * (1.0 + jnp.tanh(math.sqrt(2.0 / math.pi) * (x + 0.044715 * (x * x * x))))


def _sigmoid(x):
    return 1.0 / (1.0 + jnp.exp(-x))


def _s5_kernel(ul_ref, uh_ref, tt_ref, s_ref, ot_ref, ar_ref, ai_ref, dcol_ref, wglu_ref, bglu_ref,
               yl_ref, yh_ref, d_scr, cr_scr, ci_scr, er_scr, ei_scr, zt_scr):
    c, ng = S5_CHUNK, S5_GROUPS
    nch = ul_ref.shape[1] // c
    for j in range(c):
        tok = pl.ds(j, nch, stride=c)
        uj = jnp.concatenate([ul_ref[0, tok, :], uh_ref[0, tok, :]], axis=1)
        d_scr[:, j * S5_GROUP:(j + 1) * S5_GROUP, :] = uj.T.reshape(ng, S5_GROUP, nch)
    contrib = [lax.dot_general(d_scr[g].astype(BF16), s_ref[g], _TN, preferred_element_type=F32)
               for g in range(ng)]
    for g in range(ng):
        grp = pl.ds(g, nch, stride=ng)
        cr_scr[grp, :] = contrib[g][:, :128]
        ci_scr[grp, :] = contrib[g][:, 128:]

    ar, ai = ar_ref[...], ai_ref[...]
    fwd = lax.broadcasted_iota(jnp.int32, (ng, 128), 1) < S5_STATE

    def step(k, carry):
        sre, sim = carry
        up = pl.ds(pl.multiple_of(k * ng, ng), ng)
        dn = pl.ds(pl.multiple_of((nch - 1 - k) * ng, ng), ng)
        er_scr[up, 0:S5_STATE] = sre[:, 0:S5_STATE]
        ei_scr[up, 0:S5_STATE] = sim[:, 0:S5_STATE]
        er_scr[dn, S5_STATE:] = sre[:, S5_STATE:]
        ei_scr[dn, S5_STATE:] = sim[:, S5_STATE:]
        cre = jnp.where(fwd, cr_scr[up, :], cr_scr[dn, :])
        cim = jnp.where(fwd, ci_scr[up, :], ci_scr[dn, :])
        return ar * sre - ai * sim + cre, ar * sim + ai * sre + cim

    zero = jnp.zeros((ng, 128), F32)
    lax.fori_loop(0, nch, step, (zero, zero), unroll=8)

    enter = []
    for g in range(ng):
        grp = pl.ds(g, nch, stride=ng)
        enter.append(jnp.concatenate([er_scr[grp, :], ei_scr[grp, :]], axis=1).astype(BF16))
    yt = [_mm(tt_ref[g], d_scr[g].astype(BF16)) + lax.dot_general(ot_ref[g], enter[g], _NT, preferred_element_type=F32)
          for g in range(ng)]
    for g in range(ng):
        zt_scr[:, g * S5_GROUP:(g + 1) * S5_GROUP, :] = _gelu_tanh(yt[g] + dcol_ref[g] * d_scr[g]).reshape(c, S5_GROUP, nch)
    glu = [lax.dot_general(zt_scr[i].astype(BF16), wglu_ref[...], _TN, preferred_element_type=F32)
           for i in range(c)]
    for i in range(c):
        gl = glu[i] + bglu_ref[...]
        ya = gl[:, :S5_WIDTH] * _sigmoid(gl[:, S5_WIDTH:])
        tok = pl.ds(i, nch, stride=c)
        yl_ref[0, tok, :] = ya[:, :128]
        yh_ref[0, tok, :] = ya[:, 128:]


def _s5(ul3, uh3, mats, wglu, bglu):
    b, seq, _ = ul3.shape
    nch = seq // S5_CHUNK
    tt_mat, s_mat, ot_mat, a_r, a_i, d_col = mats
    w = S5_CHUNK * S5_GROUP
    half = pl.BlockSpec((1, seq, 128), lambda i: (i, 0, 0))
    state_scr = pltpu.VMEM((S5_GROUPS * nch, 2 * S5_STATE), F32)
    return pl.pallas_call(
        _s5_kernel,
        grid=(b,),
        in_specs=[half, half,
                  _resident((S5_GROUPS, w, w)), _resident((S5_GROUPS, w, 4 * S5_STATE)),
                  _resident((S5_GROUPS, w, 4 * S5_STATE)),
                  _resident((S5_GROUPS, 128)), _resident((S5_GROUPS, 128)),
                  _resident((S5_GROUPS, w, nch)),
                  _resident((S5_WIDTH, 2 * S5_WIDTH)), _resident((1, 2 * S5_WIDTH))],
        out_specs=[half, half],
        out_shape=[jax.ShapeDtypeStruct((b, seq, 128), F32)] * 2,
        scratch_shapes=[pltpu.VMEM((S5_GROUPS, w, nch), F32)] + [state_scr] * 4
                       + [pltpu.VMEM((S5_CHUNK, w, nch), F32)],
        compiler_params=_params("parallel"),
        name="s5",
    )(ul3, uh3, tt_mat, s_mat, ot_mat, a_r, a_i, d_col, wglu, bglu)


def _gla_constants():
    r = np.arange(256)
    same = (r[:, None] // GLA_CHUNK) == (r[None, :] // GLA_CHUNK)
    prefix = same & (r[None, :] <= r[:, None])
    avg = ((r[:, None] // GLA_DV) == (r[None, :] // GLA_DV)) / float(GLA_DV)
    as_bf = lambda m: jnp.asarray(m.astype(np.float32), BF16)
    return as_bf(prefix), as_bf(avg)


def _log_sigmoid(x):
    return jnp.minimum(x, 0.0) - jnp.log(1.0 + jnp.exp(-jnp.abs(x)))


def _split_mm(a, b):
    hi = a.astype(BF16)
    lo = (a - hi.astype(F32)).astype(BF16)
    return _mm(hi, b) + _mm(lo, b)


def _chunk_sums(mats, la):
    hi = la.astype(BF16)
    lo = (la - hi.astype(F32)).astype(BF16)
    outs = []
    for mat in mats:
        blocks = []
        for i in range(la.shape[0] // 256):
            sl = slice(i * 256, (i + 1) * 256)
            blocks.append(_mm(mat, hi[sl]) + _mm(mat, lo[sl]))
        outs.append(jnp.concatenate(blocks, axis=0))
    return outs


def _chunk_row(x, row):
    x3 = x.reshape(x.shape[0] // GLA_CHUNK, GLA_CHUNK, x.shape[1])
    return jnp.broadcast_to(x3[:, row:row + 1, :], x3.shape).reshape(x.shape)


def _gla_kernel(g_ref, wa_ref, ba_ref, pre_ref, avg_ref, lng_ref, o_ref,
                q_s, kin_s, kst_s, v_s, totf_s, totb_s, o_s, kv_s, st_s):
    seq = g_ref.shape[1]
    nchunk = seq // GLA_CHUNK
    c = GLA_CHUNK
    grp = math.gcd(GLA_GROUP, nchunk)
    q = g_ref[0, :, 0:128] * (GLA_DK ** -0.5)
    k = g_ref[0, :, 128:256]
    lfb = g_ref[0, :, 768:896].astype(BF16)
    la = _log_sigmoid(_mm(lfb, wa_ref[...]) + ba_ref[...]) * (LOG2E / GLA_TAU)
    (pre,) = _chunk_sums((pre_ref[...],), la)
    tot = _chunk_row(pre, c - 1)
    bf, tf = pre[:, 0:128], tot[:, 0:128]
    q_s[:, 0:128] = (q * jnp.exp2(bf)).astype(BF16)
    kin_s[:, 0:128] = (k * jnp.exp2(-bf)).astype(BF16)
    kst_s[:, 0:128] = (k * jnp.exp2(tf - bf)).astype(BF16)
    totf_s[...] = tf
    tb = tot[:, 128:256]
    bs = tb - pre[:, 128:256] + la[:, 128:256]
    q_s[:, 128:256] = (q * jnp.exp2(bs)).astype(BF16)
    kin_s[:, 128:256] = (k * jnp.exp2(-bs)).astype(BF16)
    kst_s[:, 128:256] = (k * jnp.exp2(tb - bs)).astype(BF16)
    totb_s[...] = tb
    v_s[...] = g_ref[0, :, 256:512].astype(BF16)

    r256 = lax.broadcasted_iota(jnp.int32, (4 * c, 256), 0)
    l256 = lax.broadcasted_iota(jnp.int32, (4 * c, 256), 1)
    head_k = (r256 // c) == ((l256 % 128) // GLA_DK)
    head_v = (r256 // c) == (l256 // GLA_DV)
    qi = lax.broadcasted_iota(jnp.int32, (c, 4 * c), 0)
    kj = lax.broadcasted_iota(jnp.int32, (c, 4 * c), 1) % c
    zero_bf = jnp.zeros((), BF16)

    def local_group(it, carry):
        starts = [pl.multiple_of((it * grp + u) * c, c) for u in range(grp)]
        qq = [q_s[pl.ds(r0, c), :] for r0 in starts]
        vv = [v_s[pl.ds(r0, c), :] for r0 in starts]
        kexp = [jnp.where(head_k, jnp.concatenate([kin_s[pl.ds(r0, c), :]] * 4, axis=0), zero_bf) for r0 in starts]
        sc_f = [lax.dot_general(q[:, 0:128], k[:, 0:128], _NT, preferred_element_type=F32) for q, k in zip(qq, kexp)]
        sc_b = [lax.dot_general(q[:, 128:256], k[:, 128:256], _NT, preferred_element_type=F32) for q, k in zip(qq, kexp)]
        kv_t = [lax.dot_general(v, kst_s[pl.ds(r0, c), :], _TN, preferred_element_type=F32) for v, r0 in zip(vv, starts)]
        for u, r0 in enumerate(starts):
            vexp = jnp.where(head_v, jnp.concatenate([vv[u]] * 4, axis=0), zero_bf)
            pm = jnp.where(kj <= qi, sc_f[u], sc_b[u]).astype(BF16)
            o_s[pl.ds(r0, c), :] = _mm(pm, vexp)
            kv_s[it * grp + u] = jnp.where(head_k, kv_t[u], 0.0)
        return carry

    lax.fori_loop(0, nchunk // grp, local_group, 0)

    def carry_states(n, carry):
        st_f, st_b = carry
        m = nchunk - 1 - n
        st_s[n, :, 0:128] = st_f.astype(BF16)
        st_s[m, :, 128:256] = st_b.astype(BF16)
        st_f = jnp.exp2(totf_s[pl.ds(n * c, 1), :]) * st_f + kv_s[n, :, 0:128]
        st_b = jnp.exp2(totb_s[pl.ds(m * c, 1), :]) * st_b + kv_s[m, :, 128:256]
        return st_f, st_b

    zero = jnp.zeros((GLA_WIDTH, GLA_QK), F32)
    lax.fori_loop(0, nchunk, carry_states, (zero, zero))

    def cross_group(it, carry):
        starts = [pl.multiple_of((it * grp + u) * c, c) for u in range(grp)]
        upd = [lax.dot_general(q_s[pl.ds(r0, c), :], st_s[it * grp + u], _NT, preferred_element_type=F32)
               for u, r0 in enumerate(starts)]
        for r0, d in zip(starts, upd):
            o_s[pl.ds(r0, c), :] += d
        return carry

    lax.fori_loop(0, nchunk // grp, cross_group, 0)

    o = o_s[...]
    avg = avg_ref[...]
    mu = _split_mm(o, avg)
    dlt = o - mu
    var = _split_mm(dlt * dlt, avg)
    on = dlt * lax.rsqrt(var + LN_EPS)
    r = g_ref[0, :, 512:768]
    o_ref[0] = on * lng_ref[...] * (r * _sigmoid(r))


def _gla(g3, wa, ba, consts, ln_g):
    b, seq, _ = g3.shape
    pre, avg = consts
    nchunk = seq // GLA_CHUNK
    both = pltpu.VMEM((seq, 2 * GLA_QK), BF16)
    return pl.pallas_call(
        _gla_kernel,
        grid=(b,),
        in_specs=[pl.BlockSpec((1, seq, G_COLS), lambda i: (i, 0, 0)),
                  _resident((128, 2 * GLA_QK)), _resident((1, 2 * GLA_QK)),
                  _resident((256, 256)), _resident((256, 256)), _resident((1, 256))],
        out_specs=pl.BlockSpec((1, seq, GLA_WIDTH), lambda i: (i, 0, 0)),
        out_shape=jax.ShapeDtypeStruct((b, seq, GLA_WIDTH), F32),
        scratch_shapes=[both, both, both, pltpu.VMEM((seq, GLA_WIDTH), BF16),
                        pltpu.VMEM((seq, GLA_QK), F32), pltpu.VMEM((seq, GLA_QK), F32),
                        pltpu.VMEM((seq, GLA_WIDTH), F32),
                        pltpu.VMEM((nchunk, GLA_WIDTH, 2 * GLA_QK), F32),
                        pltpu.VMEM((nchunk, GLA_WIDTH, 2 * GLA_QK), BF16)],
        compiler_params=_params("parallel"),
        name="gla",
    )(g3, wa, ba, pre, avg, ln_g)


def _rope_tables(seq):
    pos = jnp.arange(seq, dtype=F32)
    inv_freq = ROPE_THETA ** (-jnp.arange(0, ROT_DIM, 2, dtype=F32) / ROT_DIM)
    ang = pos[:, None] * inv_freq[None, :]
    cos, sin = jnp.cos(ang), jnp.sin(ang)
    half = ROT_DIM // 2
    pad = jnp.zeros((seq, SWA_HEAD_DIM - ROT_DIM), F32)
    c_head = jnp.concatenate([cos, cos, pad + 1.0], axis=1)
    s_head = jnp.concatenate([sin, sin, pad], axis=1)
    two = lambda t: jnp.concatenate([t, t], axis=1)
    rot = np.zeros((128, 128), np.float32)
    for base in (0, SWA_HEAD_DIM):
        for d in range(half):
            rot[base + d + half, base + d] = -1.0
            rot[base + d, base + d + half] = 1.0
    return two(c_head), two(s_head), jnp.asarray(rot, BF16)


def _swa_kernel(sink_ref, a_ref, cos_ref, sin_ref, rot_ref, o_ref,
                q_s, klo_s, khi_s, vd_s, sa_s, sb_s, pa_s, pb_s, bias_s, ma_s, mb_s):
    seq = a_ref.shape[1]
    blk = SWA_BLOCK
    hd = SWA_HEAD_DIM
    pairs = SWA_GROUP // 2
    cosv, sinv, rot = cos_ref[...], sin_ref[...], rot_ref[...]

    def rope(x):
        return x * cosv + _split_mm(x, rot) * sinv

    for slab in range(SWA_WIDTH // 128):
        q_s[slab] = (rope(a_ref[0, :, slab * 128:(slab + 1) * 128]) * (hd ** -0.5 * LOG2E)).astype(BF16)
    kr = rope(a_ref[0, :, SWA_WIDTH:SWA_WIDTH + SWA_KV])
    vr = a_ref[0, :, SWA_WIDTH + SWA_KV:A_COLS]
    ksw = pltpu.roll(kr, hd, 1)
    vsw = pltpu.roll(vr, hd, 1)
    low = lax.broadcasted_iota(jnp.int32, (seq, 128), 1) < hd
    body = slice(blk, blk + seq)
    zpad = jnp.zeros((blk, 128), BF16)
    for kh in range(SWA_KV_HEADS):
        for ref in (klo_s, khi_s):
            ref[kh, 0:blk, :] = zpad
            ref[kh, blk + seq:2 * blk + seq, :] = zpad
        vd_s[kh, 0:blk, 0:128] = zpad
        vd_s[kh, blk + seq:2 * blk + seq, 0:128] = zpad
        vd_s[kh, :, 128:256] = jnp.ones((seq + 2 * blk, 128), BF16)
    klo_s[0, body, :] = jnp.where(low, kr, 0.0).astype(BF16)
    khi_s[0, body, :] = jnp.where(low, 0.0, ksw).astype(BF16)
    klo_s[1, body, :] = jnp.where(low, ksw, 0.0).astype(BF16)
    khi_s[1, body, :] = jnp.where(low, 0.0, kr).astype(BF16)
    vd_s[0, body, 0:128] = jnp.where(low, vr, vsw).astype(BF16)
    vd_s[1, body, 0:128] = jnp.where(low, vsw, vr).astype(BF16)

    qi = lax.broadcasted_iota(jnp.int32, (blk, 3 * blk), 0)
    kk = lax.broadcasted_iota(jnp.int32, (blk, 3 * blk), 1)
    band = jnp.abs(qi + blk - kk) <= SWA_WINDOW
    low_o = lax.broadcasted_iota(jnp.int32, (blk, 128), 1) < hd
    rows = SWA_GROUP * blk
    chunks = [(kh, rb) for kh in range(SWA_KV_HEADS) for rb in range(rows // SWA_ROWS)]

    def scores(n, s_s):
        r0 = pl.multiple_of(n * blk, blk)
        for kh in range(SWA_KV_HEADS):
            kcat = jnp.concatenate([klo_s[kh, pl.ds(r0, 3 * blk), :], khi_s[kh, pl.ds(r0, 3 * blk), :]], axis=0)
            for pr in range(pairs):
                qq = q_s[kh * pairs + pr, pl.ds(r0, blk), :]
                s2 = lax.dot_general(qq, kcat, _NT, preferred_element_type=F32)
                s_s[kh, (2 * pr) * blk:(2 * pr + 1) * blk, :] = s2[:, :3 * blk]
                s_s[kh, (2 * pr + 1) * blk:(2 * pr + 2) * blk, :] = s2[:, 3 * blk:]

    nblk = seq // blk

    def softmax_block(n, s_s, p_s, m_s):
        kpos = kk + (n - 1) * blk
        bias_s[...] = jnp.where(band & (kpos >= 0) & (kpos < seq), 0.0, NEG_BIG)
        for kh, rb in chunks:
            rs = slice(rb * SWA_ROWS, (rb + 1) * SWA_ROWS)
            bs = slice((rb * SWA_ROWS) % blk, (rb * SWA_ROWS) % blk + SWA_ROWS)
            sink2 = sink_ref[kh * SWA_GROUP + (rb * SWA_ROWS) // blk] * LOG2E
            lo = s_s[kh, rs, 0:blk] + bias_s[bs, 0:blk]
            hi = s_s[kh, rs, 2 * blk:3 * blk] + bias_s[bs, 2 * blk:3 * blk]
            s_s[kh, rs, 0:blk] = lo
            s_s[kh, rs, 2 * blk:3 * blk] = hi
            top = jnp.maximum(jnp.maximum(lo, s_s[kh, rs, blk:2 * blk]), hi)
            m_s[kh, rs, :] = jnp.broadcast_to(jnp.maximum(jnp.max(top, axis=1, keepdims=True), sink2), (SWA_ROWS, 128))
        for kh, rb in chunks:
            rs = slice(rb * SWA_ROWS, (rb + 1) * SWA_ROWS)
            p = jnp.exp2(s_s[kh, rs, :] - jnp.concatenate([m_s[kh, rs, :]] * 3, axis=1))
            p_s[kh, rs, :] = p.astype(BF16)

    def output_block(n, p_s, m_s):
        r0 = pl.multiple_of(n * blk, blk)
        for kh in range(SWA_KV_HEADS):
            o3 = _mm(p_s[kh], vd_s[kh, pl.ds(r0, 3 * blk), :])
            sink2 = jnp.concatenate([jnp.full((blk, 128), sink_ref[kh * SWA_GROUP + g] * LOG2E, F32)
                                     for g in range(SWA_GROUP)], axis=0)
            o2 = o3[:, 0:128] / (o3[:, 128:256] + jnp.exp2(sink2 - m_s[kh]))
            o_ref[0, pl.ds(r0, blk), kh * SWA_GROUP * hd:(kh + 1) * SWA_GROUP * hd] = jnp.concatenate(
                [jnp.where(low_o, o2[(2 * pr) * blk:(2 * pr + 1) * blk], o2[(2 * pr + 1) * blk:(2 * pr + 2) * blk])
                 for pr in range(pairs)], axis=1)

    def body_fn(t, carry):
        n = 2 * t + 1
        output_block(n - 1, pa_s, ma_s)
        scores(n + 1, sa_s)
        softmax_block(n, sb_s, pb_s, mb_s)
        output_block(n, pb_s, mb_s)
        scores(jnp.minimum(n + 2, nblk - 1), sb_s)
        softmax_block(n + 1, sa_s, pa_s, ma_s)
        return carry

    scores(0, sa_s)
    scores(1, sb_s)
    softmax_block(0, sa_s, pa_s, ma_s)
    lax.fori_loop(0, (nblk - 2) // 2, body_fn, 0)
    softmax_block(nblk - 1, sb_s, pb_s, mb_s)
    output_block(nblk - 2, pa_s, ma_s)
    output_block(nblk - 1, pb_s, mb_s)


def _swa(a3, sink, tables):
    b, seq, _ = a3.shape
    cos_t, sin_t, rot = tables
    tab = pl.BlockSpec((seq, 128), lambda i, s: (0, 0))
    rot_spec = pl.BlockSpec((128, 128), lambda i, s: (0, 0))
    banded = pltpu.VMEM((SWA_KV_HEADS, seq + 2 * SWA_BLOCK, 128), BF16)
    stat = pltpu.VMEM((SWA_KV_HEADS, SWA_GROUP * SWA_BLOCK, 128), F32)
    grid_spec = pltpu.PrefetchScalarGridSpec(
        num_scalar_prefetch=1,
        grid=(b,),
        in_specs=[pl.BlockSpec((1, seq, A_COLS), lambda i, s: (i, 0, 0)), tab, tab, rot_spec],
        out_specs=pl.BlockSpec((1, seq, SWA_WIDTH), lambda i, s: (i, 0, 0)),
        scratch_shapes=[pltpu.VMEM((SWA_WIDTH // 128, seq, 128), BF16), banded, banded,
                        pltpu.VMEM((SWA_KV_HEADS, seq + 2 * SWA_BLOCK, 256), BF16),
                        pltpu.VMEM((SWA_KV_HEADS, SWA_GROUP * SWA_BLOCK, 3 * SWA_BLOCK), F32),
                        pltpu.VMEM((SWA_KV_HEADS, SWA_GROUP * SWA_BLOCK, 3 * SWA_BLOCK), F32),
                        pltpu.VMEM((SWA_KV_HEADS, SWA_GROUP * SWA_BLOCK, 3 * SWA_BLOCK), BF16),
                        pltpu.VMEM((SWA_KV_HEADS, SWA_GROUP * SWA_BLOCK, 3 * SWA_BLOCK), BF16),
                        pltpu.VMEM((SWA_BLOCK, 3 * SWA_BLOCK), F32), stat, stat],
    )
    return pl.pallas_call(
        _swa_kernel,
        grid_spec=grid_spec,
        out_shape=jax.ShapeDtypeStruct((b, seq, SWA_WIDTH), F32),
        compiler_params=_params("parallel"),
        name="swa",
    )(sink, a3, cos_t, sin_t, rot)


def _layer_norm(x, g, b):
    mu = jnp.mean(x, axis=-1, keepdims=True)
    d = x - mu
    var = jnp.mean(d * d, axis=-1, keepdims=True)
    return d * lax.rsqrt(var + LN_EPS) * g + b


def _out_ffn_kernel(x_ref, yal_ref, yah_ref, yb_ref, yc_ref, wo_ref,
                    g1_ref, b1_ref, w1_ref, w2_ref, g2_ref, b2_ref, o_ref, acc_ref, x1b_ref, *, alpha):
    subs = [slice(r, r + FFN_SUB) for r in range(0, x_ref.shape[0], FFN_SUB)]
    for rows in subs:
        ya = jnp.concatenate([yal_ref[rows, :], yah_ref[rows, :]], axis=1)
        mix = (_mm(ya.astype(BF16), wo_ref[0:S5_WIDTH, :])
               + _mm(yb_ref[rows, :].astype(BF16), wo_ref[S5_WIDTH:S5_WIDTH + GLA_WIDTH, :])
               + _mm(yc_ref[rows, :].astype(BF16), wo_ref[S5_WIDTH + GLA_WIDTH:D_MODEL, :]))
        x1 = _layer_norm(alpha * x_ref[rows, :] + mix, g1_ref[...], b1_ref[...])
        o_ref[rows, :] = x1
        x1b_ref[rows, :] = x1.astype(BF16)
    for rows in subs:
        for ci in range(D_FF // FF_CHUNK):
            sl = slice(ci * FF_CHUNK, (ci + 1) * FF_CHUNK)
            hid = jnp.maximum(_mm(x1b_ref[rows, :], w1_ref[:, sl]), 0.0)
            part = _mm((hid * hid).astype(BF16), w2_ref[sl, :])
            if ci == 0:
                acc_ref[rows, :] = part
            else:
                acc_ref[rows, :] += part
    for rows in subs:
        o_ref[rows, :] = _layer_norm(alpha * o_ref[rows, :] + acc_ref[rows, :], g2_ref[...], b2_ref[...])


def _out_ffn(x2d, yal2d, yah2d, yb2d, yc2d, wo, g1, b1, w1, w2, g2, b2, alpha):
    t = x2d.shape[0]
    tm = min(FFN_TILE, t)
    row = lambda n: pl.BlockSpec((tm, n), lambda i: (i, 0))
    once = lambda shape: pl.BlockSpec(shape, lambda i: (0, 0), pipeline_mode=pl.Buffered(1))
    return pl.pallas_call(
        functools.partial(_out_ffn_kernel, alpha=alpha),
        grid=(t // tm,),
        in_specs=[row(D_MODEL), row(128), row(128), row(GLA_WIDTH), row(SWA_WIDTH),
                  once((D_MODEL, D_MODEL)), once((1, D_MODEL)), once((1, D_MODEL)),
                  once((D_MODEL, D_FF)), once((D_FF, D_MODEL)),
                  once((1, D_MODEL)), once((1, D_MODEL))],
        out_specs=row(D_MODEL),
        out_shape=jax.ShapeDtypeStruct((t, D_MODEL), F32),
        scratch_shapes=[pltpu.VMEM((tm, D_MODEL), F32), pltpu.VMEM((tm, D_MODEL), BF16)],
        compiler_params=_params("parallel"),
        name="out_ffn",
    )(x2d, yal2d, yah2d, yb2d, yc2d, wo, g1, b1, w1, w2, g2, b2)


def _pack_w_in(w_in):
    points = np.cumsum([S5_WIDTH, GLA_QK, GLA_QK, GLA_WIDTH, GLA_WIDTH, GLA_RANK, GLA_RANK,
                        SWA_WIDTH, SWA_KV]).tolist()
    s5, gq, gk, gv, gr, glf, glb, aq, ak, av = jnp.split(w_in, points, axis=-1)
    pad = jnp.zeros(w_in.shape[:-1] + (128 - 2 * GLA_RANK,), w_in.dtype)
    return jnp.concatenate([s5, gq, gk, gv, gr, glf, glb, pad, aq, ak, av], axis=-1).astype(BF16)


def _pad_gate_w(w_a):
    z = jnp.zeros((128, 2 * GLA_QK), F32)
    z = z.at[0:GLA_RANK, 0:GLA_QK].set(w_a[0].astype(F32))
    z = z.at[GLA_RANK:2 * GLA_RANK, GLA_QK:2 * GLA_QK].set(w_a[1].astype(F32))
    return z.astype(BF16)


def _layer(x2d, bsz, seq, alpha, w_packed, a_re, a_im, log_step, b_re, b_im, c_re, c_im, d_skip,
           w_glu, b_glu, gla_w_a, gla_b_a, gla_ln_g, swa_sink, w_out, ln1_g, ln1_b, w_ff1, w_ff2,
           ln2_g, ln2_b, gla_consts, rope_tabs):
    t = bsz * seq
    row = lambda v: v.astype(F32)[None, :]
    ul, uh, g, a = _inproj(x2d, w_packed)

    mats = _s5_matrices(a_re, a_im, log_step, b_re, b_im, c_re, c_im, d_skip, seq // S5_CHUNK)
    yal, yah = _s5(ul.reshape(bsz, seq, 128), uh.reshape(bsz, seq, 128), mats, w_glu, row(b_glu))

    yb = _gla(g.reshape(bsz, seq, G_COLS), _pad_gate_w(gla_w_a), gla_b_a.astype(F32).reshape(1, 2 * GLA_QK),
              gla_consts, row(gla_ln_g))
    yc = _swa(a.reshape(bsz, seq, A_COLS), swa_sink.astype(F32), rope_tabs)

    return _out_ffn(x2d, yal.reshape(t, 128), yah.reshape(t, 128), yb.reshape(t, GLA_WIDTH), yc.reshape(t, SWA_WIDTH),
                    w_out, row(ln1_g), row(ln1_b), w_ff1, w_ff2, row(ln2_g), row(ln2_b), alpha)


def kernel(x, w_in, s5_a_re, s5_a_im, s5_log_step, s5_b_re, s5_b_im, s5_c_re, s5_c_im, s5_d, s5_w_glu, s5_b_glu, gla_w_a, gla_b_a, gla_ln_g, swa_sink, w_out, ln1_g, ln1_b, w_ff1, w_ff2, ln2_g, ln2_b):
    bsz, seq, _ = x.shape
    depth = w_in.shape[0]
    alpha = (2 * depth) ** 0.25
    gla_consts = _gla_constants()
    rope_tabs = _rope_tables(seq)
    w_packed = _pack_w_in(w_in)
    s5_w_glu, w_out, w_ff1, w_ff2 = (w.astype(BF16) for w in (s5_w_glu, w_out, w_ff1, w_ff2))
    h = x.reshape(bsz * seq, D_MODEL)
    for l in range(depth):
        h = _layer(h, bsz, seq, alpha, w_packed[l], s5_a_re[l], s5_a_im[l], s5_log_step[l],
                   s5_b_re[l], s5_b_im[l], s5_c_re[l], s5_c_im[l], s5_d[l], s5_w_glu[l], s5_b_glu[l],
                   gla_w_a[l], gla_b_a[l], gla_ln_g[l], swa_sink[l], w_out[l], ln1_g[l], ln1_b[l],
                   w_ff1[l], w_ff2[l], ln2_g[l], ln2_b[l], gla_consts, rope_tabs)
    return h.reshape(bsz, seq, D_MODEL)
```

```python
import functools
import math

import jax
import jax.numpy as jnp
import numpy as np
from jax import lax
from jax.experimental import pallas as pl
from jax.experimental.pallas import tpu as pltpu

F32 = jnp.float32
BF16 = jnp.bfloat16

D_MODEL = 1024
S5_WIDTH = 256
S5_GROUP = 16
S5_GROUPS = 16
S5_STATE = 64
S5_CHUNK = 16
GLA_HEADS = 4
GLA_DK = 32
GLA_DV = 64
GLA_QK = 128
GLA_WIDTH = 256
GLA_RANK = 16
GLA_TAU = 16.0
GLA_CHUNK = 64
GLA_GROUP = 32
SWA_WIDTH = 512
SWA_HEAD_DIM = 64
SWA_HEADS = 8
SWA_KV_HEADS = 2
SWA_GROUP = SWA_HEADS // SWA_KV_HEADS
SWA_KV = 128
SWA_WINDOW = 128
SWA_BLOCK = 128
ROT_DIM = 16
ROPE_THETA = 500000.0
D_FF = 4096
LN_EPS = 1e-5
NEG_BIG = -1e30
LOG2E = 1.4426950408889634

U_COLS = S5_WIDTH
G_COLS = GLA_QK + GLA_QK + GLA_WIDTH + GLA_WIDTH + 128
A_COLS = SWA_WIDTH + SWA_KV + SWA_KV
IN_COLS = U_COLS + G_COLS + A_COLS

VMEM_LIMIT = 56 * 1024 * 1024
ROW_TILE = 1024
FF_CHUNK = 512
FFN_TILE = 1024
FFN_SUB = 512
SWA_ROWS = 64

_NT = (((1,), (1,)), ((), ()))
_TN = (((0,), (0,)), ((), ()))


def _mm(a, b):
    return jnp.dot(a, b, preferred_element_type=F32)


def _params(*sem):
    return pltpu.CompilerParams(dimension_semantics=sem, vmem_limit_bytes=VMEM_LIMIT)


def _resident(shape):
    nd = len(shape)
    return pl.BlockSpec(shape, lambda *_: (0,) * nd)


def _inproj_kernel(x_ref, w_ref, ul_ref, uh_ref, g_ref, a_ref):
    x = x_ref[...].astype(BF16)
    u = _mm(x, w_ref[:, 0:U_COLS])
    ul_ref[...] = u[:, :128]
    uh_ref[...] = u[:, 128:]
    g_ref[...] = _mm(x, w_ref[:, U_COLS:U_COLS + G_COLS])
    a_ref[...] = _mm(x, w_ref[:, U_COLS + G_COLS:IN_COLS])


def _inproj(x2d, w_packed):
    t = x2d.shape[0]
    tm = min(ROW_TILE, t)
    row = lambda n: pl.BlockSpec((tm, n), lambda i: (i, 0))
    return pl.pallas_call(
        _inproj_kernel,
        grid=(t // tm,),
        in_specs=[row(D_MODEL), _resident((D_MODEL, IN_COLS))],
        out_specs=[row(128), row(128), row(G_COLS), row(A_COLS)],
        out_shape=[jax.ShapeDtypeStruct((t, 128), F32), jax.ShapeDtypeStruct((t, 128), F32),
                   jax.ShapeDtypeStruct((t, G_COLS), F32), jax.ShapeDtypeStruct((t, A_COLS), F32)],
        compiler_params=_params("parallel"),
        name="inproj",
    )(x2d, w_packed)


def _s5_matrices(a_re, a_im, log_step, b_re, b_im, c_re, c_im, d_skip, nch):
    c, ng, w = S5_CHUNK, S5_GROUPS, S5_CHUNK * S5_GROUP
    ar, ai = a_re.astype(F32), a_im.astype(F32)
    step = jnp.exp(log_step.astype(F32))
    lr, li = ar * step, ai * step

    def lam_pow(z, k):
        kk = jnp.asarray(k, F32)[None, :, None, None]
        mag = jnp.exp(kk * lr[z][:, None, None, :])
        ang = kk * li[z][:, None, None, :]
        return mag * jnp.cos(ang), mag * jnp.sin(ang)

    def cmul(xr, xi, yr, yi):
        return xr * yr - xi * yi, xr * yi + xi * yr

    l1r, l1i = jnp.exp(lr) * jnp.cos(li), jnp.exp(lr) * jnp.sin(li)
    nr, ni = l1r - 1.0, l1i
    den = ar * ar + ai * ai
    qr, qi = (nr * ar + ni * ai) / den, (ni * ar - nr * ai) / den
    bt_r = jnp.swapaxes(b_re.astype(F32), -1, -2)
    bt_i = jnp.swapaxes(b_im.astype(F32), -1, -2)
    bbr, bbi = cmul(qr[:, :, None, :], qi[:, :, None, :], bt_r, bt_i)
    cr, ci = c_re.astype(F32), c_im.astype(F32)

    idx = np.arange(c)
    flat = lambda t: t.reshape(ng, w, S5_STATE)

    def with_c(z, k):
        pr, pi = lam_pow(z, k)
        xr, xi = cmul(pr, pi, cr[z][:, None], ci[z][:, None])
        return flat(xr), flat(xi)

    def with_b(z, k):
        pr, pi = lam_pow(z, k)
        yr, yi = cmul(pr, pi, bbr[z][:, None], bbi[z][:, None])
        return flat(yr), flat(yi)

    def re_prod(x, y):
        return jnp.einsum('gmk,gnk->gmn', jnp.concatenate([x[0], -x[1]], axis=-1),
                          jnp.concatenate([y[0], y[1]], axis=-1), precision=lax.Precision.HIGHEST)

    ri = np.arange(w)[:, None] // S5_GROUP
    cj = np.arange(w)[None, :] // S5_GROUP
    tt_mat = (jnp.where(ri >= cj, re_prod(with_c(0, idx), with_b(0, -idx)), 0.0)
              + jnp.where(cj >= ri, re_prod(with_c(1, -idx), with_b(1, idx)), 0.0))

    sf, sb = with_b(0, c - 1 - idx), with_b(1, idx)
    s_mat = jnp.concatenate([sf[0], sb[0], sf[1], sb[1]], axis=-1)
    of, ob = with_c(0, idx + 1), with_c(1, c - idx)
    ot_mat = jnp.concatenate([of[0], ob[0], -of[1], -ob[1]], axis=-1)

    pcr = [lam_pow(z, [c]) for z in range(2)]
    a_r = jnp.concatenate([pcr[0][0][:, 0, 0, :], pcr[1][0][:, 0, 0, :]], axis=-1)
    a_i = jnp.concatenate([pcr[0][1][:, 0, 0, :], pcr[1][1][:, 0, 0, :]], axis=-1)
    d_col = jnp.broadcast_to(jnp.tile(d_skip.astype(F32), (1, c))[:, :, None], (ng, w, nch))
    return tt_mat.astype(BF16), s_mat.astype(BF16), ot_mat.astype(BF16), a_r, a_i, d_col


def _gelu_tanh(x):
    return 0.5 * x * (1.0 + jnp.tanh(math.sqrt(2.0 / math.pi) * (x + 0.044715 * (x * x * x))))


def _sigmoid(x):
    return 1.0 / (1.0 + jnp.exp(-x))


def _s5_kernel(ul_ref, uh_ref, tt_ref, s_ref, ot_ref, ar_ref, ai_ref, dcol_ref, wglu_ref, bglu_ref,
               yl_ref, yh_ref, d_scr, cr_scr, ci_scr, er_scr, ei_scr, zt_scr):
    c, ng = S5_CHUNK, S5_GROUPS
    nch = ul_ref.shape[1] // c
    for j in range(c):
        tok = pl.ds(j, nch, stride=c)
        uj = jnp.concatenate([ul_ref[0, tok, :], uh_ref[0, tok, :]], axis=1)
        d_scr[:, j * S5_GROUP:(j + 1) * S5_GROUP, :] = uj.T.reshape(ng, S5_GROUP, nch)
    contrib = [lax.dot_general(d_scr[g].astype(BF16), s_ref[g], _TN, preferred_element_type=F32)
               for g in range(ng)]
    for g in range(ng):
        grp = pl.ds(g, nch, stride=ng)
        cr_scr[grp, :] = contrib[g][:, :128]
        ci_scr[grp, :] = contrib[g][:, 128:]

    ar, ai = ar_ref[...], ai_ref[...]
    fwd = lax.broadcasted_iota(jnp.int32, (ng, 128), 1) < S5_STATE

    def step(k, carry):
        sre, sim = carry
        up = pl.ds(pl.multiple_of(k * ng, ng), ng)
        dn = pl.ds(pl.multiple_of((nch - 1 - k) * ng, ng), ng)
        er_scr[up, 0:S5_STATE] = sre[:, 0:S5_STATE]
        ei_scr[up, 0:S5_STATE] = sim[:, 0:S5_STATE]
        er_scr[dn, S5_STATE:] = sre[:, S5_STATE:]
        ei_scr[dn, S5_STATE:] = sim[:, S5_STATE:]
        cre = jnp.where(fwd, cr_scr[up, :], cr_scr[dn, :])
        cim = jnp.where(fwd, ci_scr[up, :], ci_scr[dn, :])
        return ar * sre - ai * sim + cre, ar * sim + ai * sre + cim

    zero = jnp.zeros((ng, 128), F32)
    lax.fori_loop(0, nch, step, (zero, zero), unroll=8)

    enter = []
    for g in range(ng):
        grp = pl.ds(g, nch, stride=ng)
        enter.append(jnp.concatenate([er_scr[grp, :], ei_scr[grp, :]], axis=1).astype(BF16))
    yt = [_mm(tt_ref[g], d_scr[g].astype(BF16)) + lax.dot_general(ot_ref[g], enter[g], _NT, preferred_element_type=F32)
          for g in range(ng)]
    for g in range(ng):
        zt_scr[:, g * S5_GROUP:(g + 1) * S5_GROUP, :] = _gelu_tanh(yt[g] + dcol_ref[g] * d_scr[g]).reshape(c, S5_GROUP, nch)
    glu = [lax.dot_general(zt_scr[i].astype(BF16), wglu_ref[...], _TN, preferred_element_type=F32)
           for i in range(c)]
    for i in range(c):
        gl = glu[i] + bglu_ref[...]
        ya = gl[:, :S5_WIDTH] * _sigmoid(gl[:, S5_WIDTH:])
        tok = pl.ds(i, nch, stride=c)
        yl_ref[0, tok, :] = ya[:, :128]
        yh_ref[0, tok, :] = ya[:, 128:]


def _s5(ul3, uh3, mats, wglu, bglu):
    b, seq, _ = ul3.shape
    nch = seq // S5_CHUNK
    tt_mat, s_mat, ot_mat, a_r, a_i, d_col = mats
    w = S5_CHUNK * S5_GROUP
    half = pl.BlockSpec((1, seq, 128), lambda i: (i, 0, 0))
    state_scr = pltpu.VMEM((S5_GROUPS * nch, 2 * S5_STATE), F32)
    return pl.pallas_call(
        _s5_kernel,
        grid=(b,),
        in_specs=[half, half,
                  _resident((S5_GROUPS, w, w)), _resident((S5_GROUPS, w, 4 * S5_STATE)),
                  _resident((S5_GROUPS, w, 4 * S5_STATE)),
                  _resident((S5_GROUPS, 128)), _resident((S5_GROUPS, 128)),
                  _resident((S5_GROUPS, w, nch)),
                  _resident((S5_WIDTH, 2 * S5_WIDTH)), _resident((1, 2 * S5_WIDTH))],
        out_specs=[half, half],
        out_shape=[jax.ShapeDtypeStruct((b, seq, 128), F32)] * 2,
        scratch_shapes=[pltpu.VMEM((S5_GROUPS, w, nch), F32)] + [state_scr] * 4
                       + [pltpu.VMEM((S5_CHUNK, w, nch), F32)],
        compiler_params=_params("parallel"),
        name="s5",
    )(ul3, uh3, tt_mat, s_mat, ot_mat, a_r, a_i, d_col, wglu, bglu)


def _gla_constants():
    r = np.arange(256)
    same = (r[:, None] // GLA_CHUNK) == (r[None, :] // GLA_CHUNK)
    prefix = same & (r[None, :] <= r[:, None])
    avg = ((r[:, None] // GLA_DV) == (r[None, :] // GLA_DV)) / float(GLA_DV)
    as_bf = lambda m: jnp.asarray(m.astype(np.float32), BF16)
    return as_bf(prefix), as_bf(avg)


def _log_sigmoid(x):
    return jnp.minimum(x, 0.0) - jnp.log(1.0 + jnp.exp(-jnp.abs(x)))


def _split_mm(a, b):
    hi = a.astype(BF16)
    lo = (a - hi.astype(F32)).astype(BF16)
    return _mm(hi, b) + _mm(lo, b)


def _chunk_sums(mats, la):
    hi = la.astype(BF16)
    lo = (la - hi.astype(F32)).astype(BF16)
    outs = []
    for mat in mats:
        blocks = []
        for i in range(la.shape[0] // 256):
            sl = slice(i * 256, (i + 1) * 256)
            blocks.append(_mm(mat, hi[sl]) + _mm(mat, lo[sl]))
        outs.append(jnp.concatenate(blocks, axis=0))
    return outs


def _chunk_row(x, row):
    x3 = x.reshape(x.shape[0] // GLA_CHUNK, GLA_CHUNK, x.shape[1])
    return jnp.broadcast_to(x3[:, row:row + 1, :], x3.shape).reshape(x.shape)


def _gla_kernel(g_ref, wa_ref, ba_ref, pre_ref, avg_ref, lng_ref, o_ref,
                q_s, kin_s, kst_s, v_s, totf_s, totb_s, o_s, kv_s, st_s):
    seq = g_ref.shape[1]
    nchunk = seq // GLA_CHUNK
    c = GLA_CHUNK
    grp = math.gcd(GLA_GROUP, nchunk)
    q = g_ref[0, :, 0:128] * (GLA_DK ** -0.5)
    k = g_ref[0, :, 128:256]
    lfb = g_ref[0, :, 768:896].astype(BF16)
    la = _log_sigmoid(_mm(lfb, wa_ref[...]) + ba_ref[...]) * (LOG2E / GLA_TAU)
    (pre,) = _chunk_sums((pre_ref[...],), la)
    tot = _chunk_row(pre, c - 1)
    bf, tf = pre[:, 0:128], tot[:, 0:128]
    q_s[:, 0:128] = (q * jnp.exp2(bf)).astype(BF16)
    kin_s[:, 0:128] = (k * jnp.exp2(-bf)).astype(BF16)
    kst_s[:, 0:128] = (k * jnp.exp2(tf - bf)).astype(BF16)
    totf_s[...] = tf
    tb = tot[:, 128:256]
    bs = tb - pre[:, 128:256] + la[:, 128:256]
    q_s[:, 128:256] = (q * jnp.exp2(bs)).astype(BF16)
    kin_s[:, 128:256] = (k * jnp.exp2(-bs)).astype(BF16)
    kst_s[:, 128:256] = (k * jnp.exp2(tb - bs)).astype(BF16)
    totb_s[...] = tb
    v_s[...] = g_ref[0, :, 256:512].astype(BF16)

    r256 = lax.broadcasted_iota(jnp.int32, (4 * c, 256), 0)
    l256 = lax.broadcasted_iota(jnp.int32, (4 * c, 256), 1)
    head_k = (r256 // c) == ((l256 % 128) // GLA_DK)
    head_v = (r256 // c) == (l256 // GLA_DV)
    qi = lax.broadcasted_iota(jnp.int32, (c, 4 * c), 0)
    kj = lax.broadcasted_iota(jnp.int32, (c, 4 * c), 1) % c
    zero_bf = jnp.zeros((), BF16)

    def local_group(it, carry):
        starts = [pl.multiple_of((it * grp + u) * c, c) for u in range(grp)]
        qq = [q_s[pl.ds(r0, c), :] for r0 in starts]
        vv = [v_s[pl.ds(r0, c), :] for r0 in starts]
        kexp = [jnp.where(head_k, jnp.concatenate([kin_s[pl.ds(r0, c), :]] * 4, axis=0), zero_bf) for r0 in starts]
        sc_f = [lax.dot_general(q[:, 0:128], k[:, 0:128], _NT, preferred_element_type=F32) for q, k in zip(qq, kexp)]
        sc_b = [lax.dot_general(q[:, 128:256], k[:, 128:256], _NT, preferred_element_type=F32) for q, k in zip(qq, kexp)]
        kv_t = [lax.dot_general(v, kst_s[pl.ds(r0, c), :], _TN, preferred_element_type=F32) for v, r0 in zip(vv, starts)]
        for u, r0 in enumerate(starts):
            vexp = jnp.where(head_v, jnp.concatenate([vv[u]] * 4, axis=0), zero_bf)
            pm = jnp.where(kj <= qi, sc_f[u], sc_b[u]).astype(BF16)
            o_s[pl.ds(r0, c), :] = _mm(pm, vexp)
            kv_s[it * grp + u] = jnp.where(head_k, kv_t[u], 0.0)
        return carry

    lax.fori_loop(0, nchunk // grp, local_group, 0)

    def carry_states(n, carry):
        st_f, st_b = carry
        m = nchunk - 1 - n
        st_s[n, :, 0:128] = st_f.astype(BF16)
        st_s[m, :, 128:256] = st_b.astype(BF16)
        st_f = jnp.exp2(totf_s[pl.ds(n * c, 1), :]) * st_f + kv_s[n, :, 0:128]
        st_b = jnp.exp2(totb_s[pl.ds(m * c, 1), :]) * st_b + kv_s[m, :, 128:256]
        return st_f, st_b

    zero = jnp.zeros((GLA_WIDTH, GLA_QK), F32)
    lax.fori_loop(0, nchunk, carry_states, (zero, zero))

    def cross_group(it, carry):
        starts = [pl.multiple_of((it * grp + u) * c, c) for u in range(grp)]
        upd = [lax.dot_general(q_s[pl.ds(r0, c), :], st_s[it * grp + u], _NT, preferred_element_type=F32)
               for u, r0 in enumerate(starts)]
        for r0, d in zip(starts, upd):
            o_s[pl.ds(r0, c), :] += d
        return carry

    lax.fori_loop(0, nchunk // grp, cross_group, 0)

    o = o_s[...]
    avg = avg_ref[...]
    mu = _split_mm(o, avg)
    dlt = o - mu
    var = _split_mm(dlt * dlt, avg)
    on = dlt * lax.rsqrt(var + LN_EPS)
    r = g_ref[0, :, 512:768]
    o_ref[0] = on * lng_ref[...] * (r * _sigmoid(r))


def _gla(g3, wa, ba, consts, ln_g):
    b, seq, _ = g3.shape
    pre, avg = consts
    nchunk = seq // GLA_CHUNK
    both = pltpu.VMEM((seq, 2 * GLA_QK), BF16)
    return pl.pallas_call(
        _gla_kernel,
        grid=(b,),
        in_specs=[pl.BlockSpec((1, seq, G_COLS), lambda i: (i, 0, 0)),
                  _resident((128, 2 * GLA_QK)), _resident((1, 2 * GLA_QK)),
                  _resident((256, 256)), _resident((256, 256)), _resident((1, 256))],
        out_specs=pl.BlockSpec((1, seq, GLA_WIDTH), lambda i: (i, 0, 0)),
        out_shape=jax.ShapeDtypeStruct((b, seq, GLA_WIDTH), F32),
        scratch_shapes=[both, both, both, pltpu.VMEM((seq, GLA_WIDTH), BF16),
                        pltpu.VMEM((seq, GLA_QK), F32), pltpu.VMEM((seq, GLA_QK), F32),
                        pltpu.VMEM((seq, GLA_WIDTH), F32),
                        pltpu.VMEM((nchunk, GLA_WIDTH, 2 * GLA_QK), F32),
                        pltpu.VMEM((nchunk, GLA_WIDTH, 2 * GLA_QK), BF16)],
        compiler_params=_params("parallel"),
        name="gla",
    )(g3, wa, ba, pre, avg, ln_g)


def _rope_tables(seq):
    pos = jnp.arange(seq, dtype=F32)
    inv_freq = ROPE_THETA ** (-jnp.arange(0, ROT_DIM, 2, dtype=F32) / ROT_DIM)
    ang = pos[:, None] * inv_freq[None, :]
    cos, sin = jnp.cos(ang), jnp.sin(ang)
    half = ROT_DIM // 2
    pad = jnp.zeros((seq, SWA_HEAD_DIM - ROT_DIM), F32)
    c_head = jnp.concatenate([cos, cos, pad + 1.0], axis=1)
    s_head = jnp.concatenate([sin, sin, pad], axis=1)
    two = lambda t: jnp.concatenate([t, t], axis=1)
    rot = np.zeros((128, 128), np.float32)
    for base in (0, SWA_HEAD_DIM):
        for d in range(half):
            rot[base + d + half, base + d] = -1.0
            rot[base + d, base + d + half] = 1.0
    return two(c_head), two(s_head), jnp.asarray(rot, BF16)


def _swa_kernel(sink_ref, a_ref, cos_ref, sin_ref, rot_ref, o_ref,
                q_s, klo_s, khi_s, vd_s, sa_s, sb_s, pa_s, pb_s, bias_s, ma_s, mb_s):
    seq = a_ref.shape[1]
    blk = SWA_BLOCK
    hd = SWA_HEAD_DIM
    pairs = SWA_GROUP // 2
    cosv, sinv, rot = cos_ref[...], sin_ref[...], rot_ref[...]

    def rope(x):
        return x * cosv + _split_mm(x, rot) * sinv

    for slab in range(SWA_WIDTH // 128):
        q_s[slab] = (rope(a_ref[0, :, slab * 128:(slab + 1) * 128]) * (hd ** -0.5 * LOG2E)).astype(BF16)
    kr = rope(a_ref[0, :, SWA_WIDTH:SWA_WIDTH + SWA_KV])
    vr = a_ref[0, :, SWA_WIDTH + SWA_KV:A_COLS]
    ksw = pltpu.roll(kr, hd, 1)
    vsw = pltpu.roll(vr, hd, 1)
    low = lax.broadcasted_iota(jnp.int32, (seq, 128), 1) < hd
    body = slice(blk, blk + seq)
    zpad = jnp.zeros((blk, 128), BF16)
    for kh in range(SWA_KV_HEADS):
        for ref in (klo_s, khi_s):
            ref[kh, 0:blk, :] = zpad
            ref[kh, blk + seq:2 * blk + seq, :] = zpad
        vd_s[kh, 0:blk, 0:128] = zpad
        vd_s[kh, blk + seq:2 * blk + seq, 0:128] = zpad
        vd_s[kh, :, 128:256] = jnp.ones((seq + 2 * blk, 128), BF16)
    klo_s[0, body, :] = jnp.where(low, kr, 0.0).astype(BF16)
    khi_s[0, body, :] = jnp.where(low, 0.0, ksw).astype(BF16)
    klo_s[1, body, :] = jnp.where(low, ksw, 0.0).astype(BF16)
    khi_s[1, body, :] = jnp.where(low, 0.0, kr).astype(BF16)
    vd_s[0, body, 0:128] = jnp.where(low, vr, vsw).astype(BF16)
    vd_s[1, body, 0:128] = jnp.where(low, vsw, vr).astype(BF16)

    qi = lax.broadcasted_iota(jnp.int32, (blk, 3 * blk), 0)
    kk = lax.broadcasted_iota(jnp.int32, (blk, 3 * blk), 1)
    band = jnp.abs(qi + blk - kk) <= SWA_WINDOW
    low_o = lax.broadcasted_iota(jnp.int32, (blk, 128), 1) < hd
    rows = SWA_GROUP * blk
    chunks = [(kh, rb) for kh in range(SWA_KV_HEADS) for rb in range(rows // SWA_ROWS)]

    def scores(n, s_s):
        r0 = pl.multiple_of(n * blk, blk)
        for kh in range(SWA_KV_HEADS):
            kcat = jnp.concatenate([klo_s[kh, pl.ds(r0, 3 * blk), :], khi_s[kh, pl.ds(r0, 3 * blk), :]], axis=0)
            for pr in range(pairs):
                qq = q_s[kh * pairs + pr, pl.ds(r0, blk), :]
                s2 = lax.dot_general(qq, kcat, _NT, preferred_element_type=F32)
                s_s[kh, (2 * pr) * blk:(2 * pr + 1) * blk, :] = s2[:, :3 * blk]
                s_s[kh, (2 * pr + 1) * blk:(2 * pr + 2) * blk, :] = s2[:, 3 * blk:]

    nblk = seq // blk

    def softmax_block(n, s_s, p_s, m_s):
        kpos = kk + (n - 1) * blk
        bias_s[...] = jnp.where(band & (kpos >= 0) & (kpos < seq), 0.0, NEG_BIG)
        for kh, rb in chunks:
            rs = slice(rb * SWA_ROWS, (rb + 1) * SWA_ROWS)
            bs = slice((rb * SWA_ROWS) % blk, (rb * SWA_ROWS) % blk + SWA_ROWS)
            sink2 = sink_ref[kh * SWA_GROUP + (rb * SWA_ROWS) // blk] * LOG2E
            lo = s_s[kh, rs, 0:blk] + bias_s[bs, 0:blk]
            hi = s_s[kh, rs, 2 * blk:3 * blk] + bias_s[bs, 2 * blk:3 * blk]
            s_s[kh, rs, 0:blk] = lo
            s_s[kh, rs, 2 * blk:3 * blk] = hi
            top = jnp.maximum(jnp.maximum(lo, s_s[kh, rs, blk:2 * blk]), hi)
            m_s[kh, rs, :] = jnp.broadcast_to(jnp.maximum(jnp.max(top, axis=1, keepdims=True), sink2), (SWA_ROWS, 128))
        for kh, rb in chunks:
            rs = slice(rb * SWA_ROWS, (rb + 1) * SWA_ROWS)
            p = jnp.exp2(s_s[kh, rs, :] - jnp.concatenate([m_s[kh, rs, :]] * 3, axis=1))
            p_s[kh, rs, :] = p.astype(BF16)

    def output_block(n, p_s, m_s):
        r0 = pl.multiple_of(n * blk, blk)
        for kh in range(SWA_KV_HEADS):
            o3 = _mm(p_s[kh], vd_s[kh, pl.ds(r0, 3 * blk), :])
            sink2 = jnp.concatenate([jnp.full((blk, 128), sink_ref[kh * SWA_GROUP + g] * LOG2E, F32)
                                     for g in range(SWA_GROUP)], axis=0)
            o2 = o3[:, 0:128] / (o3[:, 128:256] + jnp.exp2(sink2 - m_s[kh]))
            o_ref[0, pl.ds(r0, blk), kh * SWA_GROUP * hd:(kh + 1) * SWA_GROUP * hd] = jnp.concatenate(
                [jnp.where(low_o, o2[(2 * pr) * blk:(2 * pr + 1) * blk], o2[(2 * pr + 1) * blk:(2 * pr + 2) * blk])
                 for pr in range(pairs)], axis=1)

    def body_fn(t, carry):
        n = 2 * t + 1
        output_block(n - 1, pa_s, ma_s)
        scores(n + 1, sa_s)
        softmax_block(n, sb_s, pb_s, mb_s)
        output_block(n, pb_s, mb_s)
        scores(jnp.minimum(n + 2, nblk - 1), sb_s)
        softmax_block(n + 1, sa_s, pa_s, ma_s)
        return carry

    scores(0, sa_s)
    scores(1, sb_s)
    softmax_block(0, sa_s, pa_s, ma_s)
    lax.fori_loop(0, (nblk - 2) // 2, body_fn, 0)
    softmax_block(nblk - 1, sb_s, pb_s, mb_s)
    output_block(nblk - 2, pa_s, ma_s)
    output_block(nblk - 1, pb_s, mb_s)


def _swa(a3, sink, tables):
    b, seq, _ = a3.shape
    cos_t, sin_t, rot = tables
    tab = pl.BlockSpec((seq, 128), lambda i, s: (0, 0))
    rot_spec = pl.BlockSpec((128, 128), lambda i, s: (0, 0))
    banded = pltpu.VMEM((SWA_KV_HEADS, seq + 2 * SWA_BLOCK, 128), BF16)
    stat = pltpu.VMEM((SWA_KV_HEADS, SWA_GROUP * SWA_BLOCK, 128), F32)
    grid_spec = pltpu.PrefetchScalarGridSpec(
        num_scalar_prefetch=1,
        grid=(b,),
        in_specs=[pl.BlockSpec((1, seq, A_COLS), lambda i, s: (i, 0, 0)), tab, tab, rot_spec],
        out_specs=pl.BlockSpec((1, seq, SWA_WIDTH), lambda i, s: (i, 0, 0)),
        scratch_shapes=[pltpu.VMEM((SWA_WIDTH // 128, seq, 128), BF16), banded, banded,
                        pltpu.VMEM((SWA_KV_HEADS, seq + 2 * SWA_BLOCK, 256), BF16),
                        pltpu.VMEM((SWA_KV_HEADS, SWA_GROUP * SWA_BLOCK, 3 * SWA_BLOCK), F32),
                        pltpu.VMEM((SWA_KV_HEADS, SWA_GROUP * SWA_BLOCK, 3 * SWA_BLOCK), F32),
                        pltpu.VMEM((SWA_KV_HEADS, SWA_GROUP * SWA_BLOCK, 3 * SWA_BLOCK), BF16),
                        pltpu.VMEM((SWA_KV_HEADS, SWA_GROUP * SWA_BLOCK, 3 * SWA_BLOCK), BF16),
                        pltpu.VMEM((SWA_BLOCK, 3 * SWA_BLOCK), F32), stat, stat],
    )
    return pl.pallas_call(
        _swa_kernel,
        grid_spec=grid_spec,
        out_shape=jax.ShapeDtypeStruct((b, seq, SWA_WIDTH), F32),
        compiler_params=_params("parallel"),
        name="swa",
    )(sink, a3, cos_t, sin_t, rot)


def _layer_norm(x, g, b):
    mu = jnp.mean(x, axis=-1, keepdims=True)
    d = x - mu
    var = jnp.mean(d * d, axis=-1, keepdims=True)
    return d * lax.rsqrt(var + LN_EPS) * g + b


def _out_ffn_kernel(x_ref, yal_ref, yah_ref, yb_ref, yc_ref, wo_ref,
                    g1_ref, b1_ref, w1_ref, w2_ref, g2_ref, b2_ref, o_ref, acc_ref, x1b_ref, *, alpha):
    subs = [slice(r, r + FFN_SUB) for r in range(0, x_ref.shape[0], FFN_SUB)]
    for rows in subs:
        ya = jnp.concatenate([yal_ref[rows, :], yah_ref[rows, :]], axis=1)
        mix = (_mm(ya.astype(BF16), wo_ref[0:S5_WIDTH, :])
               + _mm(yb_ref[rows, :].astype(BF16), wo_ref[S5_WIDTH:S5_WIDTH + GLA_WIDTH, :])
               + _mm(yc_ref[rows, :].astype(BF16), wo_ref[S5_WIDTH + GLA_WIDTH:D_MODEL, :]))
        x1 = _layer_norm(alpha * x_ref[rows, :] + mix, g1_ref[...], b1_ref[...])
        o_ref[rows, :] = x1
        x1b_ref[rows, :] = x1.astype(BF16)
    for rows in subs:
        for ci in range(D_FF // FF_CHUNK):
            sl = slice(ci * FF_CHUNK, (ci + 1) * FF_CHUNK)
            hid = jnp.maximum(_mm(x1b_ref[rows, :], w1_ref[:, sl]), 0.0)
            part = _mm((hid * hid).astype(BF16), w2_ref[sl, :])
            if ci == 0:
                acc_ref[rows, :] = part
            else:
                acc_ref[rows, :] += part
    for rows in subs:
        o_ref[rows, :] = _layer_norm(alpha * o_ref[rows, :] + acc_ref[rows, :], g2_ref[...], b2_ref[...])


def _out_ffn(x2d, yal2d, yah2d, yb2d, yc2d, wo, g1, b1, w1, w2, g2, b2, alpha):
    t = x2d.shape[0]
    tm = min(FFN_TILE, t)
    row = lambda n: pl.BlockSpec((tm, n), lambda i: (i, 0))
    once = lambda shape: pl.BlockSpec(shape, lambda i: (0, 0), pipeline_mode=pl.Buffered(1))
    return pl.pallas_call(
        functools.partial(_out_ffn_kernel, alpha=alpha),
        grid=(t // tm,),
        in_specs=[row(D_MODEL), row(128), row(128), row(GLA_WIDTH), row(SWA_WIDTH),
                  once((D_MODEL, D_MODEL)), once((1, D_MODEL)), once((1, D_MODEL)),
                  once((D_MODEL, D_FF)), once((D_FF, D_MODEL)),
                  once((1, D_MODEL)), once((1, D_MODEL))],
        out_specs=row(D_MODEL),
        out_shape=jax.ShapeDtypeStruct((t, D_MODEL), F32),
        scratch_shapes=[pltpu.VMEM((tm, D_MODEL), F32), pltpu.VMEM((tm, D_MODEL), BF16)],
        compiler_params=_params("parallel"),
        name="out_ffn",
    )(x2d, yal2d, yah2d, yb2d, yc2d, wo, g1, b1, w1, w2, g2, b2)


def _pack_w_in(w_in):
    points = np.cumsum([S5_WIDTH, GLA_QK, GLA_QK, GLA_WIDTH, GLA_WIDTH, GLA_RANK, GLA_RANK,
                        SWA_WIDTH, SWA_KV]).tolist()
    s5, gq, gk, gv, gr, glf, glb, aq, ak, av = jnp.split(w_in, points, axis=-1)
    pad = jnp.zeros(w_in.shape[:-1] + (128 - 2 * GLA_RANK,), w_in.dtype)
    return jnp.concatenate([s5, gq, gk, gv, gr, glf, glb, pad, aq, ak, av], axis=-1).astype(BF16)


def _pad_gate_w(w_a):
    z = jnp.zeros((128, 2 * GLA_QK), F32)
    z = z.at[0:GLA_RANK, 0:GLA_QK].set(w_a[0].astype(F32))
    z = z.at[GLA_RANK:2 * GLA_RANK, GLA_QK:2 * GLA_QK].set(w_a[1].astype(F32))
    return z.astype(BF16)


def _layer(x2d, bsz, seq, alpha, w_packed, a_re, a_im, log_step, b_re, b_im, c_re, c_im, d_skip,
           w_glu, b_glu, gla_w_a, gla_b_a, gla_ln_g, swa_sink, w_out, ln1_g, ln1_b, w_ff1, w_ff2,
           ln2_g, ln2_b, gla_consts, rope_tabs):
    t = bsz * seq
    row = lambda v: v.astype(F32)[None, :]
    ul, uh, g, a = _inproj(x2d, w_packed)

    mats = _s5_matrices(a_re, a_im, log_step, b_re, b_im, c_re, c_im, d_skip, seq // S5_CHUNK)
    yal, yah = _s5(ul.reshape(bsz, seq, 128), uh.reshape(bsz, seq, 128), mats, w_glu, row(b_glu))

    yb = _gla(g.reshape(bsz, seq, G_COLS), _pad_gate_w(gla_w_a), gla_b_a.astype(F32).reshape(1, 2 * GLA_QK),
              gla_consts, row(gla_ln_g))
    yc = _swa(a.reshape(bsz, seq, A_COLS), swa_sink.astype(F32), rope_tabs)

    return _out_ffn(x2d, yal.reshape(t, 128), yah.reshape(t, 128), yb.reshape(t, GLA_WIDTH), yc.reshape(t, SWA_WIDTH),
                    w_out, row(ln1_g), row(ln1_b), w_ff1, w_ff2, row(ln2_g), row(ln2_b), alpha)


def kernel(x, w_in, s5_a_re, s5_a_im, s5_log_step, s5_b_re, s5_b_im, s5_c_re, s5_c_im, s5_d, s5_w_glu, s5_b_glu, gla_w_a, gla_b_a, gla_ln_g, swa_sink, w_out, ln1_g, ln1_b, w_ff1, w_ff2, ln2_g, ln2_b):
    bsz, seq, _ = x.shape
    depth = w_in.shape[0]
    alpha = (2 * depth) ** 0.25
    gla_consts = _gla_constants()
    rope_tabs = _rope_tables(seq)
    w_packed = _pack_w_in(w_in)
    s5_w_glu, w_out, w_ff1, w_ff2 = (w.astype(BF16) for w in (s5_w_glu, w_out, w_ff1, w_ff2))
    h = x.reshape(bsz * seq, D_MODEL)
    for l in range(depth):
        h = _layer(h, bsz, seq, alpha, w_packed[l], s5_a_re[l], s5_a_im[l], s5_log_step[l],
                   s5_b_re[l], s5_b_im[l], s5_c_re[l], s5_c_im[l], s5_d[l], s5_w_glu[l], s5_b_glu[l],
                   gla_w_a[l], gla_b_a[l], gla_ln_g[l], swa_sink[l], w_out[l], ln1_g[l], ln1_b[l],
                   w_ff1[l], w_ff2[l], ln2_g[l], ln2_b[l], gla_consts, rope_tabs)
    return h.reshape(bsz, seq, D_MODEL)
```

```python
import functools
import math

import jax
import jax.numpy as jnp
import numpy as np
from jax import lax
from jax.experimental import pallas as pl
from jax.experimental.pallas import tpu as pltpu

F32 = jnp.float32
BF16 = jnp.bfloat16

D_MODEL = 1024
S5_WIDTH = 256
S5_GROUP = 16
S5_GROUPS = 16
S5_STATE = 64
S5_CHUNK = 16
GLA_HEADS = 4
GLA_DK = 32
GLA_DV = 64
GLA_QK = 128
GLA_WIDTH = 256
GLA_RANK = 16
GLA_TAU = 16.0
GLA_CHUNK = 64
GLA_GROUP = 32
SWA_WIDTH = 512
SWA_HEAD_DIM = 64
SWA_HEADS = 8
SWA_KV_HEADS = 2
SWA_GROUP = SWA_HEADS // SWA_KV_HEADS
SWA_KV = 128
SWA_WINDOW = 128
SWA_BLOCK = 128
ROT_DIM = 16
ROPE_THETA = 500000.0
D_FF = 4096
LN_EPS = 1e-5
NEG_BIG = -1e30
LOG2E = 1.4426950408889634

U_COLS = S5_WIDTH
G_COLS = GLA_QK + GLA_QK + GLA_WIDTH + GLA_WIDTH + 128
A_COLS = SWA_WIDTH + SWA_KV + SWA_KV
IN_COLS = U_COLS + G_COLS + A_COLS

VMEM_LIMIT = 56 * 1024 * 1024
ROW_TILE = 1024
FF_CHUNK = 512
FFN_TILE = 1024
FFN_SUB = 512
SWA_ROWS = 64

_NT = (((1,), (1,)), ((), ()))
_TN = (((0,), (0,)), ((), ()))


def _mm(a, b):
    return jnp.dot(a, b, preferred_element_type=F32)


def _params(*sem):
    return pltpu.CompilerParams(dimension_semantics=sem, vmem_limit_bytes=VMEM_LIMIT)


def _resident(shape):
    nd = len(shape)
    return pl.BlockSpec(shape, lambda *_: (0,) * nd)


def _inproj_kernel(x_ref, w_ref, ul_ref, uh_ref, g_ref, a_ref):
    x = x_ref[...].astype(BF16)
    u = _mm(x, w_ref[:, 0:U_COLS])
    ul_ref[...] = u[:, :128]
    uh_ref[...] = u[:, 128:]
    g_ref[...] = _mm(x, w_ref[:, U_COLS:U_COLS + G_COLS])
    a_ref[...] = _mm(x, w_ref[:, U_COLS + G_COLS:IN_COLS])


def _inproj(x2d, w_packed):
    t = x2d.shape[0]
    tm = min(ROW_TILE, t)
    row = lambda n: pl.BlockSpec((tm, n), lambda i: (i, 0))
    return pl.pallas_call(
        _inproj_kernel,
        grid=(t // tm,),
        in_specs=[row(D_MODEL), _resident((D_MODEL, IN_COLS))],
        out_specs=[row(128), row(128), row(G_COLS), row(A_COLS)],
        out_shape=[jax.ShapeDtypeStruct((t, 128), F32), jax.ShapeDtypeStruct((t, 128), F32),
                   jax.ShapeDtypeStruct((t, G_COLS), F32), jax.ShapeDtypeStruct((t, A_COLS), F32)],
        compiler_params=_params("parallel"),
        name="inproj",
    )(x2d, w_packed)


def _s5_matrices(a_re, a_im, log_step, b_re, b_im, c_re, c_im, d_skip, nch):
    c, ng, w = S5_CHUNK, S5_GROUPS, S5_CHUNK * S5_GROUP
    ar, ai = a_re.astype(F32), a_im.astype(F32)
    step = jnp.exp(log_step.astype(F32))
    lr, li = ar * step, ai * step

    def lam_pow(z, k):
        kk = jnp.asarray(k, F32)[None, :, None, None]
        mag = jnp.exp(kk * lr[z][:, None, None, :])
        ang = kk * li[z][:, None, None, :]
        return mag * jnp.cos(ang), mag * jnp.sin(ang)

    def cmul(xr, xi, yr, yi):
        return xr * yr - xi * yi, xr * yi + xi * yr

    l1r, l1i = jnp.exp(lr) * jnp.cos(li), jnp.exp(lr) * jnp.sin(li)
    nr, ni = l1r - 1.0, l1i
    den = ar * ar + ai * ai
    qr, qi = (nr * ar + ni * ai) / den, (ni * ar - nr * ai) / den
    bt_r = jnp.swapaxes(b_re.astype(F32), -1, -2)
    bt_i = jnp.swapaxes(b_im.astype(F32), -1, -2)
    bbr, bbi = cmul(qr[:, :, None, :], qi[:, :, None, :], bt_r, bt_i)
    cr, ci = c_re.astype(F32), c_im.astype(F32)

    idx = np.arange(c)
    flat = lambda t: t.reshape(ng, w, S5_STATE)

    def with_c(z, k):
        pr, pi = lam_pow(z, k)
        xr, xi = cmul(pr, pi, cr[z][:, None], ci[z][:, None])
        return flat(xr), flat(xi)

    def with_b(z, k):
        pr, pi = lam_pow(z, k)
        yr, yi = cmul(pr, pi, bbr[z][:, None], bbi[z][:, None])
        return flat(yr), flat(yi)

    def re_prod(x, y):
        return jnp.einsum('gmk,gnk->gmn', jnp.concatenate([x[0], -x[1]], axis=-1),
                          jnp.concatenate([y[0], y[1]], axis=-1), precision=lax.Precision.HIGHEST)

    ri = np.arange(w)[:, None] // S5_GROUP
    cj = np.arange(w)[None, :] // S5_GROUP
    tt_mat = (jnp.where(ri >= cj, re_prod(with_c(0, idx), with_b(0, -idx)), 0.0)
              + jnp.where(cj >= ri, re_prod(with_c(1, -idx), with_b(1, idx)), 0.0))

    sf, sb = with_b(0, c - 1 - idx), with_b(1, idx)
    s_mat = jnp.concatenate([sf[0], sb[0], sf[1], sb[1]], axis=-1)
    of, ob = with_c(0, idx + 1), with_c(1, c - idx)
    ot_mat = jnp.concatenate([of[0], ob[0], -of[1], -ob[1]], axis=-1)

    pcr = [lam_pow(z, [c]) for z in range(2)]
    a_r = jnp.concatenate([pcr[0][0][:, 0, 0, :], pcr[1][0][:, 0, 0, :]], axis=-1)
    a_i = jnp.concatenate([pcr[0][1][:, 0, 0, :], pcr[1][1][:, 0, 0, :]], axis=-1)
    d_col = jnp.broadcast_to(jnp.tile(d_skip.astype(F32), (1, c))[:, :, None], (ng, w, nch))
    return tt_mat.astype(BF16), s_mat.astype(BF16), ot_mat.astype(BF16), a_r, a_i, d_col


def _gelu_tanh(x):
    return 0.5 * x * (1.0 + jnp.tanh(math.sqrt(2.0 / math.pi) * (x + 0.044715 * (x * x * x))))


def _sigmoid(x):
    return 1.0 / (1.0 + jnp.exp(-x))


def _s5_kernel(ul_ref, uh_ref, tt_ref, s_ref, ot_ref, ar_ref, ai_ref, dcol_ref, wglu_ref, bglu_ref,
               yl_ref, yh_ref, d_scr, cr_scr, ci_scr, er_scr, ei_scr, zt_scr):
    c, ng = S5_CHUNK, S5_GROUPS
    nch = ul_ref.shape[1] // c
    for j in range(c):
        tok = pl.ds(j, nch, stride=c)
        uj = jnp.concatenate([ul_ref[0, tok, :], uh_ref[0, tok, :]], axis=1)
        d_scr[:, j * S5_GROUP:(j + 1) * S5_GROUP, :] = uj.T.reshape(ng, S5_GROUP, nch)
    contrib = [lax.dot_general(d_scr[g].astype(BF16), s_ref[g], _TN, preferred_element_type=F32)
               for g in range(ng)]
    for g in range(ng):
        grp = pl.ds(g, nch, stride=ng)
        cr_scr[grp, :] = contrib[g][:, :128]
        ci_scr[grp, :] = contrib[g][:, 128:]

    ar, ai = ar_ref[...], ai_ref[...]
    fwd = lax.broadcasted_iota(jnp.int32, (ng, 128), 1) < S5_STATE

    def step(k, carry):
        sre, sim = carry
        up = pl.ds(pl.multiple_of(k * ng, ng), ng)
        dn = pl.ds(pl.multiple_of((nch - 1 - k) * ng, ng), ng)
        er_scr[up, 0:S5_STATE] = sre[:, 0:S5_STATE]
        ei_scr[up, 0:S5_STATE] = sim[:, 0:S5_STATE]
        er_scr[dn, S5_STATE:] = sre[:, S5_STATE:]
        ei_scr[dn, S5_STATE:] = sim[:, S5_STATE:]
        cre = jnp.where(fwd, cr_scr[up, :], cr_scr[dn, :])
        cim = jnp.where(fwd, ci_scr[up, :], ci_scr[dn, :])
        return ar * sre - ai * sim + cre, ar * sim + ai * sre + cim

    zero = jnp.zeros((ng, 128), F32)
    lax.fori_loop(0, nch, step, (zero, zero), unroll=8)

    enter = []
    for g in range(ng):
        grp = pl.ds(g, nch, stride=ng)
        enter.append(jnp.concatenate([er_scr[grp, :], ei_scr[grp, :]], axis=1).astype(BF16))
    yt = [_mm(tt_ref[g], d_scr[g].astype(BF16)) + lax.dot_general(ot_ref[g], enter[g], _NT, preferred_element_type=F32)
          for g in range(ng)]
    for g in range(ng):
        zt_scr[:, g * S5_GROUP:(g + 1) * S5_GROUP, :] = _gelu_tanh(yt[g] + dcol_ref[g] * d_scr[g]).reshape(c, S5_GROUP, nch)
    glu = [lax.dot_general(zt_scr[i].astype(BF16), wglu_ref[...], _TN, preferred_element_type=F32)
           for i in range(c)]
    for i in range(c):
        gl = glu[i] + bglu_ref[...]
        ya = gl[:, :S5_WIDTH] * _sigmoid(gl[:, S5_WIDTH:])
        tok = pl.ds(i, nch, stride=c)
        yl_ref[0, tok, :] = ya[:, :128]
        yh_ref[0, tok, :] = ya[:, 128:]


def _s5(ul3, uh3, mats, wglu, bglu):
    b, seq, _ = ul3.shape
    nch = seq // S5_CHUNK
    tt_mat, s_mat, ot_mat, a_r, a_i, d_col = mats
    w = S5_CHUNK * S5_GROUP
    half = pl.BlockSpec((1, seq, 128), lambda i: (i, 0, 0))
    state_scr = pltpu.VMEM((S5_GROUPS * nch, 2 * S5_STATE), F32)
    return pl.pallas_call(
        _s5_kernel,
        grid=(b,),
        in_specs=[half, half,
                  _resident((S5_GROUPS, w, w)), _resident((S5_GROUPS, w, 4 * S5_STATE)),
                  _resident((S5_GROUPS, w, 4 * S5_STATE)),
                  _resident((S5_GROUPS, 128)), _resident((S5_GROUPS, 128)),
                  _resident((S5_GROUPS, w, nch)),
                  _resident((S5_WIDTH, 2 * S5_WIDTH)), _resident((1, 2 * S5_WIDTH))],
        out_specs=[half, half],
        out_shape=[jax.ShapeDtypeStruct((b, seq, 128), F32)] * 2,
        scratch_shapes=[pltpu.VMEM((S5_GROUPS, w, nch), F32)] + [state_scr] * 4
                       + [pltpu.VMEM((S5_CHUNK, w, nch), F32)],
        compiler_params=_params("parallel"),
        name="s5",
    )(ul3, uh3, tt_mat, s_mat, ot_mat, a_r, a_i, d_col, wglu, bglu)


def _gla_constants():
    r = np.arange(256)
    same = (r[:, None] // GLA_CHUNK) == (r[None, :] // GLA_CHUNK)
    prefix = same & (r[None, :] <= r[:, None])
    avg = ((r[:, None] // GLA_DV) == (r[None, :] // GLA_DV)) / float(GLA_DV)
    as_bf = lambda m: jnp.asarray(m.astype(np.float32), BF16)
    return as_bf(prefix), as_bf(avg)


def _log_sigmoid(x):
    return jnp.minimum(x, 0.0) - jnp.log(1.0 + jnp.exp(-jnp.abs(x)))


def _split_mm(a, b):
    hi = a.astype(BF16)
    lo = (a - hi.astype(F32)).astype(BF16)
    return _mm(hi, b) + _mm(lo, b)


def _chunk_sums(mats, la):
    hi = la.astype(BF16)
    lo = (la - hi.astype(F32)).astype(BF16)
    outs = []
    for mat in mats:
        blocks = []
        for i in range(la.shape[0] // 256):
            sl = slice(i * 256, (i + 1) * 256)
            blocks.append(_mm(mat, hi[sl]) + _mm(mat, lo[sl]))
        outs.append(jnp.concatenate(blocks, axis=0))
    return outs


def _chunk_row(x, row):
    x3 = x.reshape(x.shape[0] // GLA_CHUNK, GLA_CHUNK, x.shape[1])
    return jnp.broadcast_to(x3[:, row:row + 1, :], x3.shape).reshape(x.shape)


def _gla_kernel(g_ref, wa_ref, ba_ref, pre_ref, avg_ref, lng_ref, o_ref,
                q_s, kin_s, kst_s, v_s, totf_s, totb_s, o_s, kv_s, st_s):
    seq = g_ref.shape[1]
    nchunk = seq // GLA_CHUNK
    c = GLA_CHUNK
    grp = math.gcd(GLA_GROUP, nchunk)
    q = g_ref[0, :, 0:128] * (GLA_DK ** -0.5)
    k = g_ref[0, :, 128:256]
    lfb = g_ref[0, :, 768:896].astype(BF16)
    la = _log_sigmoid(_mm(lfb, wa_ref[...]) + ba_ref[...]) * (LOG2E / GLA_TAU)
    (pre,) = _chunk_sums((pre_ref[...],), la)
    tot = _chunk_row(pre, c - 1)
    bf, tf = pre[:, 0:128], tot[:, 0:128]
    q_s[:, 0:128] = (q * jnp.exp2(bf)).astype(BF16)
    kin_s[:, 0:128] = (k * jnp.exp2(-bf)).astype(BF16)
    kst_s[:, 0:128] = (k * jnp.exp2(tf - bf)).astype(BF16)
    totf_s[...] = tf
    tb = tot[:, 128:256]
    bs = tb - pre[:, 128:256] + la[:, 128:256]
    q_s[:, 128:256] = (q * jnp.exp2(bs)).astype(BF16)
    kin_s[:, 128:256] = (k * jnp.exp2(-bs)).astype(BF16)
    kst_s[:, 128:256] = (k * jnp.exp2(tb - bs)).astype(BF16)
    totb_s[...] = tb
    v_s[...] = g_ref[0, :, 256:512].astype(BF16)

    r256 = lax.broadcasted_iota(jnp.int32, (4 * c, 256), 0)
    l256 = lax.broadcasted_iota(jnp.int32, (4 * c, 256), 1)
    head_k = (r256 // c) == ((l256 % 128) // GLA_DK)
    head_v = (r256 // c) == (l256 // GLA_DV)
    qi = lax.broadcasted_iota(jnp.int32, (c, 4 * c), 0)
    kj = lax.broadcasted_iota(jnp.int32, (c, 4 * c), 1) % c
    zero_bf = jnp.zeros((), BF16)

    def local_group(it, carry):
        starts = [pl.multiple_of((it * grp + u) * c, c) for u in range(grp)]
        qq = [q_s[pl.ds(r0, c), :] for r0 in starts]
        vv = [v_s[pl.ds(r0, c), :] for r0 in starts]
        kexp = [jnp.where(head_k, jnp.concatenate([kin_s[pl.ds(r0, c), :]] * 4, axis=0), zero_bf) for r0 in starts]
        sc_f = [lax.dot_general(q[:, 0:128], k[:, 0:128], _NT, preferred_element_type=F32) for q, k in zip(qq, kexp)]
        sc_b = [lax.dot_general(q[:, 128:256], k[:, 128:256], _NT, preferred_element_type=F32) for q, k in zip(qq, kexp)]
        kv_t = [lax.dot_general(v, kst_s[pl.ds(r0, c), :], _TN, preferred_element_type=F32) for v, r0 in zip(vv, starts)]
        for u, r0 in enumerate(starts):
            vexp = jnp.where(head_v, jnp.concatenate([vv[u]] * 4, axis=0), zero_bf)
            pm = jnp.where(kj <= qi, sc_f[u], sc_b[u]).astype(BF16)
            o_s[pl.ds(r0, c), :] = _mm(pm, vexp)
            kv_s[it * grp + u] = jnp.where(head_k, kv_t[u], 0.0)
        return carry

    lax.fori_loop(0, nchunk // grp, local_group, 0)

    def carry_states(n, carry):
        st_f, st_b = carry
        m = nchunk - 1 - n
        st_s[n, :, 0:128] = st_f.astype(BF16)
        st_s[m, :, 128:256] = st_b.astype(BF16)
        st_f = jnp.exp2(totf_s[pl.ds(n * c, 1), :]) * st_f + kv_s[n, :, 0:128]
        st_b = jnp.exp2(totb_s[pl.ds(m * c, 1), :]) * st_b + kv_s[m, :, 128:256]
        return st_f, st_b

    zero = jnp.zeros((GLA_WIDTH, GLA_QK), F32)
    lax.fori_loop(0, nchunk, carry_states, (zero, zero), unroll=True)

    def cross_group(it, carry):
        starts = [pl.multiple_of((it * grp + u) * c, c) for u in range(grp)]
        upd = [lax.dot_general(q_s[pl.ds(r0, c), :], st_s[it * grp + u], _NT, preferred_element_type=F32)
               for u, r0 in enumerate(starts)]
        for r0, d in zip(starts, upd):
            o_s[pl.ds(r0, c), :] += d
        return carry

    lax.fori_loop(0, nchunk // grp, cross_group, 0)

    o = o_s[...]
    avg = avg_ref[...]
    mu = _split_mm(o, avg)
    dlt = o - mu
    var = _split_mm(dlt * dlt, avg)
    on = dlt * lax.rsqrt(var + LN_EPS)
    r = g_ref[0, :, 512:768]
    o_ref[0] = on * lng_ref[...] * (r * _sigmoid(r))


def _gla(g3, wa, ba, consts, ln_g):
    b, seq, _ = g3.shape
    pre, avg = consts
    nchunk = seq // GLA_CHUNK
    both = pltpu.VMEM((seq, 2 * GLA_QK), BF16)
    return pl.pallas_call(
        _gla_kernel,
        grid=(b,),
        in_specs=[pl.BlockSpec((1, seq, G_COLS), lambda i: (i, 0, 0)),
                  _resident((128, 2 * GLA_QK)), _resident((1, 2 * GLA_QK)),
                  _resident((256, 256)), _resident((256, 256)), _resident((1, 256))],
        out_specs=pl.BlockSpec((1, seq, GLA_WIDTH), lambda i: (i, 0, 0)),
        out_shape=jax.ShapeDtypeStruct((b, seq, GLA_WIDTH), F32),
        scratch_shapes=[both, both, both, pltpu.VMEM((seq, GLA_WIDTH), BF16),
                        pltpu.VMEM((seq, GLA_QK), F32), pltpu.VMEM((seq, GLA_QK), F32),
                        pltpu.VMEM((seq, GLA_WIDTH), F32),
                        pltpu.VMEM((nchunk, GLA_WIDTH, 2 * GLA_QK), F32),
                        pltpu.VMEM((nchunk, GLA_WIDTH, 2 * GLA_QK), BF16)],
        compiler_params=_params("parallel"),
        name="gla",
    )(g3, wa, ba, pre, avg, ln_g)


def _rope_tables(seq):
    pos = jnp.arange(seq, dtype=F32)
    inv_freq = ROPE_THETA ** (-jnp.arange(0, ROT_DIM, 2, dtype=F32) / ROT_DIM)
    ang = pos[:, None] * inv_freq[None, :]
    cos, sin = jnp.cos(ang), jnp.sin(ang)
    half = ROT_DIM // 2
    pad = jnp.zeros((seq, SWA_HEAD_DIM - ROT_DIM), F32)
    c_head = jnp.concatenate([cos, cos, pad + 1.0], axis=1)
    s_head = jnp.concatenate([sin, sin, pad], axis=1)
    two = lambda t: jnp.concatenate([t, t], axis=1)
    rot = np.zeros((128, 128), np.float32)
    for base in (0, SWA_HEAD_DIM):
        for d in range(half):
            rot[base + d + half, base + d] = -1.0
            rot[base + d, base + d + half] = 1.0
    return two(c_head), two(s_head), jnp.asarray(rot, BF16)


def _swa_kernel(sink_ref, a_ref, cos_ref, sin_ref, rot_ref, o_ref,
                q_s, klo_s, khi_s, vd_s, sa_s, sb_s, pa_s, pb_s, bias_s, ma_s, mb_s):
    seq = a_ref.shape[1]
    blk = SWA_BLOCK
    hd = SWA_HEAD_DIM
    pairs = SWA_GROUP // 2
    cosv, sinv, rot = cos_ref[...], sin_ref[...], rot_ref[...]

    def rope(x):
        return x * cosv + _split_mm(x, rot) * sinv

    for slab in range(SWA_WIDTH // 128):
        q_s[slab] = (rope(a_ref[0, :, slab * 128:(slab + 1) * 128]) * (hd ** -0.5 * LOG2E)).astype(BF16)
    kr = rope(a_ref[0, :, SWA_WIDTH:SWA_WIDTH + SWA_KV])
    vr = a_ref[0, :, SWA_WIDTH + SWA_KV:A_COLS]
    ksw = pltpu.roll(kr, hd, 1)
    vsw = pltpu.roll(vr, hd, 1)
    low = lax.broadcasted_iota(jnp.int32, (seq, 128), 1) < hd
    body = slice(blk, blk + seq)
    zpad = jnp.zeros((blk, 128), BF16)
    for kh in range(SWA_KV_HEADS):
        for ref in (klo_s, khi_s):
            ref[kh, 0:blk, :] = zpad
            ref[kh, blk + seq:2 * blk + seq, :] = zpad
        vd_s[kh, 0:blk, 0:128] = zpad
        vd_s[kh, blk + seq:2 * blk + seq, 0:128] = zpad
        vd_s[kh, :, 128:256] = jnp.ones((seq + 2 * blk, 128), BF16)
    klo_s[0, body, :] = jnp.where(low, kr, 0.0).astype(BF16)
    khi_s[0, body, :] = jnp.where(low, 0.0, ksw).astype(BF16)
    klo_s[1, body, :] = jnp.where(low, ksw, 0.0).astype(BF16)
    khi_s[1, body, :] = jnp.where(low, 0.0, kr).astype(BF16)
    vd_s[0, body, 0:128] = jnp.where(low, vr, vsw).astype(BF16)
    vd_s[1, body, 0:128] = jnp.where(low, vsw, vr).astype(BF16)

    qi = lax.broadcasted_iota(jnp.int32, (blk, 3 * blk), 0)
    kk = lax.broadcasted_iota(jnp.int32, (blk, 3 * blk), 1)
    band = jnp.abs(qi + blk - kk) <= SWA_WINDOW
    low_o = lax.broadcasted_iota(jnp.int32, (blk, 128), 1) < hd
    rows = SWA_GROUP * blk
    chunks = [(kh, rb) for kh in range(SWA_KV_HEADS) for rb in range(rows // SWA_ROWS)]

    def scores(n, s_s):
        r0 = pl.multiple_of(n * blk, blk)
        for kh in range(SWA_KV_HEADS):
            kcat = jnp.concatenate([klo_s[kh, pl.ds(r0, 3 * blk), :], khi_s[kh, pl.ds(r0, 3 * blk), :]], axis=0)
            for pr in range(pairs):
                qq = q_s[kh * pairs + pr, pl.ds(r0, blk), :]
                s2 = lax.dot_general(qq, kcat, _NT, preferred_element_type=F32)
                s_s[kh, (2 * pr) * blk:(2 * pr + 1) * blk, :] = s2[:, :3 * blk]
                s_s[kh, (2 * pr + 1) * blk:(2 * pr + 2) * blk, :] = s2[:, 3 * blk:]

    nblk = seq // blk

    def softmax_block(n, s_s, p_s, m_s):
        kpos = kk + (n - 1) * blk
        bias_s[...] = jnp.where(band & (kpos >= 0) & (kpos < seq), 0.0, NEG_BIG)
        for kh, rb in chunks:
            rs = slice(rb * SWA_ROWS, (rb + 1) * SWA_ROWS)
            bs = slice((rb * SWA_ROWS) % blk, (rb * SWA_ROWS) % blk + SWA_ROWS)
            sink2 = sink_ref[kh * SWA_GROUP + (rb * SWA_ROWS) // blk] * LOG2E
            lo = s_s[kh, rs, 0:blk] + bias_s[bs, 0:blk]
            hi = s_s[kh, rs, 2 * blk:3 * blk] + bias_s[bs, 2 * blk:3 * blk]
            s_s[kh, rs, 0:blk] = lo
            s_s[kh, rs, 2 * blk:3 * blk] = hi
            top = jnp.maximum(jnp.maximum(lo, s_s[kh, rs, blk:2 * blk]), hi)
            m_s[kh, rs, :] = jnp.broadcast_to(jnp.maximum(jnp.max(top, axis=1, keepdims=True), sink2), (SWA_ROWS, 128))
        for kh, rb in chunks:
            rs = slice(rb * SWA_ROWS, (rb + 1) * SWA_ROWS)
            p = jnp.exp2(s_s[kh, rs, :] - jnp.concatenate([m_s[kh, rs, :]] * 3, axis=1))
            p_s[kh, rs, :] = p.astype(BF16)

    def output_block(n, p_s, m_s):
        r0 = pl.multiple_of(n * blk, blk)
        for kh in range(SWA_KV_HEADS):
            o3 = _mm(p_s[kh], vd_s[kh, pl.ds(r0, 3 * blk), :])
            sink2 = jnp.concatenate([jnp.full((blk, 128), sink_ref[kh * SWA_GROUP + g] * LOG2E, F32)
                                     for g in range(SWA_GROUP)], axis=0)
            o2 = o3[:, 0:128] / (o3[:, 128:256] + jnp.exp2(sink2 - m_s[kh]))
            o_ref[0, pl.ds(r0, blk), kh * SWA_GROUP * hd:(kh + 1) * SWA_GROUP * hd] = jnp.concatenate(
                [jnp.where(low_o, o2[(2 * pr) * blk:(2 * pr + 1) * blk], o2[(2 * pr + 1) * blk:(2 * pr + 2) * blk])
                 for pr in range(pairs)], axis=1)

    def body_fn(t, carry):
        n = 2 * t + 1
        output_block(n - 1, pa_s, ma_s)
        scores(n + 1, sa_s)
        softmax_block(n, sb_s, pb_s, mb_s)
        output_block(n, pb_s, mb_s)
        scores(jnp.minimum(n + 2, nblk - 1), sb_s)
        softmax_block(n + 1, sa_s, pa_s, ma_s)
        return carry

    scores(0, sa_s)
    scores(1, sb_s)
    softmax_block(0, sa_s, pa_s, ma_s)
    lax.fori_loop(0, (nblk - 2) // 2, body_fn, 0)
    softmax_block(nblk - 1, sb_s, pb_s, mb_s)
    output_block(nblk - 2, pa_s, ma_s)
    output_block(nblk - 1, pb_s, mb_s)


def _swa(a3, sink, tables):
    b, seq, _ = a3.shape
    cos_t, sin_t, rot = tables
    tab = pl.BlockSpec((seq, 128), lambda i, s: (0, 0))
    rot_spec = pl.BlockSpec((128, 128), lambda i, s: (0, 0))
    banded = pltpu.VMEM((SWA_KV_HEADS, seq + 2 * SWA_BLOCK, 128), BF16)
    stat = pltpu.VMEM((SWA_KV_HEADS, SWA_GROUP * SWA_BLOCK, 128), F32)
    grid_spec = pltpu.PrefetchScalarGridSpec(
        num_scalar_prefetch=1,
        grid=(b,),
        in_specs=[pl.BlockSpec((1, seq, A_COLS), lambda i, s: (i, 0, 0)), tab, tab, rot_spec],
        out_specs=pl.BlockSpec((1, seq, SWA_WIDTH), lambda i, s: (i, 0, 0)),
        scratch_shapes=[pltpu.VMEM((SWA_WIDTH // 128, seq, 128), BF16), banded, banded,
                        pltpu.VMEM((SWA_KV_HEADS, seq + 2 * SWA_BLOCK, 256), BF16),
                        pltpu.VMEM((SWA_KV_HEADS, SWA_GROUP * SWA_BLOCK, 3 * SWA_BLOCK), F32),
                        pltpu.VMEM((SWA_KV_HEADS, SWA_GROUP * SWA_BLOCK, 3 * SWA_BLOCK), F32),
                        pltpu.VMEM((SWA_KV_HEADS, SWA_GROUP * SWA_BLOCK, 3 * SWA_BLOCK), BF16),
                        pltpu.VMEM((SWA_KV_HEADS, SWA_GROUP * SWA_BLOCK, 3 * SWA_BLOCK), BF16),
                        pltpu.VMEM((SWA_BLOCK, 3 * SWA_BLOCK), F32), stat, stat],
    )
    return pl.pallas_call(
        _swa_kernel,
        grid_spec=grid_spec,
        out_shape=jax.ShapeDtypeStruct((b, seq, SWA_WIDTH), F32),
        compiler_params=_params("parallel"),
        name="swa",
    )(sink, a3, cos_t, sin_t, rot)


def _layer_norm(x, g, b):
    mu = jnp.mean(x, axis=-1, keepdims=True)
    d = x - mu
    var = jnp.mean(d * d, axis=-1, keepdims=True)
    return d * lax.rsqrt(var + LN_EPS) * g + b


def _out_ffn_kernel(x_ref, yal_ref, yah_ref, yb_ref, yc_ref, wo_ref,
                    g1_ref, b1_ref, w1_ref, w2_ref, g2_ref, b2_ref, o_ref, acc_ref, x1b_ref, *, alpha):
    subs = [slice(r, r + FFN_SUB) for r in range(0, x_ref.shape[0], FFN_SUB)]
    for rows in subs:
        ya = jnp.concatenate([yal_ref[rows, :], yah_ref[rows, :]], axis=1)
        mix = (_mm(ya.astype(BF16), wo_ref[0:S5_WIDTH, :])
               + _mm(yb_ref[rows, :].astype(BF16), wo_ref[S5_WIDTH:S5_WIDTH + GLA_WIDTH, :])
               + _mm(yc_ref[rows, :].astype(BF16), wo_ref[S5_WIDTH + GLA_WIDTH:D_MODEL, :]))
        x1 = _layer_norm(alpha * x_ref[rows, :] + mix, g1_ref[...], b1_ref[...])
        o_ref[rows, :] = x1
        x1b_ref[rows, :] = x1.astype(BF16)
    for rows in subs:
        for ci in range(D_FF // FF_CHUNK):
            sl = slice(ci * FF_CHUNK, (ci + 1) * FF_CHUNK)
            hid = jnp.maximum(_mm(x1b_ref[rows, :], w1_ref[:, sl]), 0.0)
            part = _mm((hid * hid).astype(BF16), w2_ref[sl, :])
            if ci == 0:
                acc_ref[rows, :] = part
            else:
                acc_ref[rows, :] += part
    for rows in subs:
        o_ref[rows, :] = _layer_norm(alpha * o_ref[rows, :] + acc_ref[rows, :], g2_ref[...], b2_ref[...])


def _out_ffn(x2d, yal2d, yah2d, yb2d, yc2d, wo, g1, b1, w1, w2, g2, b2, alpha):
    t = x2d.shape[0]
    tm = min(FFN_TILE, t)
    row = lambda n: pl.BlockSpec((tm, n), lambda i: (i, 0))
    once = lambda shape: pl.BlockSpec(shape, lambda i: (0, 0), pipeline_mode=pl.Buffered(1))
    return pl.pallas_call(
        functools.partial(_out_ffn_kernel, alpha=alpha),
        grid=(t // tm,),
        in_specs=[row(D_MODEL), row(128), row(128), row(GLA_WIDTH), row(SWA_WIDTH),
                  once((D_MODEL, D_MODEL)), once((1, D_MODEL)), once((1, D_MODEL)),
                  once((D_MODEL, D_FF)), once((D_FF, D_MODEL)),
                  once((1, D_MODEL)), once((1, D_MODEL))],
        out_specs=row(D_MODEL),
        out_shape=jax.ShapeDtypeStruct((t, D_MODEL), F32),
        scratch_shapes=[pltpu.VMEM((tm, D_MODEL), F32), pltpu.VMEM((tm, D_MODEL), BF16)],
        compiler_params=_params("parallel"),
        name="out_ffn",
    )(x2d, yal2d, yah2d, yb2d, yc2d, wo, g1, b1, w1, w2, g2, b2)


def _pack_w_in(w_in):
    points = np.cumsum([S5_WIDTH, GLA_QK, GLA_QK, GLA_WIDTH, GLA_WIDTH, GLA_RANK, GLA_RANK,
                        SWA_WIDTH, SWA_KV]).tolist()
    s5, gq, gk, gv, gr, glf, glb, aq, ak, av = jnp.split(w_in, points, axis=-1)
    pad = jnp.zeros(w_in.shape[:-1] + (128 - 2 * GLA_RANK,), w_in.dtype)
    return jnp.concatenate([s5, gq, gk, gv, gr, glf, glb, pad, aq, ak, av], axis=-1).astype(BF16)


def _pad_gate_w(w_a):
    z = jnp.zeros((128, 2 * GLA_QK), F32)
    z = z.at[0:GLA_RANK, 0:GLA_QK].set(w_a[0].astype(F32))
    z = z.at[GLA_RANK:2 * GLA_RANK, GLA_QK:2 * GLA_QK].set(w_a[1].astype(F32))
    return z.astype(BF16)


def _layer(x2d, bsz, seq, alpha, w_packed, a_re, a_im, log_step, b_re, b_im, c_re, c_im, d_skip,
           w_glu, b_glu, gla_w_a, gla_b_a, gla_ln_g, swa_sink, w_out, ln1_g, ln1_b, w_ff1, w_ff2,
           ln2_g, ln2_b, gla_consts, rope_tabs):
    t = bsz * seq
    row = lambda v: v.astype(F32)[None, :]
    ul, uh, g, a = _inproj(x2d, w_packed)

    mats = _s5_matrices(a_re, a_im, log_step, b_re, b_im, c_re, c_im, d_skip, seq // S5_CHUNK)
    yal, yah = _s5(ul.reshape(bsz, seq, 128), uh.reshape(bsz, seq, 128), mats, w_glu, row(b_glu))

    yb = _gla(g.reshape(bsz, seq, G_COLS), _pad_gate_w(gla_w_a), gla_b_a.astype(F32).reshape(1, 2 * GLA_QK),
              gla_consts, row(gla_ln_g))
    yc = _swa(a.reshape(bsz, seq, A_COLS), swa_sink.astype(F32), rope_tabs)

    return _out_ffn(x2d, yal.reshape(t, 128), yah.reshape(t, 128), yb.reshape(t, GLA_WIDTH), yc.reshape(t, SWA_WIDTH),
                    w_out, row(ln1_g), row(ln1_b), w_ff1, w_ff2, row(ln2_g), row(ln2_b), alpha)


def kernel(x, w_in, s5_a_re, s5_a_im, s5_log_step, s5_b_re, s5_b_im, s5_c_re, s5_c_im, s5_d, s5_w_glu, s5_b_glu, gla_w_a, gla_b_a, gla_ln_g, swa_sink, w_out, ln1_g, ln1_b, w_ff1, w_ff2, ln2_g, ln2_b):
    bsz, seq, _ = x.shape
    depth = w_in.shape[0]
    alpha = (2 * depth) ** 0.25
    gla_consts = _gla_constants()
    rope_tabs = _rope_tables(seq)
    w_packed = _pack_w_in(w_in)
    s5_w_glu, w_out, w_ff1, w_ff2 = (w.astype(BF16) for w in (s5_w_glu, w_out, w_ff1, w_ff2))
    h = x.reshape(bsz * seq, D_MODEL)
    for l in range(depth):
        h = _layer(h, bsz, seq, alpha, w_packed[l], s5_a_re[l], s5_a_im[l], s5_log_step[l],
                   s5_b_re[l], s5_b_im[l], s5_c_re[l], s5_c_im[l], s5_d[l], s5_w_glu[l], s5_b_glu[l],
                   gla_w_a[l], gla_b_a[l], gla_ln_g[l], swa_sink[l], w_out[l], ln1_g[l], ln1_b[l],
                   w_ff1[l], w_ff2[l], ln2_g[l], ln2_b[l], gla_consts, rope_tabs)
    return h.reshape(bsz, seq, D_MODEL)
```
